```python
import jax, jax.numpy as jnp
from jax import lax
import numpy as np

D_MODEL = 1024
BATCH = 4
SEQ = 4096
DEPTH = 4
DEC_BATCH = 8
DEC_SEQ = 8192
PAST_LEN = 128

N_EVEN = (DEPTH + 1) // 2
N_ODD = DEPTH // 2
NORM_EPS = 1e-6
CHUNK = 64
Q_BLOCK = 128

GLA_HEADS = 4
GLA_DK = D_MODEL // 8
GLA_DV = D_MODEL // 4
GLA_LR = 16
GLA_GATE_NORM = 16.0
GLA_K_TOT = GLA_HEADS * GLA_DK
GLA_V_TOT = GLA_HEADS * GLA_DV

POOL_GROUPS = 4
POOL_WINDOWS = (2, 4, 8, 16)
POOL_DG = D_MODEL // 8
POOL_W = POOL_GROUPS * POOL_DG

MLA_HEADS = 8
MLA_NOPE = D_MODEL // 16
MLA_ROPE = D_MODEL // 32
MLA_DV = D_MODEL // 16
MLA_Q_LORA = 3 * D_MODEL // 8
MLA_KV_LORA = D_MODEL // 4
MLA_W = MLA_HEADS * MLA_DV
ROPE_THETA = 10000.0

ML_HEADS = 4
ML_DH = D_MODEL // 8
ML_W = ML_HEADS * ML_DH

E_SPLITS = (GLA_K_TOT, GLA_K_TOT, GLA_V_TOT, GLA_V_TOT, 2 * GLA_LR, POOL_W, POOL_W)
E_COLS = sum(E_SPLITS)
E_OUT = GLA_V_TOT + POOL_W
O_SPLITS = (MLA_Q_LORA, MLA_KV_LORA, MLA_ROPE, MLA_W, ML_W, ML_W, ML_W, ML_W, 4 * ML_HEADS, ML_W)
O_COLS = sum(O_SPLITS)
O_OUT = MLA_W + ML_W

kernel_name = 'hybrid_bidir_gla_pool_mla_mlstm'

F32 = jnp.float32


def rmsnorm(x, g):
    xf = x.astype(F32)
    y = xf * lax.rsqrt(jnp.mean(xf * xf, axis=-1, keepdims=True) + NORM_EPS) * g.astype(F32)
    return y.astype(x.dtype)


def head_rmsnorm(o, g):
    y = o * lax.rsqrt(jnp.mean(o * o, axis=-1, keepdims=True) + NORM_EPS) * g.astype(F32)
    B, H, S, d = y.shape
    return y.transpose(0, 2, 1, 3).reshape(B, S, H * d)


def split_cols(z, sizes):
    idx = [int(i) for i in np.cumsum(sizes)[:-1]]
    return jnp.split(z, idx, axis=-1)


def to_heads(t, n_heads):
    B, S, _ = t.shape
    return t.reshape(B, S, n_heads, -1).transpose(0, 2, 1, 3)


def flip_seq(t):
    return jnp.flip(t, axis=2)


def chunk_first(t):
    B, H, S = t.shape[:3]
    t = t.reshape(B, H, S // CHUNK, CHUNK, *t.shape[3:])
    return jnp.moveaxis(t, 2, 0)


def chunk_last(t):
    t = jnp.moveaxis(t, 0, 2)
    B, H, N, L = t.shape[:4]
    return t.reshape(B, H, N * L, *t.shape[4:])


def gla_scan(q, k, v, log_a, inclusive):
    B, H, S, dk = q.shape
    dv = v.shape[-1]
    mask = jnp.tril(jnp.ones((CHUNK, CHUNK), bool), 0 if inclusive else -1)

    def step(state, inp):
        qc, kc, vc, ac = inp
        b = jnp.cumsum(ac, axis=-2)
        b_last = b[..., -1:, :]
        qe = qc * jnp.exp(b)
        a = jnp.einsum('bhid,bhjd->bhij', qe, kc * jnp.exp(-b))
        a = jnp.where(mask, a, 0.0)
        o = jnp.einsum('bhij,bhjv->bhiv', a, vc) + jnp.einsum('bhid,bhdv->bhiv', qe, state)
        state = (jnp.exp(b_last)[..., 0, :, None] * state
                 + jnp.einsum('bhjd,bhjv->bhdv', kc * jnp.exp(b_last - b), vc))
        return state, o

    init = jnp.zeros((B, H, dk, dv), F32)
    _, o = lax.scan(step, init, (chunk_first(q), chunk_first(k), chunk_first(v), chunk_first(log_a)))
    return chunk_last(o)


def mlstm_scan(q, k, v, log_i, log_f, inclusive):
    B, H, S, dk = q.shape
    dv = v.shape[-1]
    mask = jnp.tril(jnp.ones((CHUNK, CHUNK), bool), 0 if inclusive else -1)

    def step(carry, inp):
        C, n, m = carry
        qc, kc, vc, li, lf = inp
        b = jnp.cumsum(lf, axis=-1)
        g = b[..., -1]
        d = jnp.where(mask, b[..., :, None] - b[..., None, :] + li[..., None, :], -jnp.inf)
        inter = b + m[..., None]
        m_t = jnp.maximum(inter, jnp.max(d, axis=-1))
        s = jnp.einsum('bhid,bhjd->bhij', qc, kc) * jnp.exp(d - m_t[..., None])
        e = jnp.exp(inter - m_t)
        num = jnp.einsum('bhij,bhjv->bhiv', s, vc) + e[..., None] * jnp.einsum('bhid,bhdv->bhiv', qc, C)
        den = jnp.sum(s, axis=-1) + e * jnp.einsum('bhid,bhd->bhi', qc, n)
        h = num / jnp.maximum(jnp.abs(den), jnp.exp(-m_t))[..., None]
        lw = g[..., None] - b + li
        m_new = jnp.maximum(g + m, jnp.max(lw, axis=-1))
        wk = jnp.exp(lw - m_new[..., None])
        decay = jnp.exp(g + m - m_new)
        C = decay[..., None, None] * C + jnp.einsum('bhj,bhjd,bhjv->bhdv', wk, kc, vc)
        n = decay[..., None] * n + jnp.einsum('bhj,bhjd->bhd', wk, kc)
        return (C, n, m_new), h

    init = (jnp.zeros((B, H, dk, dv), F32), jnp.zeros((B, H, dk), F32), jnp.zeros((B, H), F32))
    _, h = lax.scan(step, init, (chunk_first(q), chunk_first(k), chunk_first(v),
                                 chunk_first(log_i), chunk_first(log_f)))
    return chunk_last(h)


def multiscale_pool(u, pool_w, pool_scale):
    B, S, _ = u.shape
    ug = u.astype(F32).reshape(B, S, POOL_GROUPS, POOL_DG)
    cs = jnp.concatenate([jnp.zeros_like(ug[:, :1]), jnp.cumsum(ug, axis=1)], axis=1)
    pos = jnp.arange(S)
    outs = []
    for gi, w in enumerate(POOL_WINDOWS):
        lo = jnp.clip(pos - w // 2, 0, S)
        hi = jnp.clip(pos + w // 2, 0, S)
        csg = cs[:, :, gi]
        win_sum = jnp.take(csg, hi, axis=1) - jnp.take(csg, lo, axis=1)
        cnt = (hi - lo).astype(F32)[None, :, None]
        outs.append(win_sum / cnt - ug[:, :, gi])
    pooled = jnp.stack(outs, axis=2)
    mixed = jnp.einsum('bsgc,gcd->bsgd', pooled, pool_w.astype(F32)).reshape(B, S, POOL_W)
    return mixed * pool_scale.astype(F32)


def rope_tables(S):
    inv = ROPE_THETA ** (-jnp.arange(0, MLA_ROPE, 2, dtype=F32) / MLA_ROPE)
    ang = jnp.arange(S, dtype=F32)[:, None] * inv[None, :]
    return jnp.cos(ang), jnp.sin(ang)


def apply_rope(x, cos, sin):
    xf = x.astype(F32)
    x1, x2 = jnp.split(xf, 2, axis=-1)
    return jnp.concatenate([x1 * cos - x2 * sin, x2 * cos + x1 * sin], axis=-1).astype(x.dtype)


def mla_attention(q_nope, q_rope, k_nope, k_rope, v):
    B, S, H, _ = q_nope.shape
    scale = (MLA_NOPE + MLA_ROPE) ** -0.5
    nb = S // Q_BLOCK

    def blocks(t):
        return jnp.moveaxis(t.reshape(B, nb, Q_BLOCK, *t.shape[2:]), 1, 0)

    def attend(qs):
        qn, qr = qs
        s = (jnp.einsum('bqhd,bkhd->bhqk', qn, k_nope, preferred_element_type=F32)
             + jnp.einsum('bqhr,bkr->bhqk', qr, k_rope, preferred_element_type=F32)) * scale
        p = jax.nn.softmax(s, axis=-1)
        return jnp.einsum('bhqk,bkhd->bqhd', p.astype(v.dtype), v)

    o = lax.map(attend, (blocks(q_nope), blocks(q_rope)))
    return jnp.moveaxis(o, 0, 1).reshape(B, S, H * MLA_DV)


def even_layer(h, w_in, a_up, a_bias, gla_norm_g, pool_w, pool_scale, w_out):
    B, S, _ = h.shape
    z = h @ w_in
    q, k, v, gla_gate, a_lr, pool_u, pool_gate = split_cols(z, E_SPLITS)
    q = to_heads(q, GLA_HEADS).astype(F32) * GLA_DK ** -0.5
    k = to_heads(k, GLA_HEADS).astype(F32)
    v = to_heads(v, GLA_HEADS).astype(F32)
    lr_f, lr_b = jnp.split(a_lr, 2, axis=-1)
    log_a_f = jax.nn.log_sigmoid((lr_f @ a_up[0] + a_bias[0]).astype(F32)) / GLA_GATE_NORM
    log_a_b = jax.nn.log_sigmoid((lr_b @ a_up[1] + a_bias[1]).astype(F32)) / GLA_GATE_NORM
    log_a_f = to_heads(log_a_f, GLA_HEADS)
    log_a_b = to_heads(log_a_b, GLA_HEADS)
    o = (gla_scan(q, k, v, log_a_f, True)
         + flip_seq(gla_scan(flip_seq(q), flip_seq(k), flip_seq(v), flip_seq(log_a_b), False)))
    gla_out = head_rmsnorm(o, gla_norm_g).astype(h.dtype) * jax.nn.silu(gla_gate)
    pool_out = multiscale_pool(pool_u, pool_w, pool_scale).astype(h.dtype) * jax.nn.silu(pool_gate)
    return jnp.concatenate([gla_out, pool_out], axis=-1) @ w_out


def odd_layer(h, cos, sin, w_in, q_norm_g, q_up, kv_norm_g, kv_up, if_bias, ml_norm_g, w_out):
    B, S, _ = h.shape
    z = h @ w_in
    cq, ckv, k_rope, mla_gate, mq, mk, mv, mo, mif, ml_gate = split_cols(z, O_SPLITS)
    qh = (rmsnorm(cq, q_norm_g) @ q_up).reshape(B, S, MLA_HEADS, MLA_NOPE + MLA_ROPE)
    q_nope, q_rope = qh[..., :MLA_NOPE], qh[..., MLA_NOPE:]
    kvh = (rmsnorm(ckv, kv_norm_g) @ kv_up).reshape(B, S, MLA_HEADS, MLA_NOPE + MLA_DV)
    k_nope, v_mla = kvh[..., :MLA_NOPE], kvh[..., MLA_NOPE:]
    q_rope = apply_rope(q_rope, cos[:, None, :], sin[:, None, :])
    k_rope = apply_rope(k_rope, cos, sin)
    mla_out = mla_attention(q_nope, q_rope, k_nope, k_rope, v_mla) * jax.nn.silu(mla_gate)
    q = to_heads(mq, ML_HEADS).astype(F32)
    k = to_heads(mk, ML_HEADS).astype(F32) * ML_DH ** -0.5
    v = to_heads(mv, ML_HEADS).astype(F32)
    gates = (mif.astype(F32) + if_bias.astype(F32)).reshape(B, S, 4, ML_HEADS).transpose(2, 0, 3, 1)
    li_f, li_b = gates[0], gates[1]
    lf_f, lf_b = jax.nn.log_sigmoid(gates[2]), jax.nn.log_sigmoid(gates[3])
    hm = (mlstm_scan(q, k, v, li_f, lf_f, True)
          + flip_seq(mlstm_scan(flip_seq(q), flip_seq(k), flip_seq(v), flip_seq(li_b), flip_seq(lf_b), False)))
    ml_out = head_rmsnorm(hm, ml_norm_g) * jax.nn.sigmoid(mo.astype(F32))
    ml_out = ml_out.astype(h.dtype) * jax.nn.silu(ml_gate)
    return jnp.concatenate([mla_out.astype(h.dtype), ml_out], axis=-1) @ w_out


def trunk(x, norm_g, final_norm_g, e_w_in, e_gla_a_up, e_gla_a_bias, e_gla_norm_g, e_pool_w,
          e_pool_scale, e_w_out, o_w_in, o_q_norm_g, o_q_up, o_kv_norm_g, o_kv_up, o_if_bias,
          o_mlstm_norm_g, o_w_out):
    S = x.shape[1]
    cos, sin = rope_tables(S)
    for layer in range(DEPTH):
        h = rmsnorm(x, norm_g[layer])
        i = layer // 2
        if layer % 2 == 0:
            y = even_layer(h, e_w_in[i], e_gla_a_up[i], e_gla_a_bias[i], e_gla_norm_g[i],
                           e_pool_w[i], e_pool_scale[i], e_w_out[i])
        else:
            y = odd_layer(h, cos, sin, o_w_in[i], o_q_norm_g[i], o_q_up[i], o_kv_norm_g[i], o_kv_up[i],
                          o_if_bias[i], o_mlstm_norm_g[i], o_w_out[i])
        x = x + y.astype(x.dtype)
    return rmsnorm(x, final_norm_g)


def setup_inputs(seed: int = 0) -> dict:
    key = jax.random.key(seed)
    ks = jax.random.split(key, 24)

    def nrm(k, shape, scale):
        return jax.random.normal(k, shape, F32) * scale

    f_bias = jnp.tile(jnp.linspace(3.0, 6.0, ML_HEADS), 2)
    if_bias = jnp.concatenate([
        jnp.broadcast_to(0.1 * jax.random.normal(ks[20], (1, 2 * ML_HEADS), F32), (N_ODD, 2 * ML_HEADS)),
        f_bias[None, :] + 0.1 * jax.random.normal(ks[21], (N_ODD, 2 * ML_HEADS), F32)], axis=-1)
    if_bias = if_bias + 0.01 * jax.random.normal(ks[22], if_bias.shape, F32)
    return {
        'x_prompt': nrm(ks[0], (BATCH, SEQ, D_MODEL), 1.0),
        'x_sample': nrm(ks[1], (DEC_BATCH, DEC_SEQ, D_MODEL), 1.0),
        'norm_g': 1.0 + nrm(ks[2], (DEPTH, D_MODEL), 0.02),
        'final_norm_g': 1.0 + nrm(ks[3], (D_MODEL,), 0.02),
        'e_w_in': nrm(ks[4], (N_EVEN, D_MODEL, E_COLS), D_MODEL ** -0.5),
        'e_gla_a_up': nrm(ks[5], (N_EVEN, 2, GLA_LR, GLA_K_TOT), GLA_LR ** -0.5),
        'e_gla_a_bias': nrm(ks[6], (N_EVEN, 2, GLA_K_TOT), 0.1),
        'e_gla_norm_g': 1.0 + nrm(ks[7], (N_EVEN, GLA_DV), 0.02),
        'e_pool_w': nrm(ks[8], (N_EVEN, POOL_GROUPS, POOL_DG, POOL_DG), POOL_DG ** -0.5),
        'e_pool_scale': 1.0 + nrm(ks[9], (N_EVEN, POOL_W), 0.1),
        'e_w_out': nrm(ks[10], (N_EVEN, E_OUT, D_MODEL), E_OUT ** -0.5),
        'o_w_in': nrm(ks[11], (N_ODD, D_MODEL, O_COLS), D_MODEL ** -0.5),
        'o_q_norm_g': 1.0 + nrm(ks[12], (N_ODD, MLA_Q_LORA), 0.02),
        'o_q_up': nrm(ks[13], (N_ODD, MLA_Q_LORA, MLA_HEADS * (MLA_NOPE + MLA_ROPE)), MLA_Q_LORA ** -0.5),
        'o_kv_norm_g': 1.0 + nrm(ks[14], (N_ODD, MLA_KV_LORA), 0.02),
        'o_kv_up': nrm(ks[15], (N_ODD, MLA_KV_LORA, MLA_HEADS * (MLA_NOPE + MLA_DV)), MLA_KV_LORA ** -0.5),
        'o_if_bias': if_bias,
        'o_mlstm_norm_g': 1.0 + nrm(ks[16], (N_ODD, ML_DH), 0.02),
        'o_w_out': nrm(ks[17], (N_ODD, O_OUT, D_MODEL), O_OUT ** -0.5),
    }


def reference(x_prompt, x_sample, norm_g, final_norm_g, e_w_in, e_gla_a_up, e_gla_a_bias, e_gla_norm_g,
              e_pool_w, e_pool_scale, e_w_out, o_w_in, o_q_norm_g, o_q_up, o_kv_norm_g, o_kv_up,
              o_if_bias, o_mlstm_norm_g, o_w_out):
    y_prompt = trunk(x_prompt, norm_g, final_norm_g, e_w_in, e_gla_a_up, e_gla_a_bias, e_gla_norm_g,
                     e_pool_w, e_pool_scale, e_w_out, o_w_in, o_q_norm_g, o_q_up, o_kv_norm_g, o_kv_up,
                     o_if_bias, o_mlstm_norm_g, o_w_out)
    y_sample = trunk(x_sample, norm_g, final_norm_g, e_w_in, e_gla_a_up, e_gla_a_bias, e_gla_norm_g,
                     e_pool_w, e_pool_scale, e_w_out, o_w_in, o_q_norm_g, o_q_up, o_kv_norm_g, o_kv_up,
                     o_if_bias, o_mlstm_norm_g, o_w_out)
    return (y_prompt, y_sample)
```

```python
import functools
import math

import jax
import jax.numpy as jnp
from jax import lax
from jax.experimental import pallas as pl
from jax.experimental.pallas import tpu as pltpu

F32 = jnp.float32
BF16 = jnp.bfloat16

D_MODEL = 1024
NORM_EPS = 1e-6
CHUNK = 64

GLA_HEADS = 4
GLA_DK = 128
GLA_DV = 256
GLA_LR = 16
GLA_GATE_NORM = 16.0
GLA_K_TOT = GLA_HEADS * GLA_DK
GLA_V_TOT = GLA_HEADS * GLA_DV

POOL_GROUPS = 4
POOL_WINDOWS = (2, 4, 8, 16)
POOL_DG = 128
POOL_W = POOL_GROUPS * POOL_DG
POOL_HALO = 8

MLA_HEADS = 8
MLA_NOPE = 64
MLA_ROPE = 32
MLA_DV = 64
MLA_Q_LORA = 384
MLA_KV_LORA = 256
MLA_W = MLA_HEADS * MLA_DV
MLA_HEAD_PAD = 128
MLA_VT_ROWS = 80
ROPE_THETA = 10000.0

ML_HEADS = 4
ML_DH = 128
ML_W = ML_HEADS * ML_DH

VMEM_LIMIT_BYTES = 56 * 1024 * 1024

TOKEN_TILE = 512
SEQ_TILE = 512
ATT_TQ = 512
ATT_TQ_SUB = 256
ATT_TK = 512


def _cparams(sem):
    return pltpu.CompilerParams(dimension_semantics=sem, vmem_limit_bytes=VMEM_LIMIT_BYTES)


def _const_spec(shape):
    nd = len(shape)
    return pl.BlockSpec(shape, lambda *_: (0,) * nd)


def _dot(a, b):
    return jnp.dot(a, b, preferred_element_type=F32)


def _dot_nt(a, b):
    return lax.dot_general(a, b, (((1,), (1,)), ((), ())), preferred_element_type=F32)


def _dot_tn(a, b):
    return lax.dot_general(a, b, (((0,), (0,)), ((), ())), preferred_element_type=F32)


def _sigmoid(x):
    return 1.0 / (1.0 + jnp.exp(-x))


def _silu(x):
    return x * _sigmoid(x)


def _log_sigmoid(x):
    return jnp.minimum(x, 0.0) - jnp.log1p(jnp.exp(-jnp.abs(x)))


def _rms(x, g):
    return x * lax.rsqrt(jnp.mean(x * x, axis=-1, keepdims=True) + NORM_EPS) * g


def _split3(x):
    hi = x.astype(BF16)
    r1 = x - hi.astype(F32)
    mid = r1.astype(BF16)
    lo = (r1 - mid.astype(F32)).astype(BF16)
    return hi, mid, lo


def _tri_left(tri, x):
    hi, mid, lo = _split3(x)
    return _dot(tri, hi) + _dot(tri, mid) + _dot(tri, lo)


def _tri_right(x, tri):
    hi, mid, lo = _split3(x)
    return _dot(hi, tri) + _dot(mid, tri) + _dot(lo, tri)


def _norm_proj_kernel(n_out, x_ref, g_ref, *refs):
    w_refs, o_refs = refs[:n_out], refs[n_out:]
    h = _rms(x_ref[...], g_ref[...]).astype(BF16)
    for w_ref, o_ref in zip(w_refs, o_refs):
        o_ref[...] = _dot(h, w_ref[...]).astype(o_ref.dtype)


def _norm_proj(x2d, g, weights, out_dtypes):
    t = x2d.shape[0]
    tm = TOKEN_TILE
    n_out = len(weights)
    in_specs = [pl.BlockSpec((tm, D_MODEL), lambda i: (i, 0)), _const_spec((1, D_MODEL))]
    in_specs += [_const_spec(w.shape) for w in weights]
    out_specs = [pl.BlockSpec((tm, w.shape[1]), lambda i: (i, 0)) for w in weights]
    out_shape = [jax.ShapeDtypeStruct((t, w.shape[1]), dt) for w, dt in zip(weights, out_dtypes)]
    return pl.pallas_call(
        functools.partial(_norm_proj_kernel, n_out),
        grid=(t // tm,),
        in_specs=in_specs,
        out_specs=out_specs,
        out_shape=out_shape,
        compiler_params=_cparams(("parallel",)),
        name="norm_proj",
    )(x2d, g, *weights)


def _gla_chunk(q_ref, k_ref, v_ref, o_ref, st_ref, st_idx, row, b_all, inclusive):
    ii = lax.broadcasted_iota(jnp.int32, (CHUNK, CHUNK), 0)
    jj = lax.broadcasted_iota(jnp.int32, (CHUNK, CHUNK), 1)
    mask = (jj <= ii) if inclusive else (jj > ii)
    edge = CHUNK - 1 if inclusive else 0
    for h in range(GLA_HEADS):
        ks = slice(h * GLA_DK, (h + 1) * GLA_DK)
        vs = slice(h * GLA_DV, (h + 1) * GLA_DV)
        q = q_ref[0, pl.ds(row, CHUNK), ks] * (GLA_DK ** -0.5)
        k = k_ref[0, pl.ds(row, CHUNK), ks]
        v = v_ref[0, pl.ds(row, CHUNK), vs]
        b = b_all[:, ks]
        b_edge = b[edge:edge + 1, :]
        qe = (q * jnp.exp(b)).astype(BF16)
        kd = (k * jnp.exp(-b)).astype(BF16)
        a = jnp.where(mask, _dot_nt(qe, kd), 0.0).astype(BF16)
        st = st_ref[st_idx + h]
        o = _dot(a, v) + _dot_nt(qe, st.astype(BF16))
        o_ref[0, pl.ds(row, CHUNK), vs] = o
        kdec = (k * jnp.exp(b_edge - b)).astype(BF16)
        st_ref[st_idx + h] = st * jnp.exp(b_edge) + _dot_tn(v, kdec)


def _gla_kernel(qf, kf, vf, lrf, qb, kb, vb, lrb, aupf, aupb, biasf, biasb, tril, triu,
                of, ob, st_ref, la_ref):
    c = pl.program_id(1)

    @pl.when(c == 0)
    def _():
        st_ref[...] = jnp.zeros_like(st_ref)

    inv = 1.0 / GLA_GATE_NORM
    la_ref[0] = _log_sigmoid(_dot(lrf[0], aupf[...]) + biasf[...]) * inv
    la_ref[1] = _log_sigmoid(_dot(lrb[0], aupb[...]) + biasb[...]) * inv
    n_chunks = SEQ_TILE // CHUNK

    def body(j, carry):
        rf = pl.multiple_of(j * CHUNK, CHUNK)
        rb = pl.multiple_of((n_chunks - 1 - j) * CHUNK, CHUNK)
        b_f = _tri_left(tril[...], la_ref[0, pl.ds(rf, CHUNK), :])
        b_b = _tri_left(triu[...], la_ref[1, pl.ds(rb, CHUNK), :])
        _gla_chunk(qf, kf, vf, of, st_ref, 0, rf, b_f, True)
        _gla_chunk(qb, kb, vb, ob, st_ref, GLA_HEADS, rb, b_b, False)
        return carry

    lax.fori_loop(0, n_chunks, body, 0)


def _gla(q, k, v, lr_f, lr_b, a_up, a_bias):
    bsz, s, _ = q.shape
    ts = SEQ_TILE
    ns = s // ts
    fwd = lambda b, c: (b, c, 0)
    bwd = lambda b, c: (b, ns - 1 - c, 0)
    idx = jnp.arange(CHUNK)
    tril = (idx[None, :] <= idx[:, None]).astype(BF16)
    triu = (idx[None, :] >= idx[:, None]).astype(BF16)

    def seq_specs(imap):
        return [pl.BlockSpec((1, ts, GLA_K_TOT), imap), pl.BlockSpec((1, ts, GLA_K_TOT), imap),
                pl.BlockSpec((1, ts, GLA_V_TOT), imap), pl.BlockSpec((1, ts, GLA_LR), imap)]

    in_specs = seq_specs(fwd) + seq_specs(bwd) + [
        _const_spec((GLA_LR, GLA_K_TOT)), _const_spec((GLA_LR, GLA_K_TOT)),
        _const_spec((1, GLA_K_TOT)), _const_spec((1, GLA_K_TOT)),
        _const_spec((CHUNK, CHUNK)), _const_spec((CHUNK, CHUNK))]
    out_specs = [pl.BlockSpec((1, ts, GLA_V_TOT), fwd), pl.BlockSpec((1, ts, GLA_V_TOT), bwd)]
    out_shape = [jax.ShapeDtypeStruct((bsz, s, GLA_V_TOT), F32)] * 2
    return pl.pallas_call(
        _gla_kernel,
        grid=(bsz, ns),
        in_specs=in_specs,
        out_specs=out_specs,
        out_shape=out_shape,
        scratch_shapes=[pltpu.VMEM((2 * GLA_HEADS, GLA_DV, GLA_DK), F32),
                        pltpu.VMEM((2, ts, GLA_K_TOT), F32)],
        compiler_params=_cparams(("parallel", "arbitrary")),
        name="gla_scan",
    )(q, k, v, lr_f, q, k, v, lr_b,
      a_up[0].astype(BF16), a_up[1].astype(BF16), a_bias[0:1], a_bias[1:2], tril, triu)


def _even_out_kernel(seq_len, of, ob, gate, ng, pu, pprev, pnext, pgate, pw, pscale, wa, wb, x, o_ref):
    tm = of.shape[1]
    i = pl.program_id(1)
    n_i = pl.num_programs(1)

    o = of[0] + ob[0]
    g_all = gate[0]
    parts = []
    for h in range(GLA_HEADS):
        vs = slice(h * GLA_DV, (h + 1) * GLA_DV)
        parts.append((_rms(o[:, vs], ng[...]) * _silu(g_all[:, vs])).astype(BF16))
    gla_out = jnp.concatenate(parts, axis=-1)

    u = pu[0]
    prev = jnp.where(i > 0, pprev[0], 0.0)
    nxt = jnp.where(i < n_i - 1, pnext[0], 0.0)
    ext = jnp.concatenate([prev, u, nxt], axis=0)
    n_ext = tm + 2 * POOL_HALO
    pos = i * tm + lax.broadcasted_iota(jnp.int32, (tm, 1), 0)
    mixed = []
    for gi, w in enumerate(POOL_WINDOWS):
        cs = slice(gi * POOL_DG, (gi + 1) * POOL_DG)
        a = ext[:, cs]
        span = 1
        while span < w:
            a = a + pltpu.roll(a, span, 0)
            span *= 2
        shift = w // 2 - 1
        if shift:
            a = pltpu.roll(a, n_ext - shift, 0)
        win = a[POOL_HALO:POOL_HALO + tm]
        lo = jnp.maximum(pos - w // 2, 0)
        hi = jnp.minimum(pos + w // 2, seq_len)
        pooled = win / (hi - lo).astype(F32) - u[:, cs]
        mixed.append(_dot(pooled.astype(BF16), pw[gi]))
    pool_out = (jnp.concatenate(mixed, axis=-1) * pscale[...] * _silu(pgate[0])).astype(BF16)

    y = _dot(gla_out, wa[...]) + _dot(pool_out, wb[...])
    o_ref[0] = x[0] + y


def _even_out(o_f, o_b, gate, norm_g, pool_u, pool_gate, pool_w, pool_scale, w_out, x):
    bsz, s, _ = x.shape
    tm = TOKEN_TILE
    nh = tm // POOL_HALO
    n_halo = s // POOL_HALO
    cur = lambda b, i: (b, i, 0)
    in_specs = [
        pl.BlockSpec((1, tm, GLA_V_TOT), cur), pl.BlockSpec((1, tm, GLA_V_TOT), cur),
        pl.BlockSpec((1, tm, GLA_V_TOT), cur), _const_spec((1, GLA_DV)),
        pl.BlockSpec((1, tm, POOL_W), cur),
        pl.BlockSpec((1, POOL_HALO, POOL_W), lambda b, i: (b, jnp.maximum(i * nh - 1, 0), 0)),
        pl.BlockSpec((1, POOL_HALO, POOL_W), lambda b, i: (b, jnp.minimum((i + 1) * nh, n_halo - 1), 0)),
        pl.BlockSpec((1, tm, POOL_W), cur),
        _const_spec((POOL_GROUPS, POOL_DG, POOL_DG)), _const_spec((1, POOL_W)),
        _const_spec((GLA_V_TOT, D_MODEL)), _const_spec((POOL_W, D_MODEL)),
        pl.BlockSpec((1, tm, D_MODEL), cur)]
    return pl.pallas_call(
        functools.partial(_even_out_kernel, s),
        grid=(bsz, s // tm),
        in_specs=in_specs,
        out_specs=pl.BlockSpec((1, tm, D_MODEL), cur),
        out_shape=jax.ShapeDtypeStruct((bsz, s, D_MODEL), F32),
        compiler_params=_cparams(("parallel", "parallel")),
        name="even_out",
    )(o_f, o_b, gate, norm_g[None, :], pool_u, pool_u, pool_u, pool_gate,
      pool_w.astype(BF16), pool_scale[None, :],
      w_out[:GLA_V_TOT].astype(BF16), w_out[GLA_V_TOT:].astype(BF16), x)


def _mla_qkv_kernel(cq, ckv, kr, krr, cos_h, sin_h, cos_r, sin_r, qg, kvg, wq, wkn, place, wvt,
                    q_ref, k_ref, vt_ref):
    qk_scale = math.log2(math.e) * (MLA_NOPE + MLA_ROPE) ** -0.5
    w_all = MLA_HEADS * MLA_HEAD_PAD
    nq = _rms(cq[0], qg[...]).astype(BF16)
    qq = _dot(nq, wq[...])
    cos2 = jnp.concatenate([cos_h[...], cos_h[...]], axis=-1)
    sin2 = jnp.concatenate([sin_h[...], sin_h[...]], axis=-1)
    for p in range(MLA_HEADS // 2):
        sl = slice(2 * p * MLA_HEAD_PAD, (2 * p + 2) * MLA_HEAD_PAD)
        sr = slice(w_all + 2 * p * MLA_HEAD_PAD, w_all + (2 * p + 2) * MLA_HEAD_PAD)
        q_ref[0, :, sl] = ((qq[:, sl] * cos2 + qq[:, sr] * sin2) * qk_scale).astype(BF16)

    nkv = _rms(ckv[0], kvg[...]).astype(BF16)
    k_rope = (kr[0] * cos_r[...] + krr[0] * sin_r[...]).astype(BF16)
    k_ref[0] = (_dot(nkv, wkn[...]) + _dot(k_rope, place[...])).astype(BF16)

    vt = _dot_nt(wvt[...], nkv)
    ones_row = lax.broadcasted_iota(jnp.int32, (MLA_VT_ROWS, 1), 0) == MLA_DV
    for h in range(MLA_HEADS):
        vh = vt[h * MLA_VT_ROWS:(h + 1) * MLA_VT_ROWS]
        vt_ref[0, h, 0] = jnp.where(ones_row, 1.0, vh).astype(BF16)


def _mla_qkv(cq, ckv, kr, krr, tabs, q_norm_g, kv_norm_g, wq, wkn, place, wvt):
    bsz, s, _ = cq.shape
    tm = ATT_TK
    w_all = MLA_HEADS * MLA_HEAD_PAD
    cur = lambda b, i: (b, i, 0)
    tab = lambda b, i: (i, 0)
    cos_h, sin_h, cos_r, sin_r = tabs
    in_specs = [
        pl.BlockSpec((1, tm, MLA_Q_LORA), cur), pl.BlockSpec((1, tm, MLA_KV_LORA), cur),
        pl.BlockSpec((1, tm, MLA_ROPE), cur), pl.BlockSpec((1, tm, MLA_ROPE), cur),
        pl.BlockSpec((tm, MLA_HEAD_PAD), tab), pl.BlockSpec((tm, MLA_HEAD_PAD), tab),
        pl.BlockSpec((tm, MLA_ROPE), tab), pl.BlockSpec((tm, MLA_ROPE), tab),
        _const_spec((1, MLA_Q_LORA)), _const_spec((1, MLA_KV_LORA)),
        _const_spec(wq.shape), _const_spec(wkn.shape), _const_spec(place.shape), _const_spec(wvt.shape)]
    out_specs = [
        pl.BlockSpec((1, tm, w_all), cur), pl.BlockSpec((1, tm, w_all), cur),
        pl.BlockSpec((1, MLA_HEADS, 1, MLA_VT_ROWS, tm), lambda b, i: (b, 0, i, 0, 0))]
    out_shape = [
        jax.ShapeDtypeStruct((bsz, s, w_all), BF16), jax.ShapeDtypeStruct((bsz, s, w_all), BF16),
        jax.ShapeDtypeStruct((bsz, MLA_HEADS, s // tm, MLA_VT_ROWS, tm), BF16)]
    return pl.pallas_call(
        _mla_qkv_kernel,
        grid=(bsz, s // tm),
        in_specs=in_specs,
        out_specs=out_specs,
        out_shape=out_shape,
        compiler_params=_cparams(("parallel", "parallel")),
        name="mla_qkv",
    )(cq, ckv, kr, krr, cos_h, sin_h, cos_r, sin_r, q_norm_g[None, :], kv_norm_g[None, :],
      wq, wkn, place, wvt)


def _mla_weights(q_up, kv_up):
    dq = MLA_NOPE + MLA_ROPE
    half = MLA_ROPE // 2
    qh = q_up.reshape(MLA_Q_LORA, MLA_HEADS, dq)
    zeros = jnp.zeros((MLA_Q_LORA, MLA_HEADS, MLA_HEAD_PAD - dq), F32)
    main = jnp.concatenate([qh, zeros], axis=-1)
    x1 = qh[..., MLA_NOPE:MLA_NOPE + half]
    x2 = qh[..., MLA_NOPE + half:]
    rot = jnp.concatenate([jnp.zeros((MLA_Q_LORA, MLA_HEADS, MLA_NOPE), F32), -x2, x1, zeros], axis=-1)
    w_all = MLA_HEADS * MLA_HEAD_PAD
    wq = jnp.concatenate([main.reshape(MLA_Q_LORA, w_all), rot.reshape(MLA_Q_LORA, w_all)], axis=-1)

    kvh = kv_up.reshape(MLA_KV_LORA, MLA_HEADS, MLA_NOPE + MLA_DV)
    wkn = jnp.concatenate(
        [kvh[..., :MLA_NOPE], jnp.zeros((MLA_KV_LORA, MLA_HEADS, MLA_HEAD_PAD - MLA_NOPE), F32)],
        axis=-1).reshape(MLA_KV_LORA, w_all)
    wv = jnp.transpose(kvh[..., MLA_NOPE:], (1, 2, 0))
    wvt = jnp.concatenate(
        [wv, jnp.zeros((MLA_HEADS, MLA_VT_ROWS - MLA_DV, MLA_KV_LORA), F32)],
        axis=1).reshape(MLA_HEADS * MLA_VT_ROWS, MLA_KV_LORA)
    r = jnp.arange(MLA_ROPE)
    cols = jnp.arange(w_all)
    place = ((cols[None, :] % MLA_HEAD_PAD) == (MLA_NOPE + r[:, None])).astype(BF16)
    return wq.astype(BF16), wkn.astype(BF16), place, wvt.astype(BF16)


def _rope_tables(s):
    inv = ROPE_THETA ** (-jnp.arange(0, MLA_ROPE, 2, dtype=F32) / MLA_ROPE)
    ang = jnp.arange(s, dtype=F32)[:, None] * inv[None, :]
    cos, sin = jnp.cos(ang), jnp.sin(ang)
    pad = jnp.zeros((s, MLA_HEAD_PAD - MLA_NOPE - MLA_ROPE), F32)
    cos_h = jnp.concatenate([jnp.ones((s, MLA_NOPE), F32), cos, cos, pad], axis=-1)
    sin_h = jnp.concatenate([jnp.zeros((s, MLA_NOPE), F32), sin, sin, pad], axis=-1)
    cos_r = jnp.concatenate([cos, cos], axis=-1)
    sin_r = jnp.concatenate([sin, sin], axis=-1)
    return cos_h, sin_h, cos_r, sin_r


def _mla_attn_kernel(q_ref, k_ref, vt_ref, o_ref):
    n_k = k_ref.shape[1] // ATT_TK
    for qs in range(ATT_TQ // ATT_TQ_SUB):
        rows = slice(qs * ATT_TQ_SUB, (qs + 1) * ATT_TQ_SUB)
        q0 = q_ref[0, rows, 0:MLA_HEAD_PAD]
        q1 = q_ref[0, rows, MLA_HEAD_PAD:2 * MLA_HEAD_PAD]

        def step(kk, carry):
            r = pl.multiple_of(kk * ATT_TK, ATT_TK)
            out = []
            for hh, qt in enumerate((q0, q1)):
                m, acc = carry[2 * hh], carry[2 * hh + 1]
                kt = k_ref[0, pl.ds(r, ATT_TK), hh * MLA_HEAD_PAD:(hh + 1) * MLA_HEAD_PAD]
                st = _dot_nt(kt, qt)
                m_new = jnp.maximum(m, jnp.max(st, axis=0, keepdims=True))
                alpha = jnp.exp2(m - m_new)
                p = jnp.exp2(st - m_new).astype(BF16)
                acc = acc * alpha + _dot(vt_ref[0, hh, kk], p)
                out += [m_new, acc]
            return tuple(out)

        m0 = jnp.full((1, ATT_TQ_SUB), -jnp.inf, F32)
        a0 = jnp.zeros((MLA_VT_ROWS, ATT_TQ_SUB), F32)
        res = lax.fori_loop(0, n_k, step, (m0, a0, m0, a0))
        outs = [res[2 * hh + 1][:MLA_DV] / res[2 * hh + 1][MLA_DV:MLA_DV + 1] for hh in range(2)]
        o_ref[0, rows, :] = jnp.concatenate(outs, axis=0).T


def _mla_attn(q, k, vt):
    bsz, s, _ = q.shape
    n_kt = s // ATT_TK
    return pl.pallas_call(
        _mla_attn_kernel,
        grid=(bsz, MLA_HEADS // 2, s // ATT_TQ),
        in_specs=[
            pl.BlockSpec((1, ATT_TQ, 2 * MLA_HEAD_PAD), lambda b, h, i: (b, i, h)),
            pl.BlockSpec((1, s, 2 * MLA_HEAD_PAD), lambda b, h, i: (b, 0, h)),
            pl.BlockSpec((1, 2, n_kt, MLA_VT_ROWS, ATT_TK), lambda b, h, i: (b, h, 0, 0, 0))],
        out_specs=pl.BlockSpec((1, ATT_TQ, 2 * MLA_DV), lambda b, h, i: (b, i, h)),
        out_shape=jax.ShapeDtypeStruct((bsz, s, MLA_W), F32),
        compiler_params=_cparams(("parallel", "parallel", "arbitrary")),
        name="mla_attn",
    )(q, k, vt)


def _mlstm_chunk(q_ref, k_ref, v_ref, o_ref, cn_ref, m_ref, st_idx, row, h, cum_c, cum_r, g_c, g_r,
                 i_col, f_col, inclusive):
    ii = lax.broadcasted_iota(jnp.int32, (CHUNK, CHUNK), 0)
    jj = lax.broadcasted_iota(jnp.int32, (CHUNK, CHUNK), 1)
    mask = (jj <= ii) if inclusive else (jj > ii)
    edge = CHUNK - 1 if inclusive else 0
    hs = slice(h * ML_DH, (h + 1) * ML_DH)

    b_c = cum_c[:, f_col:f_col + 1]
    b_r = cum_r[f_col:f_col + 1, :]
    li_c = g_c[:, i_col:i_col + 1]
    li_r = g_r[i_col:i_col + 1, :]
    g = b_r[:, edge:edge + 1]
    m = m_ref[st_idx]

    q = q_ref[0, pl.ds(row, CHUNK), hs]
    k = k_ref[0, pl.ds(row, CHUNK), hs] * (ML_DH ** -0.5)
    v = v_ref[0, pl.ds(row, CHUNK), hs]
    ones_col = (lax.broadcasted_iota(jnp.int32, (CHUNK, ML_DH), 1) == 0).astype(BF16)
    v_ext = jnp.concatenate([v, ones_col], axis=-1)

    d = jnp.where(mask, b_c - b_r + li_r, -jnp.inf)
    inter = b_c + m
    m_t = jnp.maximum(inter, jnp.max(d, axis=-1, keepdims=True))
    s = _dot_nt(q, k.astype(BF16)) * jnp.exp(d - m_t)
    e = jnp.exp(inter - m_t)
    cn = cn_ref[st_idx]
    hh = _dot(s.astype(BF16), v_ext) + e * _dot(q, cn.astype(BF16))
    num = hh[:, :ML_DH]
    den = hh[:, ML_DH:ML_DH + 1]
    o_ref[0, pl.ds(row, CHUNK), hs] = num / jnp.maximum(jnp.abs(den), jnp.exp(-m_t))

    lw_r = g - b_r + li_r
    lw_c = g - b_c + li_c
    m_new = jnp.maximum(g + m, jnp.max(lw_r, axis=-1, keepdims=True))
    wk = jnp.exp(lw_c - m_new)
    decay = jnp.exp(g + m - m_new)
    cn_ref[st_idx] = decay * cn + _dot_tn((wk * k).astype(BF16), v_ext)
    m_ref[st_idx] = m_new


def _mlstm_kernel(qf, kf, vf, gcf, grf, qb, kb, vb, gcb, grb, bias_c, bias_r, tril, triu,
                  of, ob, cn_ref, m_ref):
    c = pl.program_id(1)

    @pl.when(c == 0)
    def _():
        cn_ref[...] = jnp.zeros_like(cn_ref)
        m_ref[...] = jnp.zeros_like(m_ref)

    n_chunks = SEQ_TILE // CHUNK
    n_gate = 4 * ML_HEADS
    is_f_c = lax.broadcasted_iota(jnp.int32, (CHUNK, n_gate), 1) >= 2 * ML_HEADS
    is_f_r = lax.broadcasted_iota(jnp.int32, (n_gate, CHUNK), 0) >= 2 * ML_HEADS

    def gates(gc_ref, gr_ref, row, j):
        g_c = gc_ref[0, pl.ds(row, CHUNK), :] + bias_c[...]
        g_r = gr_ref[0, j] + bias_r[:, :CHUNK]
        return jnp.where(is_f_c, _log_sigmoid(g_c), g_c), jnp.where(is_f_r, _log_sigmoid(g_r), g_r)

    def body(j, carry):
        jb = n_chunks - 1 - j
        rf = pl.multiple_of(j * CHUNK, CHUNK)
        rb = pl.multiple_of(jb * CHUNK, CHUNK)
        gcf_v, grf_v = gates(gcf, grf, rf, j)
        gcb_v, grb_v = gates(gcb, grb, rb, jb)
        cum_cf = _tri_left(tril[...], gcf_v)
        cum_rf = _tri_right(grf_v, triu[...])
        cum_cb = _tri_left(triu[...], gcb_v)
        cum_rb = _tri_right(grb_v, tril[...])
        for h in range(ML_HEADS):
            _mlstm_chunk(qf, kf, vf, of, cn_ref, m_ref, h, rf, h, cum_cf, cum_rf, gcf_v, grf_v,
                         h, 2 * ML_HEADS + h, True)
            _mlstm_chunk(qb, kb, vb, ob, cn_ref, m_ref, ML_HEADS + h, rb, h, cum_cb, cum_rb, gcb_v, grb_v,
                         ML_HEADS + h, 3 * ML_HEADS + h, False)
        return carry

    lax.fori_loop(0, n_chunks, body, 0)


def _mlstm(q, k, v, gates, if_bias):
    bsz, s, _ = q.shape
    ts = SEQ_TILE
    ns = s // ts
    n_chunks = ts // CHUNK
    n_gate = 4 * ML_HEADS
    gates_r = gates.reshape(bsz, s // CHUNK, CHUNK, n_gate).transpose(0, 1, 3, 2)
    fwd = lambda b, c: (b, c, 0)
    bwd = lambda b, c: (b, ns - 1 - c, 0)
    fwd4 = lambda b, c: (b, c, 0, 0)
    bwd4 = lambda b, c: (b, ns - 1 - c, 0, 0)
    idx = jnp.arange(CHUNK)
    tril = (idx[None, :] <= idx[:, None]).astype(BF16)
    triu = (idx[None, :] >= idx[:, None]).astype(BF16)

    def seq_specs(imap, imap4):
        return [pl.BlockSpec((1, ts, ML_W), imap), pl.BlockSpec((1, ts, ML_W), imap),
                pl.BlockSpec((1, ts, ML_W), imap), pl.BlockSpec((1, ts, n_gate), imap),
                pl.BlockSpec((1, n_chunks, n_gate, CHUNK), imap4)]

    in_specs = seq_specs(fwd, fwd4) + seq_specs(bwd, bwd4) + [
        _const_spec((1, n_gate)), _const_spec((n_gate, 128)),
        _const_spec((CHUNK, CHUNK)), _const_spec((CHUNK, CHUNK))]
    out_specs = [pl.BlockSpec((1, ts, ML_W), fwd), pl.BlockSpec((1, ts, ML_W), bwd)]
    out_shape = [jax.ShapeDtypeStruct((bsz, s, ML_W), F32)] * 2
    return pl.pallas_call(
        _mlstm_kernel,
        grid=(bsz, ns),
        in_specs=in_specs,
        out_specs=out_specs,
        out_shape=out_shape,
        scratch_shapes=[pltpu.VMEM((2 * ML_HEADS, ML_DH, 2 * ML_DH), F32),
                        pltpu.VMEM((2 * ML_HEADS, 1, 1), F32)],
        compiler_params=_cparams(("parallel", "arbitrary")),
        name="mlstm_scan",
    )(q, k, v, gates, gates_r, q, k, v, gates, gates_r,
      if_bias[None, :], jnp.broadcast_to(if_bias[:, None], (n_gate, 128)), tril, triu)


def _odd_out_kernel(final, att, mgate, hf, hb, mo, lgate, ng, wa, wb, x, fg, o_ref):
    mla_out = (att[0] * _silu(mgate[0])).astype(BF16)
    hm = hf[0] + hb[0]
    mo_v, lg_v = mo[0], lgate[0]
    parts = []
    for h in range(ML_HEADS):
        hs = slice(h * ML_DH, (h + 1) * ML_DH)
        y = _rms(hm[:, hs], ng[...]) * _sigmoid(mo_v[:, hs])
        parts.append((y * _silu(lg_v[:, hs])).astype(BF16))
    ml_out = jnp.concatenate(parts, axis=-1)
    xn = x[0] + (_dot(mla_out, wa[...]) + _dot(ml_out, wb[...]))
    o_ref[0] = _rms(xn, fg[...]) if final else xn


def _odd_out(att, mla_gate, h_f, h_b, mo, ml_gate, norm_g, w_out, x, final_g, final):
    bsz, s, _ = x.shape
    tm = TOKEN_TILE
    cur = lambda b, i: (b, i, 0)
    half = pl.BlockSpec((1, tm, MLA_W), cur)
    in_specs = [half] * 6 + [
        _const_spec((1, ML_DH)), _const_spec((MLA_W, D_MODEL)), _const_spec((ML_W, D_MODEL)),
        pl.BlockSpec((1, tm, D_MODEL), cur), _const_spec((1, D_MODEL))]
    return pl.pallas_call(
        functools.partial(_odd_out_kernel, final),
        grid=(bsz, s // tm),
        in_specs=in_specs,
        out_specs=pl.BlockSpec((1, tm, D_MODEL), cur),
        out_shape=jax.ShapeDtypeStruct((bsz, s, D_MODEL), F32),
        compiler_params=_cparams(("parallel", "parallel")),
        name="odd_out",
    )(att, mla_gate, h_f, h_b, mo, ml_gate, norm_g[None, :],
      w_out[:MLA_W].astype(BF16), w_out[MLA_W:].astype(BF16), x, final_g[None, :])


def _col_split(w, sizes):
    out, off = [], 0
    for n in sizes:
        out.append(w[:, off:off + n].astype(BF16))
        off += n
    return out


def _even_layer(x, g, w_in, a_up, a_bias, gla_norm_g, pool_w, pool_scale, w_out):
    bsz, s, _ = x.shape
    sizes = (GLA_K_TOT, GLA_K_TOT, GLA_V_TOT, GLA_V_TOT, GLA_LR, GLA_LR, POOL_W, POOL_W)
    dts = (F32, F32, BF16, F32, BF16, BF16, F32, F32)
    outs = _norm_proj(x.reshape(bsz * s, D_MODEL), g[None, :], _col_split(w_in, sizes), dts)
    q, k, v, gate, lr_f, lr_b, pool_u, pool_gate = [o.reshape(bsz, s, -1) for o in outs]
    o_f, o_b = _gla(q, k, v, lr_f, lr_b, a_up, a_bias)
    return _even_out(o_f, o_b, gate, gla_norm_g, pool_u, pool_gate, pool_w, pool_scale, w_out, x)


def _odd_layer(x, g, tabs, w_in, q_norm_g, q_up, kv_norm_g, kv_up, if_bias, ml_norm_g, w_out,
               final_g, final):
    bsz, s, _ = x.shape
    half = MLA_ROPE // 2
    off = MLA_Q_LORA + MLA_KV_LORA
    kr_w = w_in[:, off:off + MLA_ROPE]
    w_ext = jnp.concatenate([w_in, -kr_w[:, half:], kr_w[:, :half]], axis=-1)
    sizes = (MLA_Q_LORA, MLA_KV_LORA, MLA_ROPE, MLA_W, ML_W, ML_W, ML_W, ML_W, 4 * ML_HEADS, ML_W,
             MLA_ROPE)
    dts = (F32, F32, F32, F32, BF16, F32, BF16, F32, F32, F32, F32)
    outs = _norm_proj(x.reshape(bsz * s, D_MODEL), g[None, :], _col_split(w_ext, sizes), dts)
    cq, ckv, kr, mla_gate, mq, mk, mv, mo, mif, ml_gate, krr = [o.reshape(bsz, s, -1) for o in outs]
    wq, wkn, place, wvt = _mla_weights(q_up, kv_up)
    qa, ka, vt = _mla_qkv(cq, ckv, kr, krr, tabs, q_norm_g, kv_norm_g, wq, wkn, place, wvt)
    att = _mla_attn(qa, ka, vt)
    h_f, h_b = _mlstm(mq, mk, mv, mif, if_bias)
    return _odd_out(att, mla_gate, h_f, h_b, mo, ml_gate, ml_norm_g, w_out, x, final_g, final)


def _trunk(x, norm_g, final_norm_g, e_w_in, e_gla_a_up, e_gla_a_bias, e_gla_norm_g, e_pool_w,
           e_pool_scale, e_w_out, o_w_in, o_q_norm_g, o_q_up, o_kv_norm_g, o_kv_up, o_if_bias,
           o_mlstm_norm_g, o_w_out):
    depth = norm_g.shape[0]
    tabs = _rope_tables(x.shape[1])
    for layer in range(depth):
        i = layer // 2
        if layer % 2 == 0:
            x = _even_layer(x, norm_g[layer], e_w_in[i], e_gla_a_up[i], e_gla_a_bias[i],
                            e_gla_norm_g[i], e_pool_w[i], e_pool_scale[i], e_w_out[i])
        else:
            x = _odd_layer(x, norm_g[layer], tabs, o_w_in[i], o_q_norm_g[i], o_q_up[i],
                           o_kv_norm_g[i], o_kv_up[i], o_if_bias[i], o_mlstm_norm_g[i], o_w_out[i],
                           final_norm_g, layer == depth - 1)
    return x


def kernel(x_prompt, x_sample, norm_g, final_norm_g, e_w_in, e_gla_a_up, e_gla_a_bias, e_gla_norm_g,
           e_pool_w, e_pool_scale, e_w_out, o_w_in, o_q_norm_g, o_q_up, o_kv_norm_g, o_kv_up,
           o_if_bias, o_mlstm_norm_g, o_w_out):
    params = (norm_g, final_norm_g, e_w_in, e_gla_a_up, e_gla_a_bias, e_gla_norm_g, e_pool_w,
              e_pool_scale, e_w_out, o_w_in, o_q_norm_g, o_q_up, o_kv_norm_g, o_kv_up, o_if_bias,
              o_mlstm_norm_g, o_w_out)
    return (_trunk(x_prompt, *params), _trunk(x_sample, *params))
```

```python
import functools
import math

import jax
import jax.numpy as jnp
from jax import lax
from jax.experimental import pallas as pl
from jax.experimental.pallas import tpu as pltpu

F32 = jnp.float32
BF16 = jnp.bfloat16

D_MODEL = 1024
NORM_EPS = 1e-6
CHUNK = 64

GLA_HEADS = 4
GLA_DK = 128
GLA_DV = 256
GLA_LR = 16
GLA_GATE_NORM = 16.0
GLA_K_TOT = GLA_HEADS * GLA_DK
GLA_V_TOT = GLA_HEADS * GLA_DV

POOL_GROUPS = 4
POOL_WINDOWS = (2, 4, 8, 16)
POOL_DG = 128
POOL_W = POOL_GROUPS * POOL_DG
POOL_HALO = 8

MLA_HEADS = 8
MLA_NOPE = 64
MLA_ROPE = 32
MLA_DV = 64
MLA_Q_LORA = 384
MLA_KV_LORA = 256
MLA_W = MLA_HEADS * MLA_DV
MLA_HEAD_PAD = 128
MLA_VT_ROWS = 80
ROPE_THETA = 10000.0

ML_HEADS = 4
ML_DH = 128
ML_W = ML_HEADS * ML_DH

VMEM_LIMIT_BYTES = 56 * 1024 * 1024

TOKEN_TILE = 512
SEQ_TILE = 512
ATT_TQ = 512
ATT_TQ_SUB = 256
ATT_TK = 512


def _cparams(sem):
    return pltpu.CompilerParams(dimension_semantics=sem, vmem_limit_bytes=VMEM_LIMIT_BYTES)


def _const_spec(shape):
    nd = len(shape)
    return pl.BlockSpec(shape, lambda *_: (0,) * nd)


def _dot(a, b):
    return jnp.dot(a, b, preferred_element_type=F32)


def _dot_nt(a, b):
    return lax.dot_general(a, b, (((1,), (1,)), ((), ())), preferred_element_type=F32)


def _dot_tn(a, b):
    return lax.dot_general(a, b, (((0,), (0,)), ((), ())), preferred_element_type=F32)


def _sigmoid(x):
    return 1.0 / (1.0 + jnp.exp(-x))


def _silu(x):
    return x * _sigmoid(x)


def _log_sigmoid(x):
    return jnp.minimum(x, 0.0) - jnp.log1p(jnp.exp(-jnp.abs(x)))


def _rms(x, g):
    return x * lax.rsqrt(jnp.mean(x * x, axis=-1, keepdims=True) + NORM_EPS) * g


def _split3(x):
    hi = x.astype(BF16)
    r1 = x - hi.astype(F32)
    mid = r1.astype(BF16)
    lo = (r1 - mid.astype(F32)).astype(BF16)
    return hi, mid, lo


def _tri_left(tri, x):
    hi, mid, lo = _split3(x)
    return _dot(tri, hi) + _dot(tri, mid) + _dot(tri, lo)


def _tri_right(x, tri):
    hi, mid, lo = _split3(x)
    return _dot(hi, tri) + _dot(mid, tri) + _dot(lo, tri)


def _norm_proj_kernel(n_out, x_ref, g_ref, *refs):
    w_refs, o_refs = refs[:n_out], refs[n_out:]
    h = _rms(x_ref[...], g_ref[...]).astype(BF16)
    for w_ref, o_ref in zip(w_refs, o_refs):
        o_ref[...] = _dot(h, w_ref[...]).astype(o_ref.dtype)


def _norm_proj(x2d, g, weights, out_dtypes):
    t = x2d.shape[0]
    tm = TOKEN_TILE
    n_out = len(weights)
    in_specs = [pl.BlockSpec((tm, D_MODEL), lambda i: (i, 0)), _const_spec((1, D_MODEL))]
    in_specs += [_const_spec(w.shape) for w in weights]
    out_specs = [pl.BlockSpec((tm, w.shape[1]), lambda i: (i, 0)) for w in weights]
    out_shape = [jax.ShapeDtypeStruct((t, w.shape[1]), dt) for w, dt in zip(weights, out_dtypes)]
    return pl.pallas_call(
        functools.partial(_norm_proj_kernel, n_out),
        grid=(t // tm,),
        in_specs=in_specs,
        out_specs=out_specs,
        out_shape=out_shape,
        compiler_params=_cparams(("parallel",)),
        name="norm_proj",
    )(x2d, g, *weights)


def _gla_chunk(q_ref, k_ref, v_ref, o_ref, st_ref, st_idx, row, b_all, inclusive):
    ii = lax.broadcasted_iota(jnp.int32, (CHUNK, CHUNK), 0)
    jj = lax.broadcasted_iota(jnp.int32, (CHUNK, CHUNK), 1)
    mask = (jj <= ii) if inclusive else (jj > ii)
    edge = CHUNK - 1 if inclusive else 0
    for h in range(GLA_HEADS):
        ks = slice(h * GLA_DK, (h + 1) * GLA_DK)
        vs = slice(h * GLA_DV, (h + 1) * GLA_DV)
        q = q_ref[0, pl.ds(row, CHUNK), ks] * (GLA_DK ** -0.5)
        k = k_ref[0, pl.ds(row, CHUNK), ks]
        v = v_ref[0, pl.ds(row, CHUNK), vs]
        b = b_all[:, ks]
        b_edge = b[edge:edge + 1, :]
        qe = (q * jnp.exp(b)).astype(BF16)
        kd = (k * jnp.exp(-b)).astype(BF16)
        a = jnp.where(mask, _dot_nt(qe, kd), 0.0).astype(BF16)
        st = st_ref[st_idx + h]
        o = _dot(a, v) + _dot_nt(qe, st.astype(BF16))
        o_ref[0, pl.ds(row, CHUNK), vs] = o
        kdec = (k * jnp.exp(b_edge - b)).astype(BF16)
        st_ref[st_idx + h] = st * jnp.exp(b_edge) + _dot_tn(v, kdec)


def _gla_kernel(qf, kf, vf, lrf, qb, kb, vb, lrb, aupf, aupb, biasf, biasb, tril, triu,
                of, ob, st_ref, la_ref):
    c = pl.program_id(1)

    @pl.when(c == 0)
    def _():
        st_ref[...] = jnp.zeros_like(st_ref)

    inv = 1.0 / GLA_GATE_NORM
    la_ref[0] = _log_sigmoid(_dot(lrf[0], aupf[...]) + biasf[...]) * inv
    la_ref[1] = _log_sigmoid(_dot(lrb[0], aupb[...]) + biasb[...]) * inv
    n_chunks = SEQ_TILE // CHUNK

    def body(j, carry):
        rf = pl.multiple_of(j * CHUNK, CHUNK)
        rb = pl.multiple_of((n_chunks - 1 - j) * CHUNK, CHUNK)
        b_f = _tri_left(tril[...], la_ref[0, pl.ds(rf, CHUNK), :])
        b_b = _tri_left(triu[...], la_ref[1, pl.ds(rb, CHUNK), :])
        _gla_chunk(qf, kf, vf, of, st_ref, 0, rf, b_f, True)
        _gla_chunk(qb, kb, vb, ob, st_ref, GLA_HEADS, rb, b_b, False)
        return carry

    lax.fori_loop(0, n_chunks, body, 0)


def _gla(q, k, v, lr_f, lr_b, a_up, a_bias):
    bsz, s, _ = q.shape
    ts = SEQ_TILE
    ns = s // ts
    fwd = lambda b, c: (b, c, 0)
    bwd = lambda b, c: (b, ns - 1 - c, 0)
    idx = jnp.arange(CHUNK)
    tril = (idx[None, :] <= idx[:, None]).astype(BF16)
    triu = (idx[None, :] >= idx[:, None]).astype(BF16)

    def seq_specs(imap):
        return [pl.BlockSpec((1, ts, GLA_K_TOT), imap), pl.BlockSpec((1, ts, GLA_K_TOT), imap),
                pl.BlockSpec((1, ts, GLA_V_TOT), imap), pl.BlockSpec((1, ts, GLA_LR), imap)]

    in_specs = seq_specs(fwd) + seq_specs(bwd) + [
        _const_spec((GLA_LR, GLA_K_TOT)), _const_spec((GLA_LR, GLA_K_TOT)),
        _const_spec((1, GLA_K_TOT)), _const_spec((1, GLA_K_TOT)),
        _const_spec((CHUNK, CHUNK)), _const_spec((CHUNK, CHUNK))]
    out_specs = [pl.BlockSpec((1, ts, GLA_V_TOT), fwd), pl.BlockSpec((1, ts, GLA_V_TOT), bwd)]
    out_shape = [jax.ShapeDtypeStruct((bsz, s, GLA_V_TOT), F32)] * 2
    return pl.pallas_call(
        _gla_kernel,
        grid=(bsz, ns),
        in_specs=in_specs,
        out_specs=out_specs,
        out_shape=out_shape,
        scratch_shapes=[pltpu.VMEM((2 * GLA_HEADS, GLA_DV, GLA_DK), F32),
                        pltpu.VMEM((2, ts, GLA_K_TOT), F32)],
        compiler_params=_cparams(("parallel", "arbitrary")),
        name="gla_scan",
    )(q, k, v, lr_f, q, k, v, lr_b,
      a_up[0].astype(BF16), a_up[1].astype(BF16), a_bias[0:1], a_bias[1:2], tril, triu)


def _even_out_kernel(seq_len, of, ob, gate, ng, pu, pprev, pnext, pgate, pw, pscale, wa, wb, x, o_ref):
    tm = of.shape[1]
    i = pl.program_id(1)
    n_i = pl.num_programs(1)

    o = of[0] + ob[0]
    g_all = gate[0]
    parts = []
    for h in range(GLA_HEADS):
        vs = slice(h * GLA_DV, (h + 1) * GLA_DV)
        parts.append((_rms(o[:, vs], ng[...]) * _silu(g_all[:, vs])).astype(BF16))
    gla_out = jnp.concatenate(parts, axis=-1)

    u = pu[0]
    prev = jnp.where(i > 0, pprev[0], 0.0)
    nxt = jnp.where(i < n_i - 1, pnext[0], 0.0)
    ext = jnp.concatenate([prev, u, nxt], axis=0)
    n_ext = tm + 2 * POOL_HALO
    pos = i * tm + lax.broadcasted_iota(jnp.int32, (tm, 1), 0)
    mixed = []
    for gi, w in enumerate(POOL_WINDOWS):
        cs = slice(gi * POOL_DG, (gi + 1) * POOL_DG)
        a = ext[:, cs]
        span = 1
        while span < w:
            a = a + pltpu.roll(a, span, 0)
            span *= 2
        shift = w // 2 - 1
        if shift:
            a = pltpu.roll(a, n_ext - shift, 0)
        win = a[POOL_HALO:POOL_HALO + tm]
        lo = jnp.maximum(pos - w // 2, 0)
        hi = jnp.minimum(pos + w // 2, seq_len)
        pooled = win / (hi - lo).astype(F32) - u[:, cs]
        mixed.append(_dot(pooled.astype(BF16), pw[gi]))
    pool_out = (jnp.concatenate(mixed, axis=-1) * pscale[...] * _silu(pgate[0])).astype(BF16)

    y = _dot(gla_out, wa[...]) + _dot(pool_out, wb[...])
    o_ref[0] = x[0] + y


def _even_out(o_f, o_b, gate, norm_g, pool_u, pool_gate, pool_w, pool_scale, w_out, x):
    bsz, s, _ = x.shape
    tm = TOKEN_TILE
    nh = tm // POOL_HALO
    n_halo = s // POOL_HALO
    cur = lambda b, i: (b, i, 0)
    in_specs = [
        pl.BlockSpec((1, tm, GLA_V_TOT), cur), pl.BlockSpec((1, tm, GLA_V_TOT), cur),
        pl.BlockSpec((1, tm, GLA_V_TOT), cur), _const_spec((1, GLA_DV)),
        pl.BlockSpec((1, tm, POOL_W), cur),
        pl.BlockSpec((1, POOL_HALO, POOL_W), lambda b, i: (b, jnp.maximum(i * nh - 1, 0), 0)),
        pl.BlockSpec((1, POOL_HALO, POOL_W), lambda b, i: (b, jnp.minimum((i + 1) * nh, n_halo - 1), 0)),
        pl.BlockSpec((1, tm, POOL_W), cur),
        _const_spec((POOL_GROUPS, POOL_DG, POOL_DG)), _const_spec((1, POOL_W)),
        _const_spec((GLA_V_TOT, D_MODEL)), _const_spec((POOL_W, D_MODEL)),
        pl.BlockSpec((1, tm, D_MODEL), cur)]
    return pl.pallas_call(
        functools.partial(_even_out_kernel, s),
        grid=(bsz, s // tm),
        in_specs=in_specs,
        out_specs=pl.BlockSpec((1, tm, D_MODEL), cur),
        out_shape=jax.ShapeDtypeStruct((bsz, s, D_MODEL), F32),
        compiler_params=_cparams(("parallel", "parallel")),
        name="even_out",
    )(o_f, o_b, gate, norm_g[None, :], pool_u, pool_u, pool_u, pool_gate,
      pool_w.astype(BF16), pool_scale[None, :],
      w_out[:GLA_V_TOT].astype(BF16), w_out[GLA_V_TOT:].astype(BF16), x)


def _mla_qkv_kernel(cq, ckv, kr, krr, cos_h, sin_h, cos_r, sin_r, qg, kvg, wq, wkn, place, wvt,
                    q_ref, k_ref, vt_ref):
    qk_scale = math.log2(math.e) * (MLA_NOPE + MLA_ROPE) ** -0.5
    w_all = MLA_HEADS * MLA_HEAD_PAD
    nq = _rms(cq[0], qg[...]).astype(BF16)
    qq = _dot(nq, wq[...])
    cos2 = jnp.concatenate([cos_h[...], cos_h[...]], axis=-1)
    sin2 = jnp.concatenate([sin_h[...], sin_h[...]], axis=-1)
    for p in range(MLA_HEADS // 2):
        sl = slice(2 * p * MLA_HEAD_PAD, (2 * p + 2) * MLA_HEAD_PAD)
        sr = slice(w_all + 2 * p * MLA_HEAD_PAD, w_all + (2 * p + 2) * MLA_HEAD_PAD)
        q_ref[0, :, sl] = ((qq[:, sl] * cos2 + qq[:, sr] * sin2) * qk_scale).astype(BF16)

    nkv = _rms(ckv[0], kvg[...]).astype(BF16)
    k_rope = (kr[0] * cos_r[...] + krr[0] * sin_r[...]).astype(BF16)
    k_ref[0] = (_dot(nkv, wkn[...]) + _dot(k_rope, place[...])).astype(BF16)

    vt = _dot_nt(wvt[...], nkv)
    ones_row = lax.broadcasted_iota(jnp.int32, (MLA_VT_ROWS, 1), 0) == MLA_DV
    for h in range(MLA_HEADS):
        vh = vt[h * MLA_VT_ROWS:(h + 1) * MLA_VT_ROWS]
        vt_ref[0, h, 0] = jnp.where(ones_row, 1.0, vh).astype(BF16)


def _mla_qkv(cq, ckv, kr, krr, tabs, q_norm_g, kv_norm_g, wq, wkn, place, wvt):
    bsz, s, _ = cq.shape
    tm = ATT_TK
    w_all = MLA_HEADS * MLA_HEAD_PAD
    cur = lambda b, i: (b, i, 0)
    tab = lambda b, i: (i, 0)
    cos_h, sin_h, cos_r, sin_r = tabs
    in_specs = [
        pl.BlockSpec((1, tm, MLA_Q_LORA), cur), pl.BlockSpec((1, tm, MLA_KV_LORA), cur),
        pl.BlockSpec((1, tm, MLA_ROPE), cur), pl.BlockSpec((1, tm, MLA_ROPE), cur),
        pl.BlockSpec((tm, MLA_HEAD_PAD), tab), pl.BlockSpec((tm, MLA_HEAD_PAD), tab),
        pl.BlockSpec((tm, MLA_ROPE), tab), pl.BlockSpec((tm, MLA_ROPE), tab),
        _const_spec((1, MLA_Q_LORA)), _const_spec((1, MLA_KV_LORA)),
        _const_spec(wq.shape), _const_spec(wkn.shape), _const_spec(place.shape), _const_spec(wvt.shape)]
    out_specs = [
        pl.BlockSpec((1, tm, w_all), cur), pl.BlockSpec((1, tm, w_all), cur),
        pl.BlockSpec((1, MLA_HEADS, 1, MLA_VT_ROWS, tm), lambda b, i: (b, 0, i, 0, 0))]
    out_shape = [
        jax.ShapeDtypeStruct((bsz, s, w_all), BF16), jax.ShapeDtypeStruct((bsz, s, w_all), BF16),
        jax.ShapeDtypeStruct((bsz, MLA_HEADS, s // tm, MLA_VT_ROWS, tm), BF16)]
    return pl.pallas_call(
        _mla_qkv_kernel,
        grid=(bsz, s // tm),
        in_specs=in_specs,
        out_specs=out_specs,
        out_shape=out_shape,
        compiler_params=_cparams(("parallel", "parallel")),
        name="mla_qkv",
    )(cq, ckv, kr, krr, cos_h, sin_h, cos_r, sin_r, q_norm_g[None, :], kv_norm_g[None, :],
      wq, wkn, place, wvt)


def _mla_weights(q_up, kv_up):
    dq = MLA_NOPE + MLA_ROPE
    half = MLA_ROPE // 2
    qh = q_up.reshape(MLA_Q_LORA, MLA_HEADS, dq)
    zeros = jnp.zeros((MLA_Q_LORA, MLA_HEADS, MLA_HEAD_PAD - dq), F32)
    main = jnp.concatenate([qh, zeros], axis=-1)
    x1 = qh[..., MLA_NOPE:MLA_NOPE + half]
    x2 = qh[..., MLA_NOPE + half:]
    rot = jnp.concatenate([jnp.zeros((MLA_Q_LORA, MLA_HEADS, MLA_NOPE), F32), -x2, x1, zeros], axis=-1)
    w_all = MLA_HEADS * MLA_HEAD_PAD
    wq = jnp.concatenate([main.reshape(MLA_Q_LORA, w_all), rot.reshape(MLA_Q_LORA, w_all)], axis=-1)

    kvh = kv_up.reshape(MLA_KV_LORA, MLA_HEADS, MLA_NOPE + MLA_DV)
    wkn = jnp.concatenate(
        [kvh[..., :MLA_NOPE], jnp.zeros((MLA_KV_LORA, MLA_HEADS, MLA_HEAD_PAD - MLA_NOPE), F32)],
        axis=-1).reshape(MLA_KV_LORA, w_all)
    wv = jnp.transpose(kvh[..., MLA_NOPE:], (1, 2, 0))
    wvt = jnp.concatenate(
        [wv, jnp.zeros((MLA_HEADS, MLA_VT_ROWS - MLA_DV, MLA_KV_LORA), F32)],
        axis=1).reshape(MLA_HEADS * MLA_VT_ROWS, MLA_KV_LORA)
    r = jnp.arange(MLA_ROPE)
    cols = jnp.arange(w_all)
    place = ((cols[None, :] % MLA_HEAD_PAD) == (MLA_NOPE + r[:, None])).astype(BF16)
    return wq.astype(BF16), wkn.astype(BF16), place, wvt.astype(BF16)


def _rope_tables(s):
    inv = ROPE_THETA ** (-jnp.arange(0, MLA_ROPE, 2, dtype=F32) / MLA_ROPE)
    ang = jnp.arange(s, dtype=F32)[:, None] * inv[None, :]
    cos, sin = jnp.cos(ang), jnp.sin(ang)
    pad = jnp.zeros((s, MLA_HEAD_PAD - MLA_NOPE - MLA_ROPE), F32)
    cos_h = jnp.concatenate([jnp.ones((s, MLA_NOPE), F32), cos, cos, pad], axis=-1)
    sin_h = jnp.concatenate([jnp.zeros((s, MLA_NOPE), F32), sin, sin, pad], axis=-1)
    cos_r = jnp.concatenate([cos, cos], axis=-1)
    sin_r = jnp.concatenate([sin, sin], axis=-1)
    return cos_h, sin_h, cos_r, sin_r


def _mla_attn_kernel(q_ref, k_ref, vt_ref, o_ref, s_ref):
    n_k = k_ref.shape[1] // ATT_TK
    for qs in range(ATT_TQ // ATT_TQ_SUB):
        rows = slice(qs * ATT_TQ_SUB, (qs + 1) * ATT_TQ_SUB)
        qts = (q_ref[0, rows, 0:MLA_HEAD_PAD], q_ref[0, rows, MLA_HEAD_PAD:2 * MLA_HEAD_PAD])

        def produce(kk, buf):
            r = pl.multiple_of(kk * ATT_TK, ATT_TK)
            tile_max = []
            for hh in range(2):
                kt = k_ref[0, pl.ds(r, ATT_TK), hh * MLA_HEAD_PAD:(hh + 1) * MLA_HEAD_PAD]
                st = _dot_nt(kt, qts[hh])
                s_ref[buf, hh] = st
                tile_max.append(jnp.max(st, axis=0, keepdims=True))
            return tuple(tile_max)

        def consume(kk, buf, tile_max, carry):
            out = []
            for hh in range(2):
                m, acc = carry[2 * hh], carry[2 * hh + 1]
                m_new = jnp.maximum(m, tile_max[hh])
                alpha = jnp.exp2(m - m_new)
                p = jnp.exp2(s_ref[buf, hh] - m_new).astype(BF16)
                acc = acc * alpha + _dot(vt_ref[0, hh, kk], p)
                out += [m_new, acc]
            return tuple(out)

        def pair(j, state):
            carry, tmax0 = state[:4], state[4:]
            k0 = 2 * j
            tmax1 = produce(k0 + 1, 1)
            carry = consume(k0, 0, tmax0, carry)
            tmax2 = produce(k0 + 2, 0)
            carry = consume(k0 + 1, 1, tmax1, carry)
            return carry + tmax2

        m0 = jnp.full((1, ATT_TQ_SUB), -jnp.inf, F32)
        a0 = jnp.zeros((MLA_VT_ROWS, ATT_TQ_SUB), F32)
        state = lax.fori_loop(0, n_k // 2 - 1, pair, (m0, a0, m0, a0) + produce(0, 0))
        carry, tmax0 = state[:4], state[4:]
        tmax1 = produce(n_k - 1, 1)
        carry = consume(n_k - 2, 0, tmax0, carry)
        carry = consume(n_k - 1, 1, tmax1, carry)
        outs = [carry[2 * hh + 1][:MLA_DV] / carry[2 * hh + 1][MLA_DV:MLA_DV + 1] for hh in range(2)]
        o_ref[0, rows, :] = jnp.concatenate(outs, axis=0).T


def _mla_attn(q, k, vt):
    bsz, s, _ = q.shape
    n_kt = s // ATT_TK
    return pl.pallas_call(
        _mla_attn_kernel,
        grid=(bsz, MLA_HEADS // 2, s // ATT_TQ),
        in_specs=[
            pl.BlockSpec((1, ATT_TQ, 2 * MLA_HEAD_PAD), lambda b, h, i: (b, i, h)),
            pl.BlockSpec((1, s, 2 * MLA_HEAD_PAD), lambda b, h, i: (b, 0, h)),
            pl.BlockSpec((1, 2, n_kt, MLA_VT_ROWS, ATT_TK), lambda b, h, i: (b, h, 0, 0, 0))],
        out_specs=pl.BlockSpec((1, ATT_TQ, 2 * MLA_DV), lambda b, h, i: (b, i, h)),
        out_shape=jax.ShapeDtypeStruct((bsz, s, MLA_W), F32),
        scratch_shapes=[pltpu.VMEM((2, 2, ATT_TK, ATT_TQ_SUB), F32)],
        compiler_params=_cparams(("parallel", "parallel", "arbitrary")),
        name="mla_attn",
    )(q, k, vt)


def _mlstm_chunk(q_ref, k_ref, v_ref, o_ref, cn_ref, m_ref, st_idx, row, h, cum_c, cum_r, g_c, g_r,
                 i_col, f_col, inclusive):
    ii = lax.broadcasted_iota(jnp.int32, (CHUNK, CHUNK), 0)
    jj = lax.broadcasted_iota(jnp.int32, (CHUNK, CHUNK), 1)
    mask = (jj <= ii) if inclusive else (jj > ii)
    edge = CHUNK - 1 if inclusive else 0
    hs = slice(h * ML_DH, (h + 1) * ML_DH)

    b_c = cum_c[:, f_col:f_col + 1]
    b_r = cum_r[f_col:f_col + 1, :]
    li_c = g_c[:, i_col:i_col + 1]
    li_r = g_r[i_col:i_col + 1, :]
    g = b_r[:, edge:edge + 1]
    m = m_ref[st_idx]

    q = q_ref[0, pl.ds(row, CHUNK), hs]
    k = k_ref[0, pl.ds(row, CHUNK), hs] * (ML_DH ** -0.5)
    v = v_ref[0, pl.ds(row, CHUNK), hs]
    ones_col = (lax.broadcasted_iota(jnp.int32, (CHUNK, ML_DH), 1) == 0).astype(BF16)
    v_ext = jnp.concatenate([v, ones_col], axis=-1)

    d = jnp.where(mask, b_c - b_r + li_r, -jnp.inf)
    inter = b_c + m
    m_t = jnp.maximum(inter, jnp.max(d, axis=-1, keepdims=True))
    s = _dot_nt(q, k.astype(BF16)) * jnp.exp(d - m_t)
    e = jnp.exp(inter - m_t)
    cn = cn_ref[st_idx]
    hh = _dot(s.astype(BF16), v_ext) + e * _dot(q, cn.astype(BF16))
    num = hh[:, :ML_DH]
    den = hh[:, ML_DH:ML_DH + 1]
    o_ref[0, pl.ds(row, CHUNK), hs] = num / jnp.maximum(jnp.abs(den), jnp.exp(-m_t))

    lw_r = g - b_r + li_r
    lw_c = g - b_c + li_c
    m_new = jnp.maximum(g + m, jnp.max(lw_r, axis=-1, keepdims=True))
    wk = jnp.exp(lw_c - m_new)
    decay = jnp.exp(g + m - m_new)
    cn_ref[st_idx] = decay * cn + _dot_tn((wk * k).astype(BF16), v_ext)
    m_ref[st_idx] = m_new


def _mlstm_kernel(qf, kf, vf, gcf, grf, qb, kb, vb, gcb, grb, bias_c, bias_r, tril, triu,
                  of, ob, cn_ref, m_ref):
    c = pl.program_id(1)

    @pl.when(c == 0)
    def _():
        cn_ref[...] = jnp.zeros_like(cn_ref)
        m_ref[...] = jnp.zeros_like(m_ref)

    n_chunks = SEQ_TILE // CHUNK
    n_gate = 4 * ML_HEADS
    is_f_c = lax.broadcasted_iota(jnp.int32, (CHUNK, n_gate), 1) >= 2 * ML_HEADS
    is_f_r = lax.broadcasted_iota(jnp.int32, (n_gate, CHUNK), 0) >= 2 * ML_HEADS

    def gates(gc_ref, gr_ref, row, j):
        g_c = gc_ref[0, pl.ds(row, CHUNK), :] + bias_c[...]
        g_r = gr_ref[0, j] + bias_r[:, :CHUNK]
        return jnp.where(is_f_c, _log_sigmoid(g_c), g_c), jnp.where(is_f_r, _log_sigmoid(g_r), g_r)

    def body(j, carry):
        jb = n_chunks - 1 - j
        rf = pl.multiple_of(j * CHUNK, CHUNK)
        rb = pl.multiple_of(jb * CHUNK, CHUNK)
        gcf_v, grf_v = gates(gcf, grf, rf, j)
        gcb_v, grb_v = gates(gcb, grb, rb, jb)
        cum_cf = _tri_left(tril[...], gcf_v)
        cum_rf = _tri_right(grf_v, triu[...])
        cum_cb = _tri_left(triu[...], gcb_v)
        cum_rb = _tri_right(grb_v, tril[...])
        for h in range(ML_HEADS):
            _mlstm_chunk(qf, kf, vf, of, cn_ref, m_ref, h, rf, h, cum_cf, cum_rf, gcf_v, grf_v,
                         h, 2 * ML_HEADS + h, True)
            _mlstm_chunk(qb, kb, vb, ob, cn_ref, m_ref, ML_HEADS + h, rb, h, cum_cb, cum_rb, gcb_v, grb_v,
                         ML_HEADS + h, 3 * ML_HEADS + h, False)
        return carry

    lax.fori_loop(0, n_chunks, body, 0)


def _mlstm(q, k, v, gates, if_bias):
    bsz, s, _ = q.shape
    ts = SEQ_TILE
    ns = s // ts
    n_chunks = ts // CHUNK
    n_gate = 4 * ML_HEADS
    gates_r = gates.reshape(bsz, s // CHUNK, CHUNK, n_gate).transpose(0, 1, 3, 2)
    fwd = lambda b, c: (b, c, 0)
    bwd = lambda b, c: (b, ns - 1 - c, 0)
    fwd4 = lambda b, c: (b, c, 0, 0)
    bwd4 = lambda b, c: (b, ns - 1 - c, 0, 0)
    idx = jnp.arange(CHUNK)
    tril = (idx[None, :] <= idx[:, None]).astype(BF16)
    triu = (idx[None, :] >= idx[:, None]).astype(BF16)

    def seq_specs(imap, imap4):
        return [pl.BlockSpec((1, ts, ML_W), imap), pl.BlockSpec((1, ts, ML_W), imap),
                pl.BlockSpec((1, ts, ML_W), imap), pl.BlockSpec((1, ts, n_gate), imap),
                pl.BlockSpec((1, n_chunks, n_gate, CHUNK), imap4)]

    in_specs = seq_specs(fwd, fwd4) + seq_specs(bwd, bwd4) + [
        _const_spec((1, n_gate)), _const_spec((n_gate, 128)),
        _const_spec((CHUNK, CHUNK)), _const_spec((CHUNK, CHUNK))]
    out_specs = [pl.BlockSpec((1, ts, ML_W), fwd), pl.BlockSpec((1, ts, ML_W), bwd)]
    out_shape = [jax.ShapeDtypeStruct((bsz, s, ML_W), F32)] * 2
    return pl.pallas_call(
        _mlstm_kernel,
        grid=(bsz, ns),
        in_specs=in_specs,
        out_specs=out_specs,
        out_shape=out_shape,
        scratch_shapes=[pltpu.VMEM((2 * ML_HEADS, ML_DH, 2 * ML_DH), F32),
                        pltpu.VMEM((2 * ML_HEADS, 1, 1), F32)],
        compiler_params=_cparams(("parallel", "arbitrary")),
        name="mlstm_scan",
    )(q, k, v, gates, gates_r, q, k, v, gates, gates_r,
      if_bias[None, :], jnp.broadcast_to(if_bias[:, None], (n_gate, 128)), tril, triu)


def _odd_out_kernel(final, att, mgate, hf, hb, mo, lgate, ng, wa, wb, x, fg, o_ref):
    mla_out = (att[0] * _silu(mgate[0])).astype(BF16)
    hm = hf[0] + hb[0]
    mo_v, lg_v = mo[0], lgate[0]
    parts = []
    for h in range(ML_HEADS):
        hs = slice(h * ML_DH, (h + 1) * ML_DH)
        y = _rms(hm[:, hs], ng[...]) * _sigmoid(mo_v[:, hs])
        parts.append((y * _silu(lg_v[:, hs])).astype(BF16))
    ml_out = jnp.concatenate(parts, axis=-1)
    xn = x[0] + (_dot(mla_out, wa[...]) + _dot(ml_out, wb[...]))
    o_ref[0] = _rms(xn, fg[...]) if final else xn


def _odd_out(att, mla_gate, h_f, h_b, mo, ml_gate, norm_g, w_out, x, final_g, final):
    bsz, s, _ = x.shape
    tm = TOKEN_TILE
    cur = lambda b, i: (b, i, 0)
    half = pl.BlockSpec((1, tm, MLA_W), cur)
    in_specs = [half] * 6 + [
        _const_spec((1, ML_DH)), _const_spec((MLA_W, D_MODEL)), _const_spec((ML_W, D_MODEL)),
        pl.BlockSpec((1, tm, D_MODEL), cur), _const_spec((1, D_MODEL))]
    return pl.pallas_call(
        functools.partial(_odd_out_kernel, final),
        grid=(bsz, s // tm),
        in_specs=in_specs,
        out_specs=pl.BlockSpec((1, tm, D_MODEL), cur),
        out_shape=jax.ShapeDtypeStruct((bsz, s, D_MODEL), F32),
        compiler_params=_cparams(("parallel", "parallel")),
        name="odd_out",
    )(att, mla_gate, h_f, h_b, mo, ml_gate, norm_g[None, :],
      w_out[:MLA_W].astype(BF16), w_out[MLA_W:].astype(BF16), x, final_g[None, :])


def _col_split(w, sizes):
    out, off = [], 0
    for n in sizes:
        out.append(w[:, off:off + n].astype(BF16))
        off += n
    return out


def _even_layer(x, g, w_in, a_up, a_bias, gla_norm_g, pool_w, pool_scale, w_out):
    bsz, s, _ = x.shape
    sizes = (GLA_K_TOT, GLA_K_TOT, GLA_V_TOT, GLA_V_TOT, GLA_LR, GLA_LR, POOL_W, POOL_W)
    dts = (F32, F32, BF16, F32, BF16, BF16, F32, F32)
    outs = _norm_proj(x.reshape(bsz * s, D_MODEL), g[None, :], _col_split(w_in, sizes), dts)
    q, k, v, gate, lr_f, lr_b, pool_u, pool_gate = [o.reshape(bsz, s, -1) for o in outs]
    o_f, o_b = _gla(q, k, v, lr_f, lr_b, a_up, a_bias)
    return _even_out(o_f, o_b, gate, gla_norm_g, pool_u, pool_gate, pool_w, pool_scale, w_out, x)


def _odd_layer(x, g, tabs, w_in, q_norm_g, q_up, kv_norm_g, kv_up, if_bias, ml_norm_g, w_out,
               final_g, final):
    bsz, s, _ = x.shape
    half = MLA_ROPE // 2
    off = MLA_Q_LORA + MLA_KV_LORA
    kr_w = w_in[:, off:off + MLA_ROPE]
    w_ext = jnp.concatenate([w_in, -kr_w[:, half:], kr_w[:, :half]], axis=-1)
    sizes = (MLA_Q_LORA, MLA_KV_LORA, MLA_ROPE, MLA_W, ML_W, ML_W, ML_W, ML_W, 4 * ML_HEADS, ML_W,
             MLA_ROPE)
    dts = (F32, F32, F32, F32, BF16, F32, BF16, F32, F32, F32, F32)
    outs = _norm_proj(x.reshape(bsz * s, D_MODEL), g[None, :], _col_split(w_ext, sizes), dts)
    cq, ckv, kr, mla_gate, mq, mk, mv, mo, mif, ml_gate, krr = [o.reshape(bsz, s, -1) for o in outs]
    wq, wkn, place, wvt = _mla_weights(q_up, kv_up)
    qa, ka, vt = _mla_qkv(cq, ckv, kr, krr, tabs, q_norm_g, kv_norm_g, wq, wkn, place, wvt)
    att = _mla_attn(qa, ka, vt)
    h_f, h_b = _mlstm(mq, mk, mv, mif, if_bias)
    return _odd_out(att, mla_gate, h_f, h_b, mo, ml_gate, ml_norm_g, w_out, x, final_g, final)


def _trunk(x, norm_g, final_norm_g, e_w_in, e_gla_a_up, e_gla_a_bias, e_gla_norm_g, e_pool_w,
           e_pool_scale, e_w_out, o_w_in, o_q_norm_g, o_q_up, o_kv_norm_g, o_kv_up, o_if_bias,
           o_mlstm_norm_g, o_w_out):
    depth = norm_g.shape[0]
    tabs = _rope_tables(x.shape[1])
    for layer in range(depth):
        i = layer // 2
        if layer % 2 == 0:
            x = _even_layer(x, norm_g[layer], e_w_in[i], e_gla_a_up[i], e_gla_a_bias[i],
                            e_gla_norm_g[i], e_pool_w[i], e_pool_scale[i], e_w_out[i])
        else:
            x = _odd_layer(x, norm_g[layer], tabs, o_w_in[i], o_q_norm_g[i], o_q_up[i],
                           o_kv_norm_g[i], o_kv_up[i], o_if_bias[i], o_mlstm_norm_g[i], o_w_out[i],
                           final_norm_g, layer == depth - 1)
    return x


def kernel(x_prompt, x_sample, norm_g, final_norm_g, e_w_in, e_gla_a_up, e_gla_a_bias, e_gla_norm_g,
           e_pool_w, e_pool_scale, e_w_out, o_w_in, o_q_norm_g, o_q_up, o_kv_norm_g, o_kv_up,
           o_if_bias, o_mlstm_norm_g, o_w_out):
    params = (norm_g, final_norm_g, e_w_in, e_gla_a_up, e_gla_a_bias, e_gla_norm_g, e_pool_w,
              e_pool_scale, e_w_out, o_w_in, o_q_norm_g, o_q_up, o_kv_norm_g, o_kv_up, o_if_bias,
              o_mlstm_norm_g, o_w_out)
    return (_trunk(x_prompt, *params), _trunk(x_sample, *params))
```

```python
import functools
import math

import jax
import jax.numpy as jnp
from jax import lax
from jax.experimental import pallas as pl
from jax.experimental.pallas import tpu as pltpu

F32 = jnp.float32
BF16 = jnp.bfloat16

D_MODEL = 1024
NORM_EPS = 1e-6
CHUNK = 64

GLA_HEADS = 4
GLA_DK = 128
GLA_DV = 256
GLA_LR = 16
GLA_GATE_NORM = 16.0
GLA_K_TOT = GLA_HEADS * GLA_DK
GLA_V_TOT = GLA_HEADS * GLA_DV

POOL_GROUPS = 4
POOL_WINDOWS = (2, 4, 8, 16)
POOL_DG = 128
POOL_W = POOL_GROUPS * POOL_DG
POOL_HALO = 8

MLA_HEADS = 8
MLA_NOPE = 64
MLA_ROPE = 32
MLA_DV = 64
MLA_Q_LORA = 384
MLA_KV_LORA = 256
MLA_W = MLA_HEADS * MLA_DV
MLA_HEAD_PAD = 128
MLA_VT_ROWS = 80
ROPE_THETA = 10000.0

ML_HEADS = 4
ML_DH = 128
ML_W = ML_HEADS * ML_DH
ML_CHUNK = 128

VMEM_LIMIT_BYTES = 56 * 1024 * 1024

TOKEN_TILE = 512
SEQ_TILE = 512
ATT_TQ = 512
ATT_TQ_SUB = 512
ATT_TK = 512


def _cparams(sem):
    return pltpu.CompilerParams(dimension_semantics=sem, vmem_limit_bytes=VMEM_LIMIT_BYTES)


def _const_spec(shape):
    nd = len(shape)
    return pl.BlockSpec(shape, lambda *_: (0,) * nd)


def _dot(a, b):
    return jnp.dot(a, b, preferred_element_type=F32)


def _dot_nt(a, b):
    return lax.dot_general(a, b, (((1,), (1,)), ((), ())), preferred_element_type=F32)


def _dot_tn(a, b):
    return lax.dot_general(a, b, (((0,), (0,)), ((), ())), preferred_element_type=F32)


def _sigmoid(x):
    return 1.0 / (1.0 + jnp.exp(-x))


def _silu(x):
    return x * _sigmoid(x)


def _log_sigmoid(x):
    return jnp.minimum(x, 0.0) - jnp.log(1.0 + jnp.exp(-jnp.abs(x)))


def _rms(x, g):
    return x * lax.rsqrt(jnp.mean(x * x, axis=-1, keepdims=True) + NORM_EPS) * g


def _split3(x):
    hi = x.astype(BF16)
    r1 = x - hi.astype(F32)
    mid = r1.astype(BF16)
    lo = (r1 - mid.astype(F32)).astype(BF16)
    return hi, mid, lo


def _tri_left(tri, x):
    hi, mid, lo = _split3(x)
    return _dot(tri, hi) + _dot(tri, mid) + _dot(tri, lo)


def _tri_right(x, tri):
    hi, mid, lo = _split3(x)
    return _dot(hi, tri) + _dot(mid, tri) + _dot(lo, tri)


def _norm_proj_kernel(n_out, x_ref, g_ref, *refs):
    w_refs, o_refs = refs[:n_out], refs[n_out:]
    h = _rms(x_ref[...], g_ref[...]).astype(BF16)
    for w_ref, o_ref in zip(w_refs, o_refs):
        o_ref[...] = _dot(h, w_ref[...]).astype(o_ref.dtype)


def _norm_proj(x2d, g, weights, out_dtypes):
    t = x2d.shape[0]
    tm = TOKEN_TILE
    n_out = len(weights)
    in_specs = [pl.BlockSpec((tm, D_MODEL), lambda i: (i, 0)), _const_spec((1, D_MODEL))]
    in_specs += [_const_spec(w.shape) for w in weights]
    out_specs = [pl.BlockSpec((tm, w.shape[1]), lambda i: (i, 0)) for w in weights]
    out_shape = [jax.ShapeDtypeStruct((t, w.shape[1]), dt) for w, dt in zip(weights, out_dtypes)]
    return pl.pallas_call(
        functools.partial(_norm_proj_kernel, n_out),
        grid=(t // tm,),
        in_specs=in_specs,
        out_specs=out_specs,
        out_shape=out_shape,
        compiler_params=_cparams(("parallel",)),
        name="norm_proj",
    )(x2d, g, *weights)


def _gla_chunk(q_ref, k_ref, v_ref, o_ref, st_ref, st_idx, row, b_all, inclusive):
    ii = lax.broadcasted_iota(jnp.int32, (CHUNK, CHUNK), 0)
    jj = lax.broadcasted_iota(jnp.int32, (CHUNK, CHUNK), 1)
    mask = (jj <= ii) if inclusive else (jj > ii)
    edge = CHUNK - 1 if inclusive else 0
    for h in range(GLA_HEADS):
        ks = slice(h * GLA_DK, (h + 1) * GLA_DK)
        vs = slice(h * GLA_DV, (h + 1) * GLA_DV)
        q = q_ref[0, pl.ds(row, CHUNK), ks] * (GLA_DK ** -0.5)
        k = k_ref[0, pl.ds(row, CHUNK), ks]
        v = v_ref[0, pl.ds(row, CHUNK), vs]
        b = b_all[:, ks]
        b_edge = b[edge:edge + 1, :]
        qe = (q * jnp.exp(b)).astype(BF16)
        kd = (k * jnp.exp(-b)).astype(BF16)
        a = jnp.where(mask, _dot_nt(qe, kd), 0.0).astype(BF16)
        st = st_ref[st_idx + h]
        o = _dot(a, v) + _dot_nt(qe, st.astype(BF16))
        o_ref[0, pl.ds(row, CHUNK), vs] = o
        kdec = (k * jnp.exp(b_edge - b)).astype(BF16)
        st_ref[st_idx + h] = st * jnp.exp(b_edge) + _dot_tn(v, kdec)


def _gla_kernel(qf, kf, vf, lrf, qb, kb, vb, lrb, aupf, aupb, biasf, biasb, tril, triu,
                of, ob, st_ref, la_ref):
    c = pl.program_id(1)

    @pl.when(c == 0)
    def _():
        st_ref[...] = jnp.zeros_like(st_ref)

    inv = 1.0 / GLA_GATE_NORM
    la_ref[0] = _log_sigmoid(_dot(lrf[0], aupf[...]) + biasf[...]) * inv
    la_ref[1] = _log_sigmoid(_dot(lrb[0], aupb[...]) + biasb[...]) * inv
    n_chunks = SEQ_TILE // CHUNK

    def body(j, carry):
        rf = pl.multiple_of(j * CHUNK, CHUNK)
        rb = pl.multiple_of((n_chunks - 1 - j) * CHUNK, CHUNK)
        b_f = _tri_left(tril[...], la_ref[0, pl.ds(rf, CHUNK), :])
        b_b = _tri_left(triu[...], la_ref[1, pl.ds(rb, CHUNK), :])
        _gla_chunk(qf, kf, vf, of, st_ref, 0, rf, b_f, True)
        _gla_chunk(qb, kb, vb, ob, st_ref, GLA_HEADS, rb, b_b, False)
        return carry

    lax.fori_loop(0, n_chunks, body, 0)


def _gla(q, k, v, lr_f, lr_b, a_up, a_bias):
    bsz, s, _ = q.shape
    ts = SEQ_TILE
    ns = s // ts
    fwd = lambda b, c: (b, c, 0)
    bwd = lambda b, c: (b, ns - 1 - c, 0)
    idx = jnp.arange(CHUNK)
    tril = (idx[None, :] <= idx[:, None]).astype(BF16)
    triu = (idx[None, :] >= idx[:, None]).astype(BF16)

    def seq_specs(imap):
        return [pl.BlockSpec((1, ts, GLA_K_TOT), imap), pl.BlockSpec((1, ts, GLA_K_TOT), imap),
                pl.BlockSpec((1, ts, GLA_V_TOT), imap), pl.BlockSpec((1, ts, GLA_LR), imap)]

    in_specs = seq_specs(fwd) + seq_specs(bwd) + [
        _const_spec((GLA_LR, GLA_K_TOT)), _const_spec((GLA_LR, GLA_K_TOT)),
        _const_spec((1, GLA_K_TOT)), _const_spec((1, GLA_K_TOT)),
        _const_spec((CHUNK, CHUNK)), _const_spec((CHUNK, CHUNK))]
    out_specs = [pl.BlockSpec((1, ts, GLA_V_TOT), fwd), pl.BlockSpec((1, ts, GLA_V_TOT), bwd)]
    out_shape = [jax.ShapeDtypeStruct((bsz, s, GLA_V_TOT), F32)] * 2
    return pl.pallas_call(
        _gla_kernel,
        grid=(bsz, ns),
        in_specs=in_specs,
        out_specs=out_specs,
        out_shape=out_shape,
        scratch_shapes=[pltpu.VMEM((2 * GLA_HEADS, GLA_DV, GLA_DK), F32),
                        pltpu.VMEM((2, ts, GLA_K_TOT), F32)],
        compiler_params=_cparams(("parallel", "arbitrary")),
        name="gla_scan",
    )(q, k, v, lr_f, q, k, v, lr_b,
      a_up[0].astype(BF16), a_up[1].astype(BF16), a_bias[0:1], a_bias[1:2], tril, triu)


def _even_out_kernel(seq_len, of, ob, gate, ng, pu, pprev, pnext, pgate, pw, pscale, wa, wb, x, o_ref):
    tm = of.shape[1]
    i = pl.program_id(1)
    n_i = pl.num_programs(1)

    o = of[0] + ob[0]
    g_all = gate[0]
    parts = []
    for h in range(GLA_HEADS):
        vs = slice(h * GLA_DV, (h + 1) * GLA_DV)
        parts.append((_rms(o[:, vs], ng[...]) * _silu(g_all[:, vs])).astype(BF16))
    gla_out = jnp.concatenate(parts, axis=-1)

    u = pu[0]
    prev = jnp.where(i > 0, pprev[0], 0.0)
    nxt = jnp.where(i < n_i - 1, pnext[0], 0.0)
    ext = jnp.concatenate([prev, u, nxt], axis=0)
    n_ext = tm + 2 * POOL_HALO
    pos = i * tm + lax.broadcasted_iota(jnp.int32, (tm, 1), 0)
    mixed = []
    for gi, w in enumerate(POOL_WINDOWS):
        cs = slice(gi * POOL_DG, (gi + 1) * POOL_DG)
        a = ext[:, cs]
        span = 1
        while span < w:
            a = a + pltpu.roll(a, span, 0)
            span *= 2
        shift = w // 2 - 1
        if shift:
            a = pltpu.roll(a, n_ext - shift, 0)
        win = a[POOL_HALO:POOL_HALO + tm]
        lo = jnp.maximum(pos - w // 2, 0)
        hi = jnp.minimum(pos + w // 2, seq_len)
        pooled = win / (hi - lo).astype(F32) - u[:, cs]
        mixed.append(_dot(pooled.astype(BF16), pw[gi]))
    pool_out = (jnp.concatenate(mixed, axis=-1) * pscale[...] * _silu(pgate[0])).astype(BF16)

    y = _dot(gla_out, wa[...]) + _dot(pool_out, wb[...])
    o_ref[0] = x[0] + y


def _even_out(o_f, o_b, gate, norm_g, pool_u, pool_gate, pool_w, pool_scale, w_out, x):
    bsz, s, _ = x.shape
    tm = TOKEN_TILE
    nh = tm // POOL_HALO
    n_halo = s // POOL_HALO
    cur = lambda b, i: (b, i, 0)
    in_specs = [
        pl.BlockSpec((1, tm, GLA_V_TOT), cur), pl.BlockSpec((1, tm, GLA_V_TOT), cur),
        pl.BlockSpec((1, tm, GLA_V_TOT), cur), _const_spec((1, GLA_DV)),
        pl.BlockSpec((1, tm, POOL_W), cur),
        pl.BlockSpec((1, POOL_HALO, POOL_W), lambda b, i: (b, jnp.maximum(i * nh - 1, 0), 0)),
        pl.BlockSpec((1, POOL_HALO, POOL_W), lambda b, i: (b, jnp.minimum((i + 1) * nh, n_halo - 1), 0)),
        pl.BlockSpec((1, tm, POOL_W), cur),
        _const_spec((POOL_GROUPS, POOL_DG, POOL_DG)), _const_spec((1, POOL_W)),
        _const_spec((GLA_V_TOT, D_MODEL)), _const_spec((POOL_W, D_MODEL)),
        pl.BlockSpec((1, tm, D_MODEL), cur)]
    return pl.pallas_call(
        functools.partial(_even_out_kernel, s),
        grid=(bsz, s // tm),
        in_specs=in_specs,
        out_specs=pl.BlockSpec((1, tm, D_MODEL), cur),
        out_shape=jax.ShapeDtypeStruct((bsz, s, D_MODEL), F32),
        compiler_params=_cparams(("parallel", "parallel")),
        name="even_out",
    )(o_f, o_b, gate, norm_g[None, :], pool_u, pool_u, pool_u, pool_gate,
      pool_w.astype(BF16), pool_scale[None, :],
      w_out[:GLA_V_TOT].astype(BF16), w_out[GLA_V_TOT:].astype(BF16), x)


def _mla_qkv_kernel(cq, ckv, kr, krr, cos_h, sin_h, cos_r, sin_r, qg, kvg, wq, wkn, place, wvt,
                    q_ref, k_ref, vt_ref):
    qk_scale = math.log2(math.e) * (MLA_NOPE + MLA_ROPE) ** -0.5
    w_all = MLA_HEADS * MLA_HEAD_PAD
    nq = _rms(cq[0], qg[...]).astype(BF16)
    qq = _dot(nq, wq[...])
    cos2 = jnp.concatenate([cos_h[...], cos_h[...]], axis=-1)
    sin2 = jnp.concatenate([sin_h[...], sin_h[...]], axis=-1)
    for p in range(MLA_HEADS // 2):
        sl = slice(2 * p * MLA_HEAD_PAD, (2 * p + 2) * MLA_HEAD_PAD)
        sr = slice(w_all + 2 * p * MLA_HEAD_PAD, w_all + (2 * p + 2) * MLA_HEAD_PAD)
        q_ref[0, :, sl] = ((qq[:, sl] * cos2 + qq[:, sr] * sin2) * qk_scale).astype(BF16)

    nkv = _rms(ckv[0], kvg[...]).astype(BF16)
    k_rope = (kr[0] * cos_r[...] + krr[0] * sin_r[...]).astype(BF16)
    k_ref[0] = (_dot(nkv, wkn[...]) + _dot(k_rope, place[...])).astype(BF16)

    vt = _dot_nt(wvt[...], nkv)
    ones_row = lax.broadcasted_iota(jnp.int32, (MLA_VT_ROWS, 1), 0) == MLA_DV
    for h in range(MLA_HEADS):
        vh = vt[h * MLA_VT_ROWS:(h + 1) * MLA_VT_ROWS]
        vt_ref[0, h, 0] = jnp.where(ones_row, 1.0, vh).astype(BF16)


def _mla_qkv(cq, ckv, kr, krr, tabs, q_norm_g, kv_norm_g, wq, wkn, place, wvt):
    bsz, s, _ = cq.shape
    tm = ATT_TK
    w_all = MLA_HEADS * MLA_HEAD_PAD
    cur = lambda b, i: (b, i, 0)
    tab = lambda b, i: (i, 0)
    cos_h, sin_h, cos_r, sin_r = tabs
    in_specs = [
        pl.BlockSpec((1, tm, MLA_Q_LORA), cur), pl.BlockSpec((1, tm, MLA_KV_LORA), cur),
        pl.BlockSpec((1, tm, MLA_ROPE), cur), pl.BlockSpec((1, tm, MLA_ROPE), cur),
        pl.BlockSpec((tm, MLA_HEAD_PAD), tab), pl.BlockSpec((tm, MLA_HEAD_PAD), tab),
        pl.BlockSpec((tm, MLA_ROPE), tab), pl.BlockSpec((tm, MLA_ROPE), tab),
        _const_spec((1, MLA_Q_LORA)), _const_spec((1, MLA_KV_LORA)),
        _const_spec(wq.shape), _const_spec(wkn.shape), _const_spec(place.shape), _const_spec(wvt.shape)]
    out_specs = [
        pl.BlockSpec((1, tm, w_all), cur), pl.BlockSpec((1, tm, w_all), cur),
        pl.BlockSpec((1, MLA_HEADS, 1, MLA_VT_ROWS, tm), lambda b, i: (b, 0, i, 0, 0))]
    out_shape = [
        jax.ShapeDtypeStruct((bsz, s, w_all), BF16), jax.ShapeDtypeStruct((bsz, s, w_all), BF16),
        jax.ShapeDtypeStruct((bsz, MLA_HEADS, s // tm, MLA_VT_ROWS, tm), BF16)]
    return pl.pallas_call(
        _mla_qkv_kernel,
        grid=(bsz, s // tm),
        in_specs=in_specs,
        out_specs=out_specs,
        out_shape=out_shape,
        compiler_params=_cparams(("parallel", "parallel")),
        name="mla_qkv",
    )(cq, ckv, kr, krr, cos_h, sin_h, cos_r, sin_r, q_norm_g[None, :], kv_norm_g[None, :],
      wq, wkn, place, wvt)


def _mla_weights(q_up, kv_up):
    dq = MLA_NOPE + MLA_ROPE
    half = MLA_ROPE // 2
    qh = q_up.reshape(MLA_Q_LORA, MLA_HEADS, dq)
    zeros = jnp.zeros((MLA_Q_LORA, MLA_HEADS, MLA_HEAD_PAD - dq), F32)
    main = jnp.concatenate([qh, zeros], axis=-1)
    x1 = qh[..., MLA_NOPE:MLA_NOPE + half]
    x2 = qh[..., MLA_NOPE + half:]
    rot = jnp.concatenate([jnp.zeros((MLA_Q_LORA, MLA_HEADS, MLA_NOPE), F32), -x2, x1, zeros], axis=-1)
    w_all = MLA_HEADS * MLA_HEAD_PAD
    wq = jnp.concatenate([main.reshape(MLA_Q_LORA, w_all), rot.reshape(MLA_Q_LORA, w_all)], axis=-1)

    kvh = kv_up.reshape(MLA_KV_LORA, MLA_HEADS, MLA_NOPE + MLA_DV)
    wkn = jnp.concatenate(
        [kvh[..., :MLA_NOPE], jnp.zeros((MLA_KV_LORA, MLA_HEADS, MLA_HEAD_PAD - MLA_NOPE), F32)],
        axis=-1).reshape(MLA_KV_LORA, w_all)
    wv = jnp.transpose(kvh[..., MLA_NOPE:], (1, 2, 0))
    wvt = jnp.concatenate(
        [wv, jnp.zeros((MLA_HEADS, MLA_VT_ROWS - MLA_DV, MLA_KV_LORA), F32)],
        axis=1).reshape(MLA_HEADS * MLA_VT_ROWS, MLA_KV_LORA)
    r = jnp.arange(MLA_ROPE)
    cols = jnp.arange(w_all)
    place = ((cols[None, :] % MLA_HEAD_PAD) == (MLA_NOPE + r[:, None])).astype(BF16)
    return wq.astype(BF16), wkn.astype(BF16), place, wvt.astype(BF16)


def _rope_tables(s):
    inv = ROPE_THETA ** (-jnp.arange(0, MLA_ROPE, 2, dtype=F32) / MLA_ROPE)
    ang = jnp.arange(s, dtype=F32)[:, None] * inv[None, :]
    cos, sin = jnp.cos(ang), jnp.sin(ang)
    pad = jnp.zeros((s, MLA_HEAD_PAD - MLA_NOPE - MLA_ROPE), F32)
    cos_h = jnp.concatenate([jnp.ones((s, MLA_NOPE), F32), cos, cos, pad], axis=-1)
    sin_h = jnp.concatenate([jnp.zeros((s, MLA_NOPE), F32), sin, sin, pad], axis=-1)
    cos_r = jnp.concatenate([cos, cos], axis=-1)
    sin_r = jnp.concatenate([sin, sin], axis=-1)
    return cos_h, sin_h, cos_r, sin_r


def _mla_attn_kernel(q_ref, k_ref, vt_ref, o_ref, s_ref):
    n_k = k_ref.shape[1] // ATT_TK
    for qs in range(ATT_TQ // ATT_TQ_SUB):
        rows = slice(qs * ATT_TQ_SUB, (qs + 1) * ATT_TQ_SUB)
        qts = (q_ref[0, rows, 0:MLA_HEAD_PAD], q_ref[0, rows, MLA_HEAD_PAD:2 * MLA_HEAD_PAD])

        def produce(kk, buf):
            r = pl.multiple_of(kk * ATT_TK, ATT_TK)
            tile_max = []
            for hh in range(2):
                kt = k_ref[0, pl.ds(r, ATT_TK), hh * MLA_HEAD_PAD:(hh + 1) * MLA_HEAD_PAD]
                st = _dot_nt(kt, qts[hh])
                s_ref[buf, hh] = st
                tile_max.append(jnp.max(st, axis=0, keepdims=True))
            return tuple(tile_max)

        def consume(kk, buf, tile_max, carry):
            out = []
            for hh in range(2):
                m, acc = carry[2 * hh], carry[2 * hh + 1]
                m_new = jnp.maximum(m, tile_max[hh])
                alpha = jnp.exp2(m - m_new)
                p = jnp.exp2(s_ref[buf, hh] - m_new).astype(BF16)
                acc = acc * alpha + _dot(vt_ref[0, hh, kk], p)
                out += [m_new, acc]
            return tuple(out)

        def pair(j, state):
            carry, tmax0 = state[:4], state[4:]
            k0 = 2 * j
            tmax1 = produce(k0 + 1, 1)
            carry = consume(k0, 0, tmax0, carry)
            tmax2 = produce(k0 + 2, 0)
            carry = consume(k0 + 1, 1, tmax1, carry)
            return carry + tmax2

        m0 = jnp.full((1, ATT_TQ_SUB), -jnp.inf, F32)
        a0 = jnp.zeros((MLA_VT_ROWS, ATT_TQ_SUB), F32)
        state = lax.fori_loop(0, n_k // 2 - 1, pair, (m0, a0, m0, a0) + produce(0, 0))
        carry, tmax0 = state[:4], state[4:]
        tmax1 = produce(n_k - 1, 1)
        carry = consume(n_k - 2, 0, tmax0, carry)
        carry = consume(n_k - 1, 1, tmax1, carry)
        outs = [carry[2 * hh + 1][:MLA_DV] / carry[2 * hh + 1][MLA_DV:MLA_DV + 1] for hh in range(2)]
        o_ref[0, rows, :] = jnp.concatenate(outs, axis=0).T


def _mla_attn(q, k, vt):
    bsz, s, _ = q.shape
    n_kt = s // ATT_TK
    return pl.pallas_call(
        _mla_attn_kernel,
        grid=(bsz, MLA_HEADS // 2, s // ATT_TQ),
        in_specs=[
            pl.BlockSpec((1, ATT_TQ, 2 * MLA_HEAD_PAD), lambda b, h, i: (b, i, h)),
            pl.BlockSpec((1, s, 2 * MLA_HEAD_PAD), lambda b, h, i: (b, 0, h)),
            pl.BlockSpec((1, 2, n_kt, MLA_VT_ROWS, ATT_TK), lambda b, h, i: (b, h, 0, 0, 0))],
        out_specs=pl.BlockSpec((1, ATT_TQ, 2 * MLA_DV), lambda b, h, i: (b, i, h)),
        out_shape=jax.ShapeDtypeStruct((bsz, s, MLA_W), F32),
        scratch_shapes=[pltpu.VMEM((2, 2, ATT_TK, ATT_TQ_SUB), F32)],
        compiler_params=_cparams(("parallel", "parallel", "arbitrary")),
        name="mla_attn",
    )(q, k, vt)


def _lane_scan(x, op, identity, reverse):
    lane = lax.broadcasted_iota(jnp.int32, x.shape, 1)
    step = 1
    while step < ML_CHUNK:
        if reverse:
            shifted = jnp.where(lane < ML_CHUNK - step, pltpu.roll(x, ML_CHUNK - step, 1), identity)
        else:
            shifted = jnp.where(lane >= step, pltpu.roll(x, step, 1), identity)
        x = op(x, shifted)
        step *= 2
    return x


def _mlstm_prologue(grf, grb, bias_r, eye, b_ref, cmax_ref, ctot_ref, gtot_ref, ccol_ref):
    n_chunks = SEQ_TILE // ML_CHUNK
    n_dir = 2 * ML_HEADS
    rows16 = lax.broadcasted_iota(jnp.int32, (1, 2 * n_dir, 1), 1)
    g = jnp.where((rows16 % n_dir) < ML_HEADS, grf[0], grb[0]) + bias_r[...][None]
    li = g[:, :n_dir].reshape(n_chunks * n_dir, ML_CHUNK)
    lf = _log_sigmoid(g[:, n_dir:]).reshape(n_chunks * n_dir, ML_CHUNK)
    is_fwd = (lax.broadcasted_iota(jnp.int32, li.shape, 0) % n_dir) < ML_HEADS
    lane = lax.broadcasted_iota(jnp.int32, li.shape, 1)
    b = jnp.where(is_fwd, _lane_scan(lf, jnp.add, 0.0, False), _lane_scan(lf, jnp.add, 0.0, True))
    c = li - b
    c_next = jnp.where(lane < ML_CHUNK - 1, pltpu.roll(c, ML_CHUNK - 1, 1), -jnp.inf)
    cmax = jnp.where(is_fwd, _lane_scan(c, jnp.maximum, -jnp.inf, False),
                     _lane_scan(c_next, jnp.maximum, -jnp.inf, True))
    shape3 = (n_chunks, n_dir, ML_CHUNK)
    b_ref[...] = b.reshape(shape3)
    cmax_ref[...] = cmax.reshape(shape3)
    ctot_ref[...] = jnp.broadcast_to(jnp.max(c, axis=-1, keepdims=True), c.shape).reshape(shape3)
    gtot_ref[...] = jnp.broadcast_to(jnp.sum(lf, axis=-1, keepdims=True), c.shape).reshape(shape3)
    hi, mid, lo = _split3(c)
    for jc in range(n_chunks):
        rs = slice(jc * n_dir, (jc + 1) * n_dir)
        ccol_ref[jc] = _dot_nt(eye, hi[rs]) + _dot_nt(eye, mid[rs]) + _dot_nt(eye, lo[rs])


def _mlstm_chunk(q_ref, k_ref, v_ref, o_ref, cnt_ref, m_ref, n, row, jc, eye, stats, inclusive):
    b_ref, cmax_ref, ctot_ref, gtot_ref, ccol_ref = stats
    h = n % ML_HEADS
    hs = slice(h * ML_DH, (h + 1) * ML_DH)
    jj = lax.broadcasted_iota(jnp.int32, (ML_CHUNK, ML_CHUNK), 0)
    ii = lax.broadcasted_iota(jnp.int32, (ML_CHUNK, ML_CHUNK), 1)
    mask = (jj <= ii) if inclusive else (jj > ii)

    m = m_ref[n]
    b_r = b_ref[jc, n:n + 1, :]
    big_m = jnp.maximum(m, cmax_ref[jc, n:n + 1, :])
    m_tot = jnp.maximum(m, ctot_ref[jc, n:n + 1, 0:1])
    g_tot = gtot_ref[jc, n:n + 1, 0:1]
    c_b = jnp.broadcast_to(ccol_ref[jc, :, n:n + 1], (ML_CHUNK, ML_CHUNK))

    q = q_ref[0, pl.ds(row, ML_CHUNK), hs]
    ks = k_ref[0, pl.ds(row, ML_CHUNK), hs] * (ML_DH ** -0.5)
    v = v_ref[0, pl.ds(row, ML_CHUNK), hs]
    ones_row = (lax.broadcasted_iota(jnp.int32, (ML_DH, ML_CHUNK), 0) == 0).astype(BF16)
    vt_ext = jnp.concatenate([_dot_nt(eye, v).astype(BF16), ones_row], axis=0)

    cnt = cnt_ref[n]
    lhs = jnp.concatenate([ks.astype(BF16), cnt.astype(BF16)], axis=0)
    sr = _dot_nt(lhs, q)
    pt = jnp.exp(jnp.where(mask, c_b - big_m, -jnp.inf))
    st = (sr[:ML_CHUNK] * pt).astype(BF16)
    kw = (ks * jnp.exp(c_b - m_tot)).astype(BF16)
    ho = _dot(vt_ext, jnp.concatenate([st, kw], axis=-1))
    ht = ho[:, :ML_CHUNK] + jnp.exp(m - big_m) * sr[ML_CHUNK:]
    den = ht[ML_DH:ML_DH + 1]
    scale = 1.0 / jnp.maximum(jnp.abs(den), jnp.exp(-(b_r + big_m)))
    o_ref[0, pl.ds(row, ML_CHUNK), hs] = (ht[:ML_DH] * scale).T
    cnt_ref[n] = jnp.exp(m - m_tot) * cnt + ho[:, ML_CHUNK:]
    m_ref[n] = g_tot + m_tot


def _mlstm_kernel(qf, kf, vf, grf, qb, kb, vb, grb, bias_r, eye_ref, of, ob,
                  cnt_ref, m_ref, b_ref, cmax_ref, ctot_ref, gtot_ref, ccol_ref):
    c = pl.program_id(1)

    @pl.when(c == 0)
    def _():
        cnt_ref[...] = jnp.zeros_like(cnt_ref)
        m_ref[...] = jnp.zeros_like(m_ref)

    eye = eye_ref[...]
    stats = (b_ref, cmax_ref, ctot_ref, gtot_ref, ccol_ref)
    _mlstm_prologue(grf, grb, bias_r, eye, *stats)
    n_chunks = SEQ_TILE // ML_CHUNK

    def body(j, carry):
        jb = n_chunks - 1 - j
        rf = pl.multiple_of(j * ML_CHUNK, ML_CHUNK)
        rb = pl.multiple_of(jb * ML_CHUNK, ML_CHUNK)
        for h in range(ML_HEADS):
            _mlstm_chunk(qf, kf, vf, of, cnt_ref, m_ref, h, rf, j, eye, stats, True)
            _mlstm_chunk(qb, kb, vb, ob, cnt_ref, m_ref, ML_HEADS + h, rb, jb, eye, stats, False)
        return carry

    lax.fori_loop(0, n_chunks, body, 0)


def _mlstm(q, k, v, gates, if_bias):
    bsz, s, _ = q.shape
    ts = SEQ_TILE
    ns = s // ts
    n_chunks = ts // ML_CHUNK
    n_gate = 4 * ML_HEADS
    n_dir = 2 * ML_HEADS
    gates_r = gates.reshape(bsz, s // ML_CHUNK, ML_CHUNK, n_gate).transpose(0, 1, 3, 2)
    fwd = lambda b, c: (b, c, 0)
    bwd = lambda b, c: (b, ns - 1 - c, 0)
    fwd4 = lambda b, c: (b, c, 0, 0)
    bwd4 = lambda b, c: (b, ns - 1 - c, 0, 0)

    def seq_specs(imap, imap4):
        return [pl.BlockSpec((1, ts, ML_W), imap), pl.BlockSpec((1, ts, ML_W), imap),
                pl.BlockSpec((1, ts, ML_W), imap),
                pl.BlockSpec((1, n_chunks, n_gate, ML_CHUNK), imap4)]

    in_specs = seq_specs(fwd, fwd4) + seq_specs(bwd, bwd4) + [
        _const_spec((n_gate, ML_CHUNK)), _const_spec((ML_CHUNK, ML_CHUNK))]
    out_specs = [pl.BlockSpec((1, ts, ML_W), fwd), pl.BlockSpec((1, ts, ML_W), bwd)]
    out_shape = [jax.ShapeDtypeStruct((bsz, s, ML_W), F32)] * 2
    stat = pltpu.VMEM((n_chunks, n_dir, ML_CHUNK), F32)
    return pl.pallas_call(
        _mlstm_kernel,
        grid=(bsz, ns),
        in_specs=in_specs,
        out_specs=out_specs,
        out_shape=out_shape,
        scratch_shapes=[pltpu.VMEM((n_dir, 2 * ML_DH, ML_DH), F32), pltpu.VMEM((n_dir, 1, 1), F32),
                        stat, stat, stat, stat, pltpu.VMEM((n_chunks, ML_CHUNK, n_dir), F32)],
        compiler_params=_cparams(("parallel", "arbitrary")),
        name="mlstm_scan",
    )(q, k, v, gates_r, q, k, v, gates_r,
      jnp.broadcast_to(if_bias[:, None], (n_gate, ML_CHUNK)), jnp.eye(ML_CHUNK, dtype=BF16))


def _odd_out_kernel(final, att, mgate, hf, hb, mo, lgate, ng, wa, wb, x, fg, o_ref):
    mla_out = (att[0] * _silu(mgate[0])).astype(BF16)
    hm = hf[0] + hb[0]
    mo_v, lg_v = mo[0], lgate[0]
    parts = []
    for h in range(ML_HEADS):
        hs = slice(h * ML_DH, (h + 1) * ML_DH)
        y = _rms(hm[:, hs], ng[...]) * _sigmoid(mo_v[:, hs])
        parts.append((y * _silu(lg_v[:, hs])).astype(BF16))
    ml_out = jnp.concatenate(parts, axis=-1)
    xn = x[0] + (_dot(mla_out, wa[...]) + _dot(ml_out, wb[...]))
    o_ref[0] = _rms(xn, fg[...]) if final else xn


def _odd_out(att, mla_gate, h_f, h_b, mo, ml_gate, norm_g, w_out, x, final_g, final):
    bsz, s, _ = x.shape
    tm = TOKEN_TILE
    cur = lambda b, i: (b, i, 0)
    half = pl.BlockSpec((1, tm, MLA_W), cur)
    in_specs = [half] * 6 + [
        _const_spec((1, ML_DH)), _const_spec((MLA_W, D_MODEL)), _const_spec((ML_W, D_MODEL)),
        pl.BlockSpec((1, tm, D_MODEL), cur), _const_spec((1, D_MODEL))]
    return pl.pallas_call(
        functools.partial(_odd_out_kernel, final),
        grid=(bsz, s // tm),
        in_specs=in_specs,
        out_specs=pl.BlockSpec((1, tm, D_MODEL), cur),
        out_shape=jax.ShapeDtypeStruct((bsz, s, D_MODEL), F32),
        compiler_params=_cparams(("parallel", "parallel")),
        name="odd_out",
    )(att, mla_gate, h_f, h_b, mo, ml_gate, norm_g[None, :],
      w_out[:MLA_W].astype(BF16), w_out[MLA_W:].astype(BF16), x, final_g[None, :])


def _col_split(w, sizes):
    out, off = [], 0
    for n in sizes:
        out.append(w[:, off:off + n].astype(BF16))
        off += n
    return out


def _even_layer(x, g, w_in, a_up, a_bias, gla_norm_g, pool_w, pool_scale, w_out):
    bsz, s, _ = x.shape
    sizes = (GLA_K_TOT, GLA_K_TOT, GLA_V_TOT, GLA_V_TOT, GLA_LR, GLA_LR, POOL_W, POOL_W)
    dts = (F32, F32, BF16, F32, BF16, BF16, F32, F32)
    outs = _norm_proj(x.reshape(bsz * s, D_MODEL), g[None, :], _col_split(w_in, sizes), dts)
    q, k, v, gate, lr_f, lr_b, pool_u, pool_gate = [o.reshape(bsz, s, -1) for o in outs]
    o_f, o_b = _gla(q, k, v, lr_f, lr_b, a_up, a_bias)
    return _even_out(o_f, o_b, gate, gla_norm_g, pool_u, pool_gate, pool_w, pool_scale, w_out, x)


def _odd_layer(x, g, tabs, w_in, q_norm_g, q_up, kv_norm_g, kv_up, if_bias, ml_norm_g, w_out,
               final_g, final):
    bsz, s, _ = x.shape
    half = MLA_ROPE // 2
    off = MLA_Q_LORA + MLA_KV_LORA
    kr_w = w_in[:, off:off + MLA_ROPE]
    w_ext = jnp.concatenate([w_in, -kr_w[:, half:], kr_w[:, :half]], axis=-1)
    sizes = (MLA_Q_LORA, MLA_KV_LORA, MLA_ROPE, MLA_W, ML_W, ML_W, ML_W, ML_W, 4 * ML_HEADS, ML_W,
             MLA_ROPE)
    dts = (F32, F32, F32, F32, BF16, F32, BF16, F32, F32, F32, F32)
    outs = _norm_proj(x.reshape(bsz * s, D_MODEL), g[None, :], _col_split(w_ext, sizes), dts)
    cq, ckv, kr, mla_gate, mq, mk, mv, mo, mif, ml_gate, krr = [o.reshape(bsz, s, -1) for o in outs]
    wq, wkn, place, wvt = _mla_weights(q_up, kv_up)
    qa, ka, vt = _mla_qkv(cq, ckv, kr, krr, tabs, q_norm_g, kv_norm_g, wq, wkn, place, wvt)
    att = _mla_attn(qa, ka, vt)
    h_f, h_b = _mlstm(mq, mk, mv, mif, if_bias)
    return _odd_out(att, mla_gate, h_f, h_b, mo, ml_gate, ml_norm_g, w_out, x, final_g, final)


def _trunk(x, norm_g, final_norm_g, e_w_in, e_gla_a_up, e_gla_a_bias, e_gla_norm_g, e_pool_w,
           e_pool_scale, e_w_out, o_w_in, o_q_norm_g, o_q_up, o_kv_norm_g, o_kv_up, o_if_bias,
           o_mlstm_norm_g, o_w_out):
    depth = norm_g.shape[0]
    tabs = _rope_tables(x.shape[1])
    for layer in range(depth):
        i = layer // 2
        if layer % 2 == 0:
            x = _even_layer(x, norm_g[layer], e_w_in[i], e_gla_a_up[i], e_gla_a_bias[i],
                            e_gla_norm_g[i], e_pool_w[i], e_pool_scale[i], e_w_out[i])
        else:
            x = _odd_layer(x, norm_g[layer], tabs, o_w_in[i], o_q_norm_g[i], o_q_up[i],
                           o_kv_norm_g[i], o_kv_up[i], o_if_bias[i], o_mlstm_norm_g[i], o_w_out[i],
                           final_norm_g, layer == depth - 1)
    return x


def kernel(x_prompt, x_sample, norm_g, final_norm_g, e_w_in, e_gla_a_up, e_gla_a_bias, e_gla_norm_g,
           e_pool_w, e_pool_scale, e_w_out, o_w_in, o_q_norm_g, o_q_up, o_kv_norm_g, o_kv_up,
           o_if_bias, o_mlstm_norm_g, o_w_out):
    params = (norm_g, final_norm_g, e_w_in, e_gla_a_up, e_gla_a_bias, e_gla_norm_g, e_pool_w,
              e_pool_scale, e_w_out, o_w_in, o_q_norm_g, o_q_up, o_kv_norm_g, o_kv_up, o_if_bias,
              o_mlstm_norm_g, o_w_out)
    return (_trunk(x_prompt, *params), _trunk(x_sample, *params))
```

```python
import functools
import math

import jax
import jax.numpy as jnp
from jax import lax
from jax.experimental import pallas as pl
from jax.experimental.pallas import tpu as pltpu

F32 = jnp.float32
BF16 = jnp.bfloat16

D_MODEL = 1024
NORM_EPS = 1e-6
CHUNK = 64

GLA_HEADS = 4
GLA_DK = 128
GLA_DV = 256
GLA_LR = 16
GLA_GATE_NORM = 16.0
GLA_K_TOT = GLA_HEADS * GLA_DK
GLA_V_TOT = GLA_HEADS * GLA_DV

POOL_GROUPS = 4
POOL_WINDOWS = (2, 4, 8, 16)
POOL_DG = 128
POOL_W = POOL_GROUPS * POOL_DG
POOL_HALO = 8

MLA_HEADS = 8
MLA_NOPE = 64
MLA_ROPE = 32
MLA_DV = 64
MLA_Q_LORA = 384
MLA_KV_LORA = 256
MLA_W = MLA_HEADS * MLA_DV
MLA_HEAD_PAD = 128
MLA_VT_ROWS = 80
ROPE_THETA = 10000.0

ML_HEADS = 4
ML_DH = 128
ML_W = ML_HEADS * ML_DH
ML_CHUNK = 128

VMEM_LIMIT_BYTES = 56 * 1024 * 1024

TOKEN_TILE = 512
SEQ_TILE = 512
ATT_TQ = 1024
ATT_TQ_SUB = 512
ATT_TK = 512
ATT_UNROLL = 4


def _cparams(sem):
    return pltpu.CompilerParams(dimension_semantics=sem, vmem_limit_bytes=VMEM_LIMIT_BYTES)


def _const_spec(shape):
    nd = len(shape)
    return pl.BlockSpec(shape, lambda *_: (0,) * nd)


def _dot(a, b):
    return jnp.dot(a, b, preferred_element_type=F32)


def _dot_nt(a, b):
    return lax.dot_general(a, b, (((1,), (1,)), ((), ())), preferred_element_type=F32)


def _dot_tn(a, b):
    return lax.dot_general(a, b, (((0,), (0,)), ((), ())), preferred_element_type=F32)


def _sigmoid(x):
    return 1.0 / (1.0 + jnp.exp(-x))


def _silu(x):
    return x * _sigmoid(x)


def _log_sigmoid(x):
    return jnp.minimum(x, 0.0) - jnp.log(1.0 + jnp.exp(-jnp.abs(x)))


def _rms(x, g):
    return x * lax.rsqrt(jnp.mean(x * x, axis=-1, keepdims=True) + NORM_EPS) * g


def _split3(x):
    hi = x.astype(BF16)
    r1 = x - hi.astype(F32)
    mid = r1.astype(BF16)
    lo = (r1 - mid.astype(F32)).astype(BF16)
    return hi, mid, lo


def _tri_left(tri, x):
    hi, mid, lo = _split3(x)
    return _dot(tri, hi) + _dot(tri, mid) + _dot(tri, lo)


def _tri_right(x, tri):
    hi, mid, lo = _split3(x)
    return _dot(hi, tri) + _dot(mid, tri) + _dot(lo, tri)


def _norm_proj_kernel(n_out, x_ref, g_ref, *refs):
    w_refs, o_refs = refs[:n_out], refs[n_out:]
    h = _rms(x_ref[...], g_ref[...]).astype(BF16)
    for w_ref, o_ref in zip(w_refs, o_refs):
        o_ref[...] = _dot(h, w_ref[...]).astype(o_ref.dtype)


def _norm_proj(x2d, g, weights, out_dtypes):
    t = x2d.shape[0]
    tm = TOKEN_TILE
    n_out = len(weights)
    in_specs = [pl.BlockSpec((tm, D_MODEL), lambda i: (i, 0)), _const_spec((1, D_MODEL))]
    in_specs += [_const_spec(w.shape) for w in weights]
    out_specs = [pl.BlockSpec((tm, w.shape[1]), lambda i: (i, 0)) for w in weights]
    out_shape = [jax.ShapeDtypeStruct((t, w.shape[1]), dt) for w, dt in zip(weights, out_dtypes)]
    return pl.pallas_call(
        functools.partial(_norm_proj_kernel, n_out),
        grid=(t // tm,),
        in_specs=in_specs,
        out_specs=out_specs,
        out_shape=out_shape,
        compiler_params=_cparams(("parallel",)),
        name="norm_proj",
    )(x2d, g, *weights)


def _gla_chunk(q_ref, k_ref, v_ref, o_ref, st_ref, st_idx, row, b_all, inclusive):
    ii = lax.broadcasted_iota(jnp.int32, (CHUNK, CHUNK), 0)
    jj = lax.broadcasted_iota(jnp.int32, (CHUNK, CHUNK), 1)
    mask = (jj <= ii) if inclusive else (jj > ii)
    edge = CHUNK - 1 if inclusive else 0
    for h in range(GLA_HEADS):
        ks = slice(h * GLA_DK, (h + 1) * GLA_DK)
        vs = slice(h * GLA_DV, (h + 1) * GLA_DV)
        q = q_ref[0, pl.ds(row, CHUNK), ks] * (GLA_DK ** -0.5)
        k = k_ref[0, pl.ds(row, CHUNK), ks]
        v = v_ref[0, pl.ds(row, CHUNK), vs]
        b = b_all[:, ks]
        b_edge = b[edge:edge + 1, :]
        qe = (q * jnp.exp(b)).astype(BF16)
        kd = (k * jnp.exp(-b)).astype(BF16)
        a = jnp.where(mask, _dot_nt(qe, kd), 0.0).astype(BF16)
        st = st_ref[st_idx + h]
        o = _dot(a, v) + _dot_nt(qe, st.astype(BF16))
        o_ref[0, pl.ds(row, CHUNK), vs] = o
        kdec = (k * jnp.exp(b_edge - b)).astype(BF16)
        st_ref[st_idx + h] = st * jnp.exp(b_edge) + _dot_tn(v, kdec)


def _gla_kernel(qf, kf, vf, lrf, qb, kb, vb, lrb, aupf, aupb, biasf, biasb, tril, triu,
                of, ob, st_ref, la_ref):
    c = pl.program_id(1)

    @pl.when(c == 0)
    def _():
        st_ref[...] = jnp.zeros_like(st_ref)

    inv = 1.0 / GLA_GATE_NORM
    la_ref[0] = _log_sigmoid(_dot(lrf[0], aupf[...]) + biasf[...]) * inv
    la_ref[1] = _log_sigmoid(_dot(lrb[0], aupb[...]) + biasb[...]) * inv
    n_chunks = SEQ_TILE // CHUNK

    def body(j, carry):
        rf = pl.multiple_of(j * CHUNK, CHUNK)
        rb = pl.multiple_of((n_chunks - 1 - j) * CHUNK, CHUNK)
        b_f = _tri_left(tril[...], la_ref[0, pl.ds(rf, CHUNK), :])
        b_b = _tri_left(triu[...], la_ref[1, pl.ds(rb, CHUNK), :])
        _gla_chunk(qf, kf, vf, of, st_ref, 0, rf, b_f, True)
        _gla_chunk(qb, kb, vb, ob, st_ref, GLA_HEADS, rb, b_b, False)
        return carry

    lax.fori_loop(0, n_chunks, body, 0)


def _gla(q, k, v, lr_f, lr_b, a_up, a_bias):
    bsz, s, _ = q.shape
    ts = SEQ_TILE
    ns = s // ts
    fwd = lambda b, c: (b, c, 0)
    bwd = lambda b, c: (b, ns - 1 - c, 0)
    idx = jnp.arange(CHUNK)
    tril = (idx[None, :] <= idx[:, None]).astype(BF16)
    triu = (idx[None, :] >= idx[:, None]).astype(BF16)

    def seq_specs(imap):
        return [pl.BlockSpec((1, ts, GLA_K_TOT), imap), pl.BlockSpec((1, ts, GLA_K_TOT), imap),
                pl.BlockSpec((1, ts, GLA_V_TOT), imap), pl.BlockSpec((1, ts, GLA_LR), imap)]

    in_specs = seq_specs(fwd) + seq_specs(bwd) + [
        _const_spec((GLA_LR, GLA_K_TOT)), _const_spec((GLA_LR, GLA_K_TOT)),
        _const_spec((1, GLA_K_TOT)), _const_spec((1, GLA_K_TOT)),
        _const_spec((CHUNK, CHUNK)), _const_spec((CHUNK, CHUNK))]
    out_specs = [pl.BlockSpec((1, ts, GLA_V_TOT), fwd), pl.BlockSpec((1, ts, GLA_V_TOT), bwd)]
    out_shape = [jax.ShapeDtypeStruct((bsz, s, GLA_V_TOT), F32)] * 2
    return pl.pallas_call(
        _gla_kernel,
        grid=(bsz, ns),
        in_specs=in_specs,
        out_specs=out_specs,
        out_shape=out_shape,
        scratch_shapes=[pltpu.VMEM((2 * GLA_HEADS, GLA_DV, GLA_DK), F32),
                        pltpu.VMEM((2, ts, GLA_K_TOT), F32)],
        compiler_params=_cparams(("parallel", "arbitrary")),
        name="gla_scan",
    )(q, k, v, lr_f, q, k, v, lr_b,
      a_up[0].astype(BF16), a_up[1].astype(BF16), a_bias[0:1], a_bias[1:2], tril, triu)


def _even_out_kernel(seq_len, of, ob, gate, ng, pu, pprev, pnext, pgate, pw, pscale, wa, wb, x, o_ref):
    tm = of.shape[1]
    i = pl.program_id(1)
    n_i = pl.num_programs(1)

    o = of[0] + ob[0]
    g_all = gate[0]
    parts = []
    for h in range(GLA_HEADS):
        vs = slice(h * GLA_DV, (h + 1) * GLA_DV)
        parts.append((_rms(o[:, vs], ng[...]) * _silu(g_all[:, vs])).astype(BF16))
    gla_out = jnp.concatenate(parts, axis=-1)

    u = pu[0]
    prev = jnp.where(i > 0, pprev[0], 0.0)
    nxt = jnp.where(i < n_i - 1, pnext[0], 0.0)
    ext = jnp.concatenate([prev, u, nxt], axis=0)
    n_ext = tm + 2 * POOL_HALO
    pos = i * tm + lax.broadcasted_iota(jnp.int32, (tm, 1), 0)
    mixed = []
    for gi, w in enumerate(POOL_WINDOWS):
        cs = slice(gi * POOL_DG, (gi + 1) * POOL_DG)
        a = ext[:, cs]
        span = 1
        while span < w:
            a = a + pltpu.roll(a, span, 0)
            span *= 2
        shift = w // 2 - 1
        if shift:
            a = pltpu.roll(a, n_ext - shift, 0)
        win = a[POOL_HALO:POOL_HALO + tm]
        lo = jnp.maximum(pos - w // 2, 0)
        hi = jnp.minimum(pos + w // 2, seq_len)
        pooled = win / (hi - lo).astype(F32) - u[:, cs]
        mixed.append(_dot(pooled.astype(BF16), pw[gi]))
    pool_out = (jnp.concatenate(mixed, axis=-1) * pscale[...] * _silu(pgate[0])).astype(BF16)

    y = _dot(gla_out, wa[...]) + _dot(pool_out, wb[...])
    o_ref[0] = x[0] + y


def _even_out(o_f, o_b, gate, norm_g, pool_u, pool_gate, pool_w, pool_scale, w_out, x):
    bsz, s, _ = x.shape
    tm = TOKEN_TILE
    nh = tm // POOL_HALO
    n_halo = s // POOL_HALO
    cur = lambda b, i: (b, i, 0)
    in_specs = [
        pl.BlockSpec((1, tm, GLA_V_TOT), cur), pl.BlockSpec((1, tm, GLA_V_TOT), cur),
        pl.BlockSpec((1, tm, GLA_V_TOT), cur), _const_spec((1, GLA_DV)),
        pl.BlockSpec((1, tm, POOL_W), cur),
        pl.BlockSpec((1, POOL_HALO, POOL_W), lambda b, i: (b, jnp.maximum(i * nh - 1, 0), 0)),
        pl.BlockSpec((1, POOL_HALO, POOL_W), lambda b, i: (b, jnp.minimum((i + 1) * nh, n_halo - 1), 0)),
        pl.BlockSpec((1, tm, POOL_W), cur),
        _const_spec((POOL_GROUPS, POOL_DG, POOL_DG)), _const_spec((1, POOL_W)),
        _const_spec((GLA_V_TOT, D_MODEL)), _const_spec((POOL_W, D_MODEL)),
        pl.BlockSpec((1, tm, D_MODEL), cur)]
    return pl.pallas_call(
        functools.partial(_even_out_kernel, s),
        grid=(bsz, s // tm),
        in_specs=in_specs,
        out_specs=pl.BlockSpec((1, tm, D_MODEL), cur),
        out_shape=jax.ShapeDtypeStruct((bsz, s, D_MODEL), F32),
        compiler_params=_cparams(("parallel", "parallel")),
        name="even_out",
    )(o_f, o_b, gate, norm_g[None, :], pool_u, pool_u, pool_u, pool_gate,
      pool_w.astype(BF16), pool_scale[None, :],
      w_out[:GLA_V_TOT].astype(BF16), w_out[GLA_V_TOT:].astype(BF16), x)


def _mla_qkv_kernel(cq, ckv, kr, krr, cos_h, sin_h, cos_r, sin_r, qg, kvg, wq, wkn, place, wvt,
                    q_ref, k_ref, vt_ref):
    qk_scale = math.log2(math.e) * (MLA_NOPE + MLA_ROPE) ** -0.5
    w_all = MLA_HEADS * MLA_HEAD_PAD
    nq = _rms(cq[0], qg[...]).astype(BF16)
    qq = _dot(nq, wq[...])
    cos2 = jnp.concatenate([cos_h[...], cos_h[...]], axis=-1)
    sin2 = jnp.concatenate([sin_h[...], sin_h[...]], axis=-1)
    for p in range(MLA_HEADS // 2):
        sl = slice(2 * p * MLA_HEAD_PAD, (2 * p + 2) * MLA_HEAD_PAD)
        sr = slice(w_all + 2 * p * MLA_HEAD_PAD, w_all + (2 * p + 2) * MLA_HEAD_PAD)
        q_ref[0, :, sl] = ((qq[:, sl] * cos2 + qq[:, sr] * sin2) * qk_scale).astype(BF16)

    nkv = _rms(ckv[0], kvg[...]).astype(BF16)
    k_rope = (kr[0] * cos_r[...] + krr[0] * sin_r[...]).astype(BF16)
    k_ref[0] = (_dot(nkv, wkn[...]) + _dot(k_rope, place[...])).astype(BF16)

    vt = _dot_nt(wvt[...], nkv)
    ones_row = lax.broadcasted_iota(jnp.int32, (MLA_VT_ROWS, 1), 0) == MLA_DV
    for h in range(MLA_HEADS):
        vh = vt[h * MLA_VT_ROWS:(h + 1) * MLA_VT_ROWS]
        vt_ref[0, h, 0] = jnp.where(ones_row, 1.0, vh).astype(BF16)


def _mla_qkv(cq, ckv, kr, krr, tabs, q_norm_g, kv_norm_g, wq, wkn, place, wvt):
    bsz, s, _ = cq.shape
    tm = ATT_TK
    w_all = MLA_HEADS * MLA_HEAD_PAD
    cur = lambda b, i: (b, i, 0)
    tab = lambda b, i: (i, 0)
    cos_h, sin_h, cos_r, sin_r = tabs
    in_specs = [
        pl.BlockSpec((1, tm, MLA_Q_LORA), cur), pl.BlockSpec((1, tm, MLA_KV_LORA), cur),
        pl.BlockSpec((1, tm, MLA_ROPE), cur), pl.BlockSpec((1, tm, MLA_ROPE), cur),
        pl.BlockSpec((tm, MLA_HEAD_PAD), tab), pl.BlockSpec((tm, MLA_HEAD_PAD), tab),
        pl.BlockSpec((tm, MLA_ROPE), tab), pl.BlockSpec((tm, MLA_ROPE), tab),
        _const_spec((1, MLA_Q_LORA)), _const_spec((1, MLA_KV_LORA)),
        _const_spec(wq.shape), _const_spec(wkn.shape), _const_spec(place.shape), _const_spec(wvt.shape)]
    out_specs = [
        pl.BlockSpec((1, tm, w_all), cur), pl.BlockSpec((1, tm, w_all), cur),
        pl.BlockSpec((1, MLA_HEADS, 1, MLA_VT_ROWS, tm), lambda b, i: (b, 0, i, 0, 0))]
    out_shape = [
        jax.ShapeDtypeStruct((bsz, s, w_all), BF16), jax.ShapeDtypeStruct((bsz, s, w_all), BF16),
        jax.ShapeDtypeStruct((bsz, MLA_HEADS, s // tm, MLA_VT_ROWS, tm), BF16)]
    return pl.pallas_call(
        _mla_qkv_kernel,
        grid=(bsz, s // tm),
        in_specs=in_specs,
        out_specs=out_specs,
        out_shape=out_shape,
        compiler_params=_cparams(("parallel", "parallel")),
        name="mla_qkv",
    )(cq, ckv, kr, krr, cos_h, sin_h, cos_r, sin_r, q_norm_g[None, :], kv_norm_g[None, :],
      wq, wkn, place, wvt)


def _mla_weights(q_up, kv_up):
    dq = MLA_NOPE + MLA_ROPE
    half = MLA_ROPE // 2
    qh = q_up.reshape(MLA_Q_LORA, MLA_HEADS, dq)
    zeros = jnp.zeros((MLA_Q_LORA, MLA_HEADS, MLA_HEAD_PAD - dq), F32)
    main = jnp.concatenate([qh, zeros], axis=-1)
    x1 = qh[..., MLA_NOPE:MLA_NOPE + half]
    x2 = qh[..., MLA_NOPE + half:]
    rot = jnp.concatenate([jnp.zeros((MLA_Q_LORA, MLA_HEADS, MLA_NOPE), F32), -x2, x1, zeros], axis=-1)
    w_all = MLA_HEADS * MLA_HEAD_PAD
    wq = jnp.concatenate([main.reshape(MLA_Q_LORA, w_all), rot.reshape(MLA_Q_LORA, w_all)], axis=-1)

    kvh = kv_up.reshape(MLA_KV_LORA, MLA_HEADS, MLA_NOPE + MLA_DV)
    wkn = jnp.concatenate(
        [kvh[..., :MLA_NOPE], jnp.zeros((MLA_KV_LORA, MLA_HEADS, MLA_HEAD_PAD - MLA_NOPE), F32)],
        axis=-1).reshape(MLA_KV_LORA, w_all)
    wv = jnp.transpose(kvh[..., MLA_NOPE:], (1, 2, 0))
    wvt = jnp.concatenate(
        [wv, jnp.zeros((MLA_HEADS, MLA_VT_ROWS - MLA_DV, MLA_KV_LORA), F32)],
        axis=1).reshape(MLA_HEADS * MLA_VT_ROWS, MLA_KV_LORA)
    r = jnp.arange(MLA_ROPE)
    cols = jnp.arange(w_all)
    place = ((cols[None, :] % MLA_HEAD_PAD) == (MLA_NOPE + r[:, None])).astype(BF16)
    return wq.astype(BF16), wkn.astype(BF16), place, wvt.astype(BF16)


def _rope_tables(s):
    inv = ROPE_THETA ** (-jnp.arange(0, MLA_ROPE, 2, dtype=F32) / MLA_ROPE)
    ang = jnp.arange(s, dtype=F32)[:, None] * inv[None, :]
    cos, sin = jnp.cos(ang), jnp.sin(ang)
    pad = jnp.zeros((s, MLA_HEAD_PAD - MLA_NOPE - MLA_ROPE), F32)
    cos_h = jnp.concatenate([jnp.ones((s, MLA_NOPE), F32), cos, cos, pad], axis=-1)
    sin_h = jnp.concatenate([jnp.zeros((s, MLA_NOPE), F32), sin, sin, pad], axis=-1)
    cos_r = jnp.concatenate([cos, cos], axis=-1)
    sin_r = jnp.concatenate([sin, sin], axis=-1)
    return cos_h, sin_h, cos_r, sin_r


def _mla_attn_kernel(q_ref, k_ref, vt_ref, o_ref, s_ref):
    n_k = k_ref.shape[1] // ATT_TK
    for qs in range(ATT_TQ // ATT_TQ_SUB):
        rows = slice(qs * ATT_TQ_SUB, (qs + 1) * ATT_TQ_SUB)
        qts = (q_ref[0, rows, 0:MLA_HEAD_PAD], q_ref[0, rows, MLA_HEAD_PAD:2 * MLA_HEAD_PAD])

        def produce(kk, buf):
            r = pl.multiple_of(kk * ATT_TK, ATT_TK)
            tile_max = []
            for hh in range(2):
                kt = k_ref[0, pl.ds(r, ATT_TK), hh * MLA_HEAD_PAD:(hh + 1) * MLA_HEAD_PAD]
                st = _dot_nt(kt, qts[hh])
                s_ref[buf, hh] = st
                tile_max.append(jnp.max(st, axis=0, keepdims=True))
            return tuple(tile_max)

        def consume(kk, buf, tile_max, carry):
            out = []
            for hh in range(2):
                m, acc = carry[2 * hh], carry[2 * hh + 1]
                m_new = jnp.maximum(m, tile_max[hh])
                alpha = jnp.exp2(m - m_new)
                p = jnp.exp2(s_ref[buf, hh] - m_new).astype(BF16)
                acc = acc * alpha + _dot(vt_ref[0, hh, kk], p)
                out += [m_new, acc]
            return tuple(out)

        def run(k0, state, produce_next):
            carry, tmax = state[:4], state[4:]
            for t in range(ATT_UNROLL):
                last = t == ATT_UNROLL - 1
                nxt = produce(k0 + t + 1, (t + 1) % 2) if (produce_next or not last) else ()
                carry = consume(k0 + t, t % 2, tmax, carry)
                tmax = nxt
            return carry + tmax

        m0 = jnp.full((1, ATT_TQ_SUB), -jnp.inf, F32)
        a0 = jnp.zeros((MLA_VT_ROWS, ATT_TQ_SUB), F32)
        state = lax.fori_loop(0, n_k // ATT_UNROLL - 1, lambda j, st: run(ATT_UNROLL * j, st, True),
                              (m0, a0, m0, a0) + produce(0, 0))
        carry = run(n_k - ATT_UNROLL, state, False)
        outs = [carry[2 * hh + 1][:MLA_DV] / carry[2 * hh + 1][MLA_DV:MLA_DV + 1] for hh in range(2)]
        o_ref[0, rows, :] = jnp.concatenate(outs, axis=0).T


def _mla_attn(q, k, vt):
    bsz, s, _ = q.shape
    n_kt = s // ATT_TK
    return pl.pallas_call(
        _mla_attn_kernel,
        grid=(bsz, MLA_HEADS // 2, s // ATT_TQ),
        in_specs=[
            pl.BlockSpec((1, ATT_TQ, 2 * MLA_HEAD_PAD), lambda b, h, i: (b, i, h)),
            pl.BlockSpec((1, s, 2 * MLA_HEAD_PAD), lambda b, h, i: (b, 0, h)),
            pl.BlockSpec((1, 2, n_kt, MLA_VT_ROWS, ATT_TK), lambda b, h, i: (b, h, 0, 0, 0))],
        out_specs=pl.BlockSpec((1, ATT_TQ, 2 * MLA_DV), lambda b, h, i: (b, i, h)),
        out_shape=jax.ShapeDtypeStruct((bsz, s, MLA_W), F32),
        scratch_shapes=[pltpu.VMEM((2, 2, ATT_TK, ATT_TQ_SUB), F32)],
        compiler_params=_cparams(("parallel", "parallel", "arbitrary")),
        name="mla_attn",
    )(q, k, vt)


def _lane_scan(x, op, identity, reverse):
    lane = lax.broadcasted_iota(jnp.int32, x.shape, 1)
    step = 1
    while step < ML_CHUNK:
        if reverse:
            shifted = jnp.where(lane < ML_CHUNK - step, pltpu.roll(x, ML_CHUNK - step, 1), identity)
        else:
            shifted = jnp.where(lane >= step, pltpu.roll(x, step, 1), identity)
        x = op(x, shifted)
        step *= 2
    return x


def _mlstm_prologue(grf, grb, bias_r, eye, tri_pre, tri_suf, b_ref, cmax_ref, ctot_ref, gtot_ref, ccol_ref):
    n_chunks = SEQ_TILE // ML_CHUNK
    n_dir = 2 * ML_HEADS
    rows16 = lax.broadcasted_iota(jnp.int32, (1, 2 * n_dir, 1), 1)
    g = jnp.where((rows16 % n_dir) < ML_HEADS, grf[0], grb[0]) + bias_r[...][None]
    li = g[:, :n_dir].reshape(n_chunks * n_dir, ML_CHUNK)
    lf = _log_sigmoid(g[:, n_dir:]).reshape(n_chunks * n_dir, ML_CHUNK)
    is_fwd = (lax.broadcasted_iota(jnp.int32, li.shape, 0) % n_dir) < ML_HEADS
    lane = lax.broadcasted_iota(jnp.int32, li.shape, 1)
    b = jnp.where(is_fwd, _tri_right(lf, tri_pre), _tri_right(lf, tri_suf))
    c = li - b
    c_next = jnp.where(lane < ML_CHUNK - 1, pltpu.roll(c, ML_CHUNK - 1, 1), -jnp.inf)
    cmax = jnp.where(is_fwd, _lane_scan(c, jnp.maximum, -jnp.inf, False),
                     _lane_scan(c_next, jnp.maximum, -jnp.inf, True))
    shape3 = (n_chunks, n_dir, ML_CHUNK)
    b_ref[...] = b.reshape(shape3)
    cmax_ref[...] = cmax.reshape(shape3)
    ctot_ref[...] = jnp.broadcast_to(jnp.max(c, axis=-1, keepdims=True), c.shape).reshape(shape3)
    gtot_ref[...] = jnp.broadcast_to(jnp.sum(lf, axis=-1, keepdims=True), c.shape).reshape(shape3)
    hi, mid, lo = _split3(c)
    for jc in range(n_chunks):
        rs = slice(jc * n_dir, (jc + 1) * n_dir)
        ccol_ref[jc] = _dot_nt(eye, hi[rs]) + _dot_nt(eye, mid[rs]) + _dot_nt(eye, lo[rs])


def _mlstm_chunk(q_ref, k_ref, v_ref, o_ref, cnt_ref, m_ref, n, row, jc, stats, inclusive):
    b_ref, cmax_ref, ctot_ref, gtot_ref, ccol_ref = stats
    h = n % ML_HEADS
    hs = slice(h * ML_DH, (h + 1) * ML_DH)
    jj = lax.broadcasted_iota(jnp.int32, (ML_CHUNK, ML_CHUNK), 0)
    ii = lax.broadcasted_iota(jnp.int32, (ML_CHUNK, ML_CHUNK), 1)
    mask = (jj <= ii) if inclusive else (jj > ii)

    m = m_ref[n]
    b_r = b_ref[jc, n:n + 1, :]
    big_m = jnp.maximum(m, cmax_ref[jc, n:n + 1, :])
    m_tot = jnp.maximum(m, ctot_ref[jc, n:n + 1, 0:1])
    g_tot = gtot_ref[jc, n:n + 1, 0:1]
    c_b = jnp.broadcast_to(ccol_ref[jc, :, n:n + 1], (ML_CHUNK, ML_CHUNK))

    q = q_ref[0, pl.ds(row, ML_CHUNK), hs]
    ks = k_ref[0, pl.ds(row, ML_CHUNK), hs] * (ML_DH ** -0.5)
    v = v_ref[0, pl.ds(row, ML_CHUNK), hs]
    ones_row = (lax.broadcasted_iota(jnp.int32, (ML_DH, ML_CHUNK), 0) == 0).astype(BF16)
    vt_ext = jnp.concatenate([v.T, ones_row], axis=0)

    cnt = cnt_ref[n]
    lhs = jnp.concatenate([ks.astype(BF16), cnt.astype(BF16)], axis=0)
    sr = _dot_nt(lhs, q)
    pt = jnp.exp(jnp.where(mask, c_b - big_m, -jnp.inf))
    st = (sr[:ML_CHUNK] * pt).astype(BF16)
    kw = (ks * jnp.exp(c_b - m_tot)).astype(BF16)
    ho = _dot(vt_ext, jnp.concatenate([st, kw], axis=-1))
    ht = ho[:, :ML_CHUNK] + jnp.exp(m - big_m) * sr[ML_CHUNK:]
    den = ht[ML_DH:ML_DH + 1]
    scale = 1.0 / jnp.maximum(jnp.abs(den), jnp.exp(-(b_r + big_m)))
    o_ref[0, pl.ds(row, ML_CHUNK), hs] = (ht[:ML_DH] * scale).T
    cnt_ref[n] = jnp.exp(m - m_tot) * cnt + ho[:, ML_CHUNK:]
    m_ref[n] = g_tot + m_tot


def _mlstm_kernel(qf, kf, vf, grf, qb, kb, vb, grb, bias_r, eye_ref, tri_pre, tri_suf, of, ob,
                  cnt_ref, m_ref, b_ref, cmax_ref, ctot_ref, gtot_ref, ccol_ref):
    c = pl.program_id(1)

    @pl.when(c == 0)
    def _():
        cnt_ref[...] = jnp.zeros_like(cnt_ref)
        m_ref[...] = jnp.zeros_like(m_ref)

    stats = (b_ref, cmax_ref, ctot_ref, gtot_ref, ccol_ref)
    _mlstm_prologue(grf, grb, bias_r, eye_ref[...], tri_pre[...], tri_suf[...], *stats)
    n_chunks = SEQ_TILE // ML_CHUNK

    def body(j, carry):
        jb = n_chunks - 1 - j
        rf = pl.multiple_of(j * ML_CHUNK, ML_CHUNK)
        rb = pl.multiple_of(jb * ML_CHUNK, ML_CHUNK)
        for h in range(ML_HEADS):
            _mlstm_chunk(qf, kf, vf, of, cnt_ref, m_ref, h, rf, j, stats, True)
            _mlstm_chunk(qb, kb, vb, ob, cnt_ref, m_ref, ML_HEADS + h, rb, jb, stats, False)
        return carry

    lax.fori_loop(0, n_chunks, body, 0)


def _mlstm(q, k, v, gates, if_bias):
    bsz, s, _ = q.shape
    ts = SEQ_TILE
    ns = s // ts
    n_chunks = ts // ML_CHUNK
    n_gate = 4 * ML_HEADS
    n_dir = 2 * ML_HEADS
    gates_r = gates.reshape(bsz, s // ML_CHUNK, ML_CHUNK, n_gate).transpose(0, 1, 3, 2)
    fwd = lambda b, c: (b, c, 0)
    bwd = lambda b, c: (b, ns - 1 - c, 0)
    fwd4 = lambda b, c: (b, c, 0, 0)
    bwd4 = lambda b, c: (b, ns - 1 - c, 0, 0)

    def seq_specs(imap, imap4):
        return [pl.BlockSpec((1, ts, ML_W), imap), pl.BlockSpec((1, ts, ML_W), imap),
                pl.BlockSpec((1, ts, ML_W), imap),
                pl.BlockSpec((1, n_chunks, n_gate, ML_CHUNK), imap4)]

    in_specs = seq_specs(fwd, fwd4) + seq_specs(bwd, bwd4) + [
        _const_spec((n_gate, ML_CHUNK))] + [_const_spec((ML_CHUNK, ML_CHUNK))] * 3
    idx = jnp.arange(ML_CHUNK)
    tri_pre = (idx[:, None] <= idx[None, :]).astype(BF16)
    tri_suf = (idx[:, None] >= idx[None, :]).astype(BF16)
    out_specs = [pl.BlockSpec((1, ts, ML_W), fwd), pl.BlockSpec((1, ts, ML_W), bwd)]
    out_shape = [jax.ShapeDtypeStruct((bsz, s, ML_W), F32)] * 2
    stat = pltpu.VMEM((n_chunks, n_dir, ML_CHUNK), F32)
    return pl.pallas_call(
        _mlstm_kernel,
        grid=(bsz, ns),
        in_specs=in_specs,
        out_specs=out_specs,
        out_shape=out_shape,
        scratch_shapes=[pltpu.VMEM((n_dir, 2 * ML_DH, ML_DH), F32), pltpu.VMEM((n_dir, 1, 1), F32),
                        stat, stat, stat, stat, pltpu.VMEM((n_chunks, ML_CHUNK, n_dir), F32)],
        compiler_params=_cparams(("parallel", "arbitrary")),
        name="mlstm_scan",
    )(q, k, v, gates_r, q, k, v, gates_r,
      jnp.broadcast_to(if_bias[:, None], (n_gate, ML_CHUNK)), jnp.eye(ML_CHUNK, dtype=BF16),
      tri_pre, tri_suf)


def _odd_out_kernel(final, att, mgate, hf, hb, mo, lgate, ng, wa, wb, x, fg, o_ref):
    mla_out = (att[0] * _silu(mgate[0])).astype(BF16)
    hm = hf[0] + hb[0]
    mo_v, lg_v = mo[0], lgate[0]
    parts = []
    for h in range(ML_HEADS):
        hs = slice(h * ML_DH, (h + 1) * ML_DH)
        y = _rms(hm[:, hs], ng[...]) * _sigmoid(mo_v[:, hs])
        parts.append((y * _silu(lg_v[:, hs])).astype(BF16))
    ml_out = jnp.concatenate(parts, axis=-1)
    xn = x[0] + (_dot(mla_out, wa[...]) + _dot(ml_out, wb[...]))
    o_ref[0] = _rms(xn, fg[...]) if final else xn


def _odd_out(att, mla_gate, h_f, h_b, mo, ml_gate, norm_g, w_out, x, final_g, final):
    bsz, s, _ = x.shape
    tm = TOKEN_TILE
    cur = lambda b, i: (b, i, 0)
    half = pl.BlockSpec((1, tm, MLA_W), cur)
    in_specs = [half] * 6 + [
        _const_spec((1, ML_DH)), _const_spec((MLA_W, D_MODEL)), _const_spec((ML_W, D_MODEL)),
        pl.BlockSpec((1, tm, D_MODEL), cur), _const_spec((1, D_MODEL))]
    return pl.pallas_call(
        functools.partial(_odd_out_kernel, final),
        grid=(bsz, s // tm),
        in_specs=in_specs,
        out_specs=pl.BlockSpec((1, tm, D_MODEL), cur),
        out_shape=jax.ShapeDtypeStruct((bsz, s, D_MODEL), F32),
        compiler_params=_cparams(("parallel", "parallel")),
        name="odd_out",
    )(att, mla_gate, h_f, h_b, mo, ml_gate, norm_g[None, :],
      w_out[:MLA_W].astype(BF16), w_out[MLA_W:].astype(BF16), x, final_g[None, :])


def _col_split(w, sizes):
    out, off = [], 0
    for n in sizes:
        out.append(w[:, off:off + n].astype(BF16))
        off += n
    return out


def _even_layer(x, g, w_in, a_up, a_bias, gla_norm_g, pool_w, pool_scale, w_out):
    bsz, s, _ = x.shape
    sizes = (GLA_K_TOT, GLA_K_TOT, GLA_V_TOT, GLA_V_TOT, GLA_LR, GLA_LR, POOL_W, POOL_W)
    dts = (F32, F32, BF16, F32, BF16, BF16, F32, F32)
    outs = _norm_proj(x.reshape(bsz * s, D_MODEL), g[None, :], _col_split(w_in, sizes), dts)
    q, k, v, gate, lr_f, lr_b, pool_u, pool_gate = [o.reshape(bsz, s, -1) for o in outs]
    o_f, o_b = _gla(q, k, v, lr_f, lr_b, a_up, a_bias)
    return _even_out(o_f, o_b, gate, gla_norm_g, pool_u, pool_gate, pool_w, pool_scale, w_out, x)


def _odd_layer(x, g, tabs, w_in, q_norm_g, q_up, kv_norm_g, kv_up, if_bias, ml_norm_g, w_out,
               final_g, final):
    bsz, s, _ = x.shape
    half = MLA_ROPE // 2
    off = MLA_Q_LORA + MLA_KV_LORA
    kr_w = w_in[:, off:off + MLA_ROPE]
    w_ext = jnp.concatenate([w_in, -kr_w[:, half:], kr_w[:, :half]], axis=-1)
    sizes = (MLA_Q_LORA, MLA_KV_LORA, MLA_ROPE, MLA_W, ML_W, ML_W, ML_W, ML_W, 4 * ML_HEADS, ML_W,
             MLA_ROPE)
    dts = (F32, F32, F32, F32, BF16, F32, BF16, F32, F32, F32, F32)
    outs = _norm_proj(x.reshape(bsz * s, D_MODEL), g[None, :], _col_split(w_ext, sizes), dts)
    cq, ckv, kr, mla_gate, mq, mk, mv, mo, mif, ml_gate, krr = [o.reshape(bsz, s, -1) for o in outs]
    wq, wkn, place, wvt = _mla_weights(q_up, kv_up)
    qa, ka, vt = _mla_qkv(cq, ckv, kr, krr, tabs, q_norm_g, kv_norm_g, wq, wkn, place, wvt)
    att = _mla_attn(qa, ka, vt)
    h_f, h_b = _mlstm(mq, mk, mv, mif, if_bias)
    return _odd_out(att, mla_gate, h_f, h_b, mo, ml_gate, ml_norm_g, w_out, x, final_g, final)


def _trunk(x, norm_g, final_norm_g, e_w_in, e_gla_a_up, e_gla_a_bias, e_gla_norm_g, e_pool_w,
           e_pool_scale, e_w_out, o_w_in, o_q_norm_g, o_q_up, o_kv_norm_g, o_kv_up, o_if_bias,
           o_mlstm_norm_g, o_w_out):
    depth = norm_g.shape[0]
    tabs = _rope_tables(x.shape[1])
    for layer in range(depth):
        i = layer // 2
        if layer % 2 == 0:
            x = _even_layer(x, norm_g[layer], e_w_in[i], e_gla_a_up[i], e_gla_a_bias[i],
                            e_gla_norm_g[i], e_pool_w[i], e_pool_scale[i], e_w_out[i])
        else:
            x = _odd_layer(x, norm_g[layer], tabs, o_w_in[i], o_q_norm_g[i], o_q_up[i],
                           o_kv_norm_g[i], o_kv_up[i], o_if_bias[i], o_mlstm_norm_g[i], o_w_out[i],
                           final_norm_g, layer == depth - 1)
    return x


def kernel(x_prompt, x_sample, norm_g, final_norm_g, e_w_in, e_gla_a_up, e_gla_a_bias, e_gla_norm_g,
           e_pool_w, e_pool_scale, e_w_out, o_w_in, o_q_norm_g, o_q_up, o_kv_norm_g, o_kv_up,
           o_if_bias, o_mlstm_norm_g, o_w_out):
    params = (norm_g, final_norm_g, e_w_in, e_gla_a_up, e_gla_a_bias, e_gla_norm_g, e_pool_w,
              e_pool_scale, e_w_out, o_w_in, o_q_norm_g, o_q_up, o_kv_norm_g, o_kv_up, o_if_bias,
              o_mlstm_norm_g, o_w_out)
    return (_trunk(x_prompt, *params), _trunk(x_sample, *params))
```

```python
import functools
import math

import jax
import jax.numpy as jnp
from jax import lax
from jax.experimental import pallas as pl
from jax.experimental.pallas import tpu as pltpu

F32 = jnp.float32
BF16 = jnp.bfloat16

D_MODEL = 1024
NORM_EPS = 1e-6
CHUNK = 64

GLA_HEADS = 4
GLA_DK = 128
GLA_DV = 256
GLA_LR = 16
GLA_GATE_NORM = 16.0
GLA_K_TOT = GLA_HEADS * GLA_DK
GLA_V_TOT = GLA_HEADS * GLA_DV
GLA_BLOCK = 2 * CHUNK

POOL_GROUPS = 4
POOL_WINDOWS = (2, 4, 8, 16)
POOL_DG = 128
POOL_W = POOL_GROUPS * POOL_DG
POOL_HALO = 8

MLA_HEADS = 8
MLA_NOPE = 64
MLA_ROPE = 32
MLA_DV = 64
MLA_Q_LORA = 384
MLA_KV_LORA = 256
MLA_W = MLA_HEADS * MLA_DV
MLA_HEAD_PAD = 128
MLA_VT_ROWS = 80
ROPE_THETA = 10000.0

ML_HEADS = 4
ML_DH = 128
ML_W = ML_HEADS * ML_DH
ML_CHUNK = 128

VMEM_LIMIT_BYTES = 56 * 1024 * 1024

TOKEN_TILE = 512
SEQ_TILE = 512
ATT_TQ = 1024
ATT_TQ_SUB = 512
ATT_TK = 512
ATT_UNROLL = 4


def _cparams(sem):
    return pltpu.CompilerParams(dimension_semantics=sem, vmem_limit_bytes=VMEM_LIMIT_BYTES)


def _const_spec(shape):
    nd = len(shape)
    return pl.BlockSpec(shape, lambda *_: (0,) * nd)


def _dot(a, b):
    return jnp.dot(a, b, preferred_element_type=F32)


def _dot_nt(a, b):
    return lax.dot_general(a, b, (((1,), (1,)), ((), ())), preferred_element_type=F32)


def _dot_tn(a, b):
    return lax.dot_general(a, b, (((0,), (0,)), ((), ())), preferred_element_type=F32)


def _sigmoid(x):
    return 1.0 / (1.0 + jnp.exp(-x))


def _silu(x):
    return x * _sigmoid(x)


def _log_sigmoid(x):
    return jnp.minimum(x, 0.0) - jnp.log(1.0 + jnp.exp(-jnp.abs(x)))


def _rms(x, g):
    return x * lax.rsqrt(jnp.mean(x * x, axis=-1, keepdims=True) + NORM_EPS) * g


def _split3(x):
    hi = x.astype(BF16)
    r1 = x - hi.astype(F32)
    mid = r1.astype(BF16)
    lo = (r1 - mid.astype(F32)).astype(BF16)
    return hi, mid, lo


def _tri_left(tri, x):
    hi, mid, lo = _split3(x)
    return _dot(tri, hi) + _dot(tri, mid) + _dot(tri, lo)


def _tri_right(x, tri):
    hi, mid, lo = _split3(x)
    return _dot(hi, tri) + _dot(mid, tri) + _dot(lo, tri)


def _norm_proj_kernel(n_out, x_ref, g_ref, *refs):
    w_refs, o_refs = refs[:n_out], refs[n_out:]
    h = _rms(x_ref[...], g_ref[...]).astype(BF16)
    for w_ref, o_ref in zip(w_refs, o_refs):
        o_ref[...] = _dot(h, w_ref[...]).astype(o_ref.dtype)


def _norm_proj(x2d, g, weights, out_dtypes):
    t = x2d.shape[0]
    tm = TOKEN_TILE
    n_out = len(weights)
    in_specs = [pl.BlockSpec((tm, D_MODEL), lambda i: (i, 0)), _const_spec((1, D_MODEL))]
    in_specs += [_const_spec(w.shape) for w in weights]
    out_specs = [pl.BlockSpec((tm, w.shape[1]), lambda i: (i, 0)) for w in weights]
    out_shape = [jax.ShapeDtypeStruct((t, w.shape[1]), dt) for w, dt in zip(weights, out_dtypes)]
    return pl.pallas_call(
        functools.partial(_norm_proj_kernel, n_out),
        grid=(t // tm,),
        in_specs=in_specs,
        out_specs=out_specs,
        out_shape=out_shape,
        compiler_params=_cparams(("parallel",)),
        name="norm_proj",
    )(x2d, g, *weights)


def _gla_block(q_ref, k_ref, v_ref, o_ref, st_ref, st_idx, row, b_all, inclusive):
    ii = lax.broadcasted_iota(jnp.int32, (GLA_BLOCK, GLA_BLOCK), 0)
    jj = lax.broadcasted_iota(jnp.int32, (GLA_BLOCK, GLA_BLOCK), 1)
    rr = lax.broadcasted_iota(jnp.int32, (GLA_BLOCK, 1), 0)
    same = (ii >= CHUNK) == (jj >= CHUNK)
    if inclusive:
        intra, cross, far_rows = same & (jj <= ii), (ii >= CHUNK) & (jj < CHUNK), rr >= CHUNK
        edge_lo, edge_hi = CHUNK - 1, GLA_BLOCK - 1
    else:
        intra, cross, far_rows = same & (jj > ii), (ii < CHUNK) & (jj >= CHUNK), rr < CHUNK
        edge_lo, edge_hi = 0, CHUNK
    for h in range(GLA_HEADS):
        ks = slice(h * GLA_DK, (h + 1) * GLA_DK)
        vs = slice(h * GLA_DV, (h + 1) * GLA_DV)
        q = q_ref[0, pl.ds(row, GLA_BLOCK), ks] * (GLA_DK ** -0.5)
        k = k_ref[0, pl.ds(row, GLA_BLOCK), ks]
        v = v_ref[0, pl.ds(row, GLA_BLOCK), vs]
        b = b_all[:, ks]
        tot_lo, tot_hi = b[edge_lo:edge_lo + 1, :], b[edge_hi:edge_hi + 1, :]
        tot_near, tot_far = (tot_lo, tot_hi) if inclusive else (tot_hi, tot_lo)
        qe = q * jnp.exp(b)
        kd = (k * jnp.exp(-b)).astype(BF16)
        kdec = k * jnp.exp(jnp.where(rr < CHUNK, tot_lo, tot_hi) - b)
        pp = _dot_nt(qe.astype(BF16), jnp.concatenate([kd, kdec.astype(BF16)], axis=0))
        a = jnp.where(intra, pp[:, :GLA_BLOCK], jnp.where(cross, pp[:, GLA_BLOCK:], 0.0)).astype(BF16)
        qe_in = jnp.where(far_rows, qe * jnp.exp(tot_near), qe).astype(BF16)
        kdec_out = jnp.where(far_rows, kdec, kdec * jnp.exp(tot_far)).astype(BF16)
        st = st_ref[st_idx + h]
        o_ref[0, pl.ds(row, GLA_BLOCK), vs] = _dot(a, v) + _dot_nt(qe_in, st.astype(BF16))
        st_ref[st_idx + h] = st * jnp.exp(tot_near + tot_far) + _dot_tn(v, kdec_out)


def _gla_kernel(qf, kf, vf, lrf, qb, kb, vb, lrb, aupf, aupb, biasf, biasb, tril, triu,
                of, ob, st_ref, la_ref):
    c = pl.program_id(1)

    @pl.when(c == 0)
    def _():
        st_ref[...] = jnp.zeros_like(st_ref)

    inv = 1.0 / GLA_GATE_NORM
    la_ref[0] = _log_sigmoid(_dot(lrf[0], aupf[...]) + biasf[...]) * inv
    la_ref[1] = _log_sigmoid(_dot(lrb[0], aupb[...]) + biasb[...]) * inv
    n_blocks = SEQ_TILE // GLA_BLOCK

    def body(j, carry):
        rf = pl.multiple_of(j * GLA_BLOCK, GLA_BLOCK)
        rb = pl.multiple_of((n_blocks - 1 - j) * GLA_BLOCK, GLA_BLOCK)
        b_f = _tri_left(tril[...], la_ref[0, pl.ds(rf, GLA_BLOCK), :])
        b_b = _tri_left(triu[...], la_ref[1, pl.ds(rb, GLA_BLOCK), :])
        _gla_block(qf, kf, vf, of, st_ref, 0, rf, b_f, True)
        _gla_block(qb, kb, vb, ob, st_ref, GLA_HEADS, rb, b_b, False)
        return carry

    lax.fori_loop(0, n_blocks, body, 0)


def _gla(q, k, v, lr_f, lr_b, a_up, a_bias):
    bsz, s, _ = q.shape
    ts = SEQ_TILE
    ns = s // ts
    fwd = lambda b, c: (b, c, 0)
    bwd = lambda b, c: (b, ns - 1 - c, 0)
    idx = jnp.arange(GLA_BLOCK)
    same_chunk = (idx[None, :] // CHUNK) == (idx[:, None] // CHUNK)
    tril = (same_chunk & (idx[None, :] <= idx[:, None])).astype(BF16)
    triu = (same_chunk & (idx[None, :] >= idx[:, None])).astype(BF16)

    def seq_specs(imap):
        return [pl.BlockSpec((1, ts, GLA_K_TOT), imap), pl.BlockSpec((1, ts, GLA_K_TOT), imap),
                pl.BlockSpec((1, ts, GLA_V_TOT), imap), pl.BlockSpec((1, ts, GLA_LR), imap)]

    in_specs = seq_specs(fwd) + seq_specs(bwd) + [
        _const_spec((GLA_LR, GLA_K_TOT)), _const_spec((GLA_LR, GLA_K_TOT)),
        _const_spec((1, GLA_K_TOT)), _const_spec((1, GLA_K_TOT)),
        _const_spec((GLA_BLOCK, GLA_BLOCK)), _const_spec((GLA_BLOCK, GLA_BLOCK))]
    out_specs = [pl.BlockSpec((1, ts, GLA_V_TOT), fwd), pl.BlockSpec((1, ts, GLA_V_TOT), bwd)]
    out_shape = [jax.ShapeDtypeStruct((bsz, s, GLA_V_TOT), F32)] * 2
    return pl.pallas_call(
        _gla_kernel,
        grid=(bsz, ns),
        in_specs=in_specs,
        out_specs=out_specs,
        out_shape=out_shape,
        scratch_shapes=[pltpu.VMEM((2 * GLA_HEADS, GLA_DV, GLA_DK), F32),
                        pltpu.VMEM((2, ts, GLA_K_TOT), F32)],
        compiler_params=_cparams(("parallel", "arbitrary")),
        name="gla_scan",
    )(q, k, v, lr_f, q, k, v, lr_b,
      a_up[0].astype(BF16), a_up[1].astype(BF16), a_bias[0:1], a_bias[1:2], tril, triu)


def _even_out_kernel(seq_len, of, ob, gate, ng, pu, pprev, pnext, pgate, pw, pscale, wa, wb, x, o_ref):
    tm = of.shape[1]
    i = pl.program_id(1)
    n_i = pl.num_programs(1)

    o = of[0] + ob[0]
    g_all = gate[0]
    parts = []
    for h in range(GLA_HEADS):
        vs = slice(h * GLA_DV, (h + 1) * GLA_DV)
        parts.append((_rms(o[:, vs], ng[...]) * _silu(g_all[:, vs])).astype(BF16))
    gla_out = jnp.concatenate(parts, axis=-1)

    u = pu[0]
    prev = jnp.where(i > 0, pprev[0], 0.0)
    nxt = jnp.where(i < n_i - 1, pnext[0], 0.0)
    ext = jnp.concatenate([prev, u, nxt], axis=0)
    n_ext = tm + 2 * POOL_HALO
    pos = i * tm + lax.broadcasted_iota(jnp.int32, (tm, 1), 0)
    mixed = []
    for gi, w in enumerate(POOL_WINDOWS):
        cs = slice(gi * POOL_DG, (gi + 1) * POOL_DG)
        a = ext[:, cs]
        span = 1
        while span < w:
            a = a + pltpu.roll(a, span, 0)
            span *= 2
        shift = w // 2 - 1
        if shift:
            a = pltpu.roll(a, n_ext - shift, 0)
        win = a[POOL_HALO:POOL_HALO + tm]
        lo = jnp.maximum(pos - w // 2, 0)
        hi = jnp.minimum(pos + w // 2, seq_len)
        pooled = win / (hi - lo).astype(F32) - u[:, cs]
        mixed.append(_dot(pooled.astype(BF16), pw[gi]))
    pool_out = (jnp.concatenate(mixed, axis=-1) * pscale[...] * _silu(pgate[0])).astype(BF16)

    y = _dot(gla_out, wa[...]) + _dot(pool_out, wb[...])
    o_ref[0] = x[0] + y


def _even_out(o_f, o_b, gate, norm_g, pool_u, pool_gate, pool_w, pool_scale, w_out, x):
    bsz, s, _ = x.shape
    tm = TOKEN_TILE
    nh = tm // POOL_HALO
    n_halo = s // POOL_HALO
    cur = lambda b, i: (b, i, 0)
    in_specs = [
        pl.BlockSpec((1, tm, GLA_V_TOT), cur), pl.BlockSpec((1, tm, GLA_V_TOT), cur),
        pl.BlockSpec((1, tm, GLA_V_TOT), cur), _const_spec((1, GLA_DV)),
        pl.BlockSpec((1, tm, POOL_W), cur),
        pl.BlockSpec((1, POOL_HALO, POOL_W), lambda b, i: (b, jnp.maximum(i * nh - 1, 0), 0)),
        pl.BlockSpec((1, POOL_HALO, POOL_W), lambda b, i: (b, jnp.minimum((i + 1) * nh, n_halo - 1), 0)),
        pl.BlockSpec((1, tm, POOL_W), cur),
        _const_spec((POOL_GROUPS, POOL_DG, POOL_DG)), _const_spec((1, POOL_W)),
        _const_spec((GLA_V_TOT, D_MODEL)), _const_spec((POOL_W, D_MODEL)),
        pl.BlockSpec((1, tm, D_MODEL), cur)]
    return pl.pallas_call(
        functools.partial(_even_out_kernel, s),
        grid=(bsz, s // tm),
        in_specs=in_specs,
        out_specs=pl.BlockSpec((1, tm, D_MODEL), cur),
        out_shape=jax.ShapeDtypeStruct((bsz, s, D_MODEL), F32),
        compiler_params=_cparams(("parallel", "parallel")),
        name="even_out",
    )(o_f, o_b, gate, norm_g[None, :], pool_u, pool_u, pool_u, pool_gate,
      pool_w.astype(BF16), pool_scale[None, :],
      w_out[:GLA_V_TOT].astype(BF16), w_out[GLA_V_TOT:].astype(BF16), x)


def _mla_qkv_kernel(cq, ckv, kr, krr, cos_h, sin_h, cos_r, sin_r, qg, kvg, wq, wkn, place, wvt,
                    q_ref, k_ref, vt_ref):
    qk_scale = math.log2(math.e) * (MLA_NOPE + MLA_ROPE) ** -0.5
    w_all = MLA_HEADS * MLA_HEAD_PAD
    nq = _rms(cq[0], qg[...]).astype(BF16)
    qq = _dot(nq, wq[...])
    cos2 = jnp.concatenate([cos_h[...], cos_h[...]], axis=-1)
    sin2 = jnp.concatenate([sin_h[...], sin_h[...]], axis=-1)
    for p in range(MLA_HEADS // 2):
        sl = slice(2 * p * MLA_HEAD_PAD, (2 * p + 2) * MLA_HEAD_PAD)
        sr = slice(w_all + 2 * p * MLA_HEAD_PAD, w_all + (2 * p + 2) * MLA_HEAD_PAD)
        q_ref[0, :, sl] = ((qq[:, sl] * cos2 + qq[:, sr] * sin2) * qk_scale).astype(BF16)

    nkv = _rms(ckv[0], kvg[...]).astype(BF16)
    k_rope = (kr[0] * cos_r[...] + krr[0] * sin_r[...]).astype(BF16)
    k_ref[0] = (_dot(nkv, wkn[...]) + _dot(k_rope, place[...])).astype(BF16)

    vt = _dot_nt(wvt[...], nkv)
    ones_row = lax.broadcasted_iota(jnp.int32, (MLA_VT_ROWS, 1), 0) == MLA_DV
    for h in range(MLA_HEADS):
        vh = vt[h * MLA_VT_ROWS:(h + 1) * MLA_VT_ROWS]
        vt_ref[0, h, 0] = jnp.where(ones_row, 1.0, vh).astype(BF16)


def _mla_qkv(cq, ckv, kr, krr, tabs, q_norm_g, kv_norm_g, wq, wkn, place, wvt):
    bsz, s, _ = cq.shape
    tm = ATT_TK
    w_all = MLA_HEADS * MLA_HEAD_PAD
    cur = lambda b, i: (b, i, 0)
    tab = lambda b, i: (i, 0)
    cos_h, sin_h, cos_r, sin_r = tabs
    in_specs = [
        pl.BlockSpec((1, tm, MLA_Q_LORA), cur), pl.BlockSpec((1, tm, MLA_KV_LORA), cur),
        pl.BlockSpec((1, tm, MLA_ROPE), cur), pl.BlockSpec((1, tm, MLA_ROPE), cur),
        pl.BlockSpec((tm, MLA_HEAD_PAD), tab), pl.BlockSpec((tm, MLA_HEAD_PAD), tab),
        pl.BlockSpec((tm, MLA_ROPE), tab), pl.BlockSpec((tm, MLA_ROPE), tab),
        _const_spec((1, MLA_Q_LORA)), _const_spec((1, MLA_KV_LORA)),
        _const_spec(wq.shape), _const_spec(wkn.shape), _const_spec(place.shape), _const_spec(wvt.shape)]
    out_specs = [
        pl.BlockSpec((1, tm, w_all), cur), pl.BlockSpec((1, tm, w_all), cur),
        pl.BlockSpec((1, MLA_HEADS, 1, MLA_VT_ROWS, tm), lambda b, i: (b, 0, i, 0, 0))]
    out_shape = [
        jax.ShapeDtypeStruct((bsz, s, w_all), BF16), jax.ShapeDtypeStruct((bsz, s, w_all), BF16),
        jax.ShapeDtypeStruct((bsz, MLA_HEADS, s // tm, MLA_VT_ROWS, tm), BF16)]
    return pl.pallas_call(
        _mla_qkv_kernel,
        grid=(bsz, s // tm),
        in_specs=in_specs,
        out_specs=out_specs,
        out_shape=out_shape,
        compiler_params=_cparams(("parallel", "parallel")),
        name="mla_qkv",
    )(cq, ckv, kr, krr, cos_h, sin_h, cos_r, sin_r, q_norm_g[None, :], kv_norm_g[None, :],
      wq, wkn, place, wvt)


def _mla_weights(q_up, kv_up):
    dq = MLA_NOPE + MLA_ROPE
    half = MLA_ROPE // 2
    qh = q_up.reshape(MLA_Q_LORA, MLA_HEADS, dq)
    zeros = jnp.zeros((MLA_Q_LORA, MLA_HEADS, MLA_HEAD_PAD - dq), F32)
    main = jnp.concatenate([qh, zeros], axis=-1)
    x1 = qh[..., MLA_NOPE:MLA_NOPE + half]
    x2 = qh[..., MLA_NOPE + half:]
    rot = jnp.concatenate([jnp.zeros((MLA_Q_LORA, MLA_HEADS, MLA_NOPE), F32), -x2, x1, zeros], axis=-1)
    w_all = MLA_HEADS * MLA_HEAD_PAD
    wq = jnp.concatenate([main.reshape(MLA_Q_LORA, w_all), rot.reshape(MLA_Q_LORA, w_all)], axis=-1)

    kvh = kv_up.reshape(MLA_KV_LORA, MLA_HEADS, MLA_NOPE + MLA_DV)
    wkn = jnp.concatenate(
        [kvh[..., :MLA_NOPE], jnp.zeros((MLA_KV_LORA, MLA_HEADS, MLA_HEAD_PAD - MLA_NOPE), F32)],
        axis=-1).reshape(MLA_KV_LORA, w_all)
    wv = jnp.transpose(kvh[..., MLA_NOPE:], (1, 2, 0))
    wvt = jnp.concatenate(
        [wv, jnp.zeros((MLA_HEADS, MLA_VT_ROWS - MLA_DV, MLA_KV_LORA), F32)],
        axis=1).reshape(MLA_HEADS * MLA_VT_ROWS, MLA_KV_LORA)
    r = jnp.arange(MLA_ROPE)
    cols = jnp.arange(w_all)
    place = ((cols[None, :] % MLA_HEAD_PAD) == (MLA_NOPE + r[:, None])).astype(BF16)
    return wq.astype(BF16), wkn.astype(BF16), place, wvt.astype(BF16)


def _rope_tables(s):
    inv = ROPE_THETA ** (-jnp.arange(0, MLA_ROPE, 2, dtype=F32) / MLA_ROPE)
    ang = jnp.arange(s, dtype=F32)[:, None] * inv[None, :]
    cos, sin = jnp.cos(ang), jnp.sin(ang)
    pad = jnp.zeros((s, MLA_HEAD_PAD - MLA_NOPE - MLA_ROPE), F32)
    cos_h = jnp.concatenate([jnp.ones((s, MLA_NOPE), F32), cos, cos, pad], axis=-1)
    sin_h = jnp.concatenate([jnp.zeros((s, MLA_NOPE), F32), sin, sin, pad], axis=-1)
    cos_r = jnp.concatenate([cos, cos], axis=-1)
    sin_r = jnp.concatenate([sin, sin], axis=-1)
    return cos_h, sin_h, cos_r, sin_r


def _mla_attn_kernel(q_ref, k_ref, vt_ref, o_ref, s_ref):
    n_k = k_ref.shape[1] // ATT_TK
    for qs in range(ATT_TQ // ATT_TQ_SUB):
        rows = slice(qs * ATT_TQ_SUB, (qs + 1) * ATT_TQ_SUB)
        qts = (q_ref[0, rows, 0:MLA_HEAD_PAD], q_ref[0, rows, MLA_HEAD_PAD:2 * MLA_HEAD_PAD])

        def produce(kk, buf):
            r = pl.multiple_of(kk * ATT_TK, ATT_TK)
            tile_max = []
            for hh in range(2):
                kt = k_ref[0, pl.ds(r, ATT_TK), hh * MLA_HEAD_PAD:(hh + 1) * MLA_HEAD_PAD]
                st = _dot_nt(kt, qts[hh])
                s_ref[buf, hh] = st
                tile_max.append(jnp.max(st, axis=0, keepdims=True))
            return tuple(tile_max)

        def consume(kk, buf, tile_max, carry):
            out = []
            for hh in range(2):
                m, acc = carry[2 * hh], carry[2 * hh + 1]
                m_new = jnp.maximum(m, tile_max[hh])
                alpha = jnp.exp2(m - m_new)
                p = jnp.exp2(s_ref[buf, hh] - m_new).astype(BF16)
                acc = acc * alpha + _dot(vt_ref[0, hh, kk], p)
                out += [m_new, acc]
            return tuple(out)

        def run(k0, state, produce_next):
            carry, tmax = state[:4], state[4:]
            for t in range(ATT_UNROLL):
                last = t == ATT_UNROLL - 1
                nxt = produce(k0 + t + 1, (t + 1) % 2) if (produce_next or not last) else ()
                carry = consume(k0 + t, t % 2, tmax, carry)
                tmax = nxt
            return carry + tmax

        m0 = jnp.full((1, ATT_TQ_SUB), -jnp.inf, F32)
        a0 = jnp.zeros((MLA_VT_ROWS, ATT_TQ_SUB), F32)
        state = lax.fori_loop(0, n_k // ATT_UNROLL - 1, lambda j, st: run(ATT_UNROLL * j, st, True),
                              (m0, a0, m0, a0) + produce(0, 0))
        carry = run(n_k - ATT_UNROLL, state, False)
        outs = [carry[2 * hh + 1][:MLA_DV] / carry[2 * hh + 1][MLA_DV:MLA_DV + 1] for hh in range(2)]
        o_ref[0, rows, :] = jnp.concatenate(outs, axis=0).T


def _mla_attn(q, k, vt):
    bsz, s, _ = q.shape
    n_kt = s // ATT_TK
    assert s % (ATT_TK * ATT_UNROLL) == 0 and s % ATT_TQ == 0 and ATT_UNROLL % 2 == 0
    return pl.pallas_call(
        _mla_attn_kernel,
        grid=(bsz, MLA_HEADS // 2, s // ATT_TQ),
        in_specs=[
            pl.BlockSpec((1, ATT_TQ, 2 * MLA_HEAD_PAD), lambda b, h, i: (b, i, h)),
            pl.BlockSpec((1, s, 2 * MLA_HEAD_PAD), lambda b, h, i: (b, 0, h)),
            pl.BlockSpec((1, 2, n_kt, MLA_VT_ROWS, ATT_TK), lambda b, h, i: (b, h, 0, 0, 0))],
        out_specs=pl.BlockSpec((1, ATT_TQ, 2 * MLA_DV), lambda b, h, i: (b, i, h)),
        out_shape=jax.ShapeDtypeStruct((bsz, s, MLA_W), F32),
        scratch_shapes=[pltpu.VMEM((2, 2, ATT_TK, ATT_TQ_SUB), F32)],
        compiler_params=_cparams(("parallel", "parallel", "arbitrary")),
        name="mla_attn",
    )(q, k, vt)


def _lane_scan(x, op, identity, reverse):
    lane = lax.broadcasted_iota(jnp.int32, x.shape, 1)
    step = 1
    while step < ML_CHUNK:
        if reverse:
            shifted = jnp.where(lane < ML_CHUNK - step, pltpu.roll(x, ML_CHUNK - step, 1), identity)
        else:
            shifted = jnp.where(lane >= step, pltpu.roll(x, step, 1), identity)
        x = op(x, shifted)
        step *= 2
    return x


def _mlstm_prologue(grf, grb, bias_r, eye, tri_pre, tri_suf, b_ref, cmax_ref, ctot_ref, gtot_ref, ccol_ref):
    n_chunks = SEQ_TILE // ML_CHUNK
    n_dir = 2 * ML_HEADS
    rows16 = lax.broadcasted_iota(jnp.int32, (1, 2 * n_dir, 1), 1)
    g = jnp.where((rows16 % n_dir) < ML_HEADS, grf[0], grb[0]) + bias_r[...][None]
    li = g[:, :n_dir].reshape(n_chunks * n_dir, ML_CHUNK)
    lf = _log_sigmoid(g[:, n_dir:]).reshape(n_chunks * n_dir, ML_CHUNK)
    is_fwd = (lax.broadcasted_iota(jnp.int32, li.shape, 0) % n_dir) < ML_HEADS
    lane = lax.broadcasted_iota(jnp.int32, li.shape, 1)
    b = jnp.where(is_fwd, _tri_right(lf, tri_pre), _tri_right(lf, tri_suf))
    c = li - b
    c_next = jnp.where(lane < ML_CHUNK - 1, pltpu.roll(c, ML_CHUNK - 1, 1), -jnp.inf)
    cmax = jnp.where(is_fwd, _lane_scan(c, jnp.maximum, -jnp.inf, False),
                     _lane_scan(c_next, jnp.maximum, -jnp.inf, True))
    shape3 = (n_chunks, n_dir, ML_CHUNK)
    b_ref[...] = b.reshape(shape3)
    cmax_ref[...] = cmax.reshape(shape3)
    ctot_ref[...] = jnp.broadcast_to(jnp.max(c, axis=-1, keepdims=True), c.shape).reshape(shape3)
    gtot_ref[...] = jnp.broadcast_to(jnp.sum(lf, axis=-1, keepdims=True), c.shape).reshape(shape3)
    hi, mid, lo = _split3(c)
    for jc in range(n_chunks):
        rs = slice(jc * n_dir, (jc + 1) * n_dir)
        ccol_ref[jc] = _dot_nt(eye, hi[rs]) + _dot_nt(eye, mid[rs]) + _dot_nt(eye, lo[rs])


def _mlstm_chunk(q_ref, k_ref, v_ref, o_ref, cnt_ref, m_ref, n, row, jc, stats, inclusive):
    b_ref, cmax_ref, ctot_ref, gtot_ref, ccol_ref = stats
    h = n % ML_HEADS
    hs = slice(h * ML_DH, (h + 1) * ML_DH)
    jj = lax.broadcasted_iota(jnp.int32, (ML_CHUNK, ML_CHUNK), 0)
    ii = lax.broadcasted_iota(jnp.int32, (ML_CHUNK, ML_CHUNK), 1)
    mask = (jj <= ii) if inclusive else (jj > ii)

    m = m_ref[n]
    b_r = b_ref[jc, n:n + 1, :]
    big_m = jnp.maximum(m, cmax_ref[jc, n:n + 1, :])
    m_tot = jnp.maximum(m, ctot_ref[jc, n:n + 1, 0:1])
    g_tot = gtot_ref[jc, n:n + 1, 0:1]
    c_b = jnp.broadcast_to(ccol_ref[jc, :, n:n + 1], (ML_CHUNK, ML_CHUNK))

    q = q_ref[0, pl.ds(row, ML_CHUNK), hs]
    ks = k_ref[0, pl.ds(row, ML_CHUNK), hs] * (ML_DH ** -0.5)
    v = v_ref[0, pl.ds(row, ML_CHUNK), hs]
    ones_row = (lax.broadcasted_iota(jnp.int32, (ML_DH, ML_CHUNK), 0) == 0).astype(BF16)
    vt_ext = jnp.concatenate([v.T, ones_row], axis=0)

    cnt = cnt_ref[n]
    lhs = jnp.concatenate([ks.astype(BF16), cnt.astype(BF16)], axis=0)
    sr = _dot_nt(lhs, q)
    pt = jnp.exp(jnp.where(mask, c_b - big_m, -jnp.inf))
    st = (sr[:ML_CHUNK] * pt).astype(BF16)
    kw = (ks * jnp.exp(c_b - m_tot)).astype(BF16)
    ho = _dot(vt_ext, jnp.concatenate([st, kw], axis=-1))
    ht = ho[:, :ML_CHUNK] + jnp.exp(m - big_m) * sr[ML_CHUNK:]
    den = ht[ML_DH:ML_DH + 1]
    scale = 1.0 / jnp.maximum(jnp.abs(den), jnp.exp(-(b_r + big_m)))
    o_ref[0, pl.ds(row, ML_CHUNK), hs] = (ht[:ML_DH] * scale).T
    cnt_ref[n] = jnp.exp(m - m_tot) * cnt + ho[:, ML_CHUNK:]
    m_ref[n] = g_tot + m_tot


def _mlstm_kernel(qf, kf, vf, grf, qb, kb, vb, grb, bias_r, eye_ref, tri_pre, tri_suf, of, ob,
                  cnt_ref, m_ref, b_ref, cmax_ref, ctot_ref, gtot_ref, ccol_ref):
    c = pl.program_id(1)

    @pl.when(c == 0)
    def _():
        cnt_ref[...] = jnp.zeros_like(cnt_ref)
        m_ref[...] = jnp.zeros_like(m_ref)

    stats = (b_ref, cmax_ref, ctot_ref, gtot_ref, ccol_ref)
    _mlstm_prologue(grf, grb, bias_r, eye_ref[...], tri_pre[...], tri_suf[...], *stats)
    n_chunks = SEQ_TILE // ML_CHUNK

    def body(j, carry):
        jb = n_chunks - 1 - j
        rf = pl.multiple_of(j * ML_CHUNK, ML_CHUNK)
        rb = pl.multiple_of(jb * ML_CHUNK, ML_CHUNK)
        for h in range(ML_HEADS):
            _mlstm_chunk(qf, kf, vf, of, cnt_ref, m_ref, h, rf, j, stats, True)
            _mlstm_chunk(qb, kb, vb, ob, cnt_ref, m_ref, ML_HEADS + h, rb, jb, stats, False)
        return carry

    lax.fori_loop(0, n_chunks, body, 0)


def _mlstm(q, k, v, gates, if_bias):
    bsz, s, _ = q.shape
    ts = SEQ_TILE
    ns = s // ts
    n_chunks = ts // ML_CHUNK
    n_gate = 4 * ML_HEADS
    n_dir = 2 * ML_HEADS
    gates_r = gates.reshape(bsz, s // ML_CHUNK, ML_CHUNK, n_gate).transpose(0, 1, 3, 2)
    fwd = lambda b, c: (b, c, 0)
    bwd = lambda b, c: (b, ns - 1 - c, 0)
    fwd4 = lambda b, c: (b, c, 0, 0)
    bwd4 = lambda b, c: (b, ns - 1 - c, 0, 0)

    def seq_specs(imap, imap4):
        return [pl.BlockSpec((1, ts, ML_W), imap), pl.BlockSpec((1, ts, ML_W), imap),
                pl.BlockSpec((1, ts, ML_W), imap),
                pl.BlockSpec((1, n_chunks, n_gate, ML_CHUNK), imap4)]

    in_specs = seq_specs(fwd, fwd4) + seq_specs(bwd, bwd4) + [
        _const_spec((n_gate, ML_CHUNK))] + [_const_spec((ML_CHUNK, ML_CHUNK))] * 3
    idx = jnp.arange(ML_CHUNK)
    tri_pre = (idx[:, None] <= idx[None, :]).astype(BF16)
    tri_suf = (idx[:, None] >= idx[None, :]).astype(BF16)
    out_specs = [pl.BlockSpec((1, ts, ML_W), fwd), pl.BlockSpec((1, ts, ML_W), bwd)]
    out_shape = [jax.ShapeDtypeStruct((bsz, s, ML_W), F32)] * 2
    stat = pltpu.VMEM((n_chunks, n_dir, ML_CHUNK), F32)
    return pl.pallas_call(
        _mlstm_kernel,
        grid=(bsz, ns),
        in_specs=in_specs,
        out_specs=out_specs,
        out_shape=out_shape,
        scratch_shapes=[pltpu.VMEM((n_dir, 2 * ML_DH, ML_DH), F32), pltpu.VMEM((n_dir, 1, 1), F32),
                        stat, stat, stat, stat, pltpu.VMEM((n_chunks, ML_CHUNK, n_dir), F32)],
        compiler_params=_cparams(("parallel", "arbitrary")),
        name="mlstm_scan",
    )(q, k, v, gates_r, q, k, v, gates_r,
      jnp.broadcast_to(if_bias[:, None], (n_gate, ML_CHUNK)), jnp.eye(ML_CHUNK, dtype=BF16),
      tri_pre, tri_suf)


def _odd_out_kernel(final, att, mgate, hf, hb, mo, lgate, ng, wa, wb, x, fg, o_ref):
    mla_out = (att[0] * _silu(mgate[0])).astype(BF16)
    hm = hf[0] + hb[0]
    mo_v, lg_v = mo[0], lgate[0]
    parts = []
    for h in range(ML_HEADS):
        hs = slice(h * ML_DH, (h + 1) * ML_DH)
        y = _rms(hm[:, hs], ng[...]) * _sigmoid(mo_v[:, hs])
        parts.append((y * _silu(lg_v[:, hs])).astype(BF16))
    ml_out = jnp.concatenate(parts, axis=-1)
    xn = x[0] + (_dot(mla_out, wa[...]) + _dot(ml_out, wb[...]))
    o_ref[0] = _rms(xn, fg[...]) if final else xn


def _odd_out(att, mla_gate, h_f, h_b, mo, ml_gate, norm_g, w_out, x, final_g, final):
    bsz, s, _ = x.shape
    tm = TOKEN_TILE
    cur = lambda b, i: (b, i, 0)
    half = pl.BlockSpec((1, tm, MLA_W), cur)
    in_specs = [half] * 6 + [
        _const_spec((1, ML_DH)), _const_spec((MLA_W, D_MODEL)), _const_spec((ML_W, D_MODEL)),
        pl.BlockSpec((1, tm, D_MODEL), cur), _const_spec((1, D_MODEL))]
    return pl.pallas_call(
        functools.partial(_odd_out_kernel, final),
        grid=(bsz, s // tm),
        in_specs=in_specs,
        out_specs=pl.BlockSpec((1, tm, D_MODEL), cur),
        out_shape=jax.ShapeDtypeStruct((bsz, s, D_MODEL), F32),
        compiler_params=_cparams(("parallel", "parallel")),
        name="odd_out",
    )(att, mla_gate, h_f, h_b, mo, ml_gate, norm_g[None, :],
      w_out[:MLA_W].astype(BF16), w_out[MLA_W:].astype(BF16), x, final_g[None, :])


def _col_split(w, sizes):
    out, off = [], 0
    for n in sizes:
        out.append(w[:, off:off + n].astype(BF16))
        off += n
    return out


def _even_layer(x, g, w_in, a_up, a_bias, gla_norm_g, pool_w, pool_scale, w_out):
    bsz, s, _ = x.shape
    sizes = (GLA_K_TOT, GLA_K_TOT, GLA_V_TOT, GLA_V_TOT, GLA_LR, GLA_LR, POOL_W, POOL_W)
    dts = (F32, F32, BF16, F32, BF16, BF16, F32, F32)
    outs = _norm_proj(x.reshape(bsz * s, D_MODEL), g[None, :], _col_split(w_in, sizes), dts)
    q, k, v, gate, lr_f, lr_b, pool_u, pool_gate = [o.reshape(bsz, s, -1) for o in outs]
    o_f, o_b = _gla(q, k, v, lr_f, lr_b, a_up, a_bias)
    return _even_out(o_f, o_b, gate, gla_norm_g, pool_u, pool_gate, pool_w, pool_scale, w_out, x)


def _odd_layer(x, g, tabs, w_in, q_norm_g, q_up, kv_norm_g, kv_up, if_bias, ml_norm_g, w_out,
               final_g, final):
    bsz, s, _ = x.shape
    half = MLA_ROPE // 2
    off = MLA_Q_LORA + MLA_KV_LORA
    kr_w = w_in[:, off:off + MLA_ROPE]
    w_ext = jnp.concatenate([w_in, -kr_w[:, half:], kr_w[:, :half]], axis=-1)
    sizes = (MLA_Q_LORA, MLA_KV_LORA, MLA_ROPE, MLA_W, ML_W, ML_W, ML_W, ML_W, 4 * ML_HEADS, ML_W,
             MLA_ROPE)
    dts = (F32, F32, F32, F32, BF16, F32, BF16, F32, F32, F32, F32)
    outs = _norm_proj(x.reshape(bsz * s, D_MODEL), g[None, :], _col_split(w_ext, sizes), dts)
    cq, ckv, kr, mla_gate, mq, mk, mv, mo, mif, ml_gate, krr = [o.reshape(bsz, s, -1) for o in outs]
    wq, wkn, place, wvt = _mla_weights(q_up, kv_up)
    qa, ka, vt = _mla_qkv(cq, ckv, kr, krr, tabs, q_norm_g, kv_norm_g, wq, wkn, place, wvt)
    att = _mla_attn(qa, ka, vt)
    h_f, h_b = _mlstm(mq, mk, mv, mif, if_bias)
    return _odd_out(att, mla_gate, h_f, h_b, mo, ml_gate, ml_norm_g, w_out, x, final_g, final)


def _trunk(x, norm_g, final_norm_g, e_w_in, e_gla_a_up, e_gla_a_bias, e_gla_norm_g, e_pool_w,
           e_pool_scale, e_w_out, o_w_in, o_q_norm_g, o_q_up, o_kv_norm_g, o_kv_up, o_if_bias,
           o_mlstm_norm_g, o_w_out):
    depth = norm_g.shape[0]
    tabs = _rope_tables(x.shape[1])
    for layer in range(depth):
        i = layer // 2
        if layer % 2 == 0:
            x = _even_layer(x, norm_g[layer], e_w_in[i], e_gla_a_up[i], e_gla_a_bias[i],
                            e_gla_norm_g[i], e_pool_w[i], e_pool_scale[i], e_w_out[i])
        else:
            x = _odd_layer(x, norm_g[layer], tabs, o_w_in[i], o_q_norm_g[i], o_q_up[i],
                           o_kv_norm_g[i], o_kv_up[i], o_if_bias[i], o_mlstm_norm_g[i], o_w_out[i],
                           final_norm_g, layer == depth - 1)
    return x


def kernel(x_prompt, x_sample, norm_g, final_norm_g, e_w_in, e_gla_a_up, e_gla_a_bias, e_gla_norm_g,
           e_pool_w, e_pool_scale, e_w_out, o_w_in, o_q_norm_g, o_q_up, o_kv_norm_g, o_kv_up,
           o_if_bias, o_mlstm_norm_g, o_w_out):
    params = (norm_g, final_norm_g, e_w_in, e_gla_a_up, e_gla_a_bias, e_gla_norm_g, e_pool_w,
              e_pool_scale, e_w_out, o_w_in, o_q_norm_g, o_q_up, o_kv_norm_g, o_kv_up, o_if_bias,
              o_mlstm_norm_g, o_w_out)
    return (_trunk(x_prompt, *params), _trunk(x_sample, *params))
```

```python
import functools
import math

import jax
import jax.numpy as jnp
from jax import lax
from jax.experimental import pallas as pl
from jax.experimental.pallas import tpu as pltpu

F32 = jnp.float32
BF16 = jnp.bfloat16

D_MODEL = 1024
NORM_EPS = 1e-6
CHUNK = 64

GLA_HEADS = 4
GLA_DK = 128
GLA_DV = 256
GLA_LR = 16
GLA_GATE_NORM = 16.0
GLA_K_TOT = GLA_HEADS * GLA_DK
GLA_V_TOT = GLA_HEADS * GLA_DV
GLA_BLOCK = 2 * CHUNK

POOL_GROUPS = 4
POOL_WINDOWS = (2, 4, 8, 16)
POOL_DG = 128
POOL_W = POOL_GROUPS * POOL_DG
POOL_HALO = 8

MLA_HEADS = 8
MLA_NOPE = 64
MLA_ROPE = 32
MLA_DV = 64
MLA_Q_LORA = 384
MLA_KV_LORA = 256
MLA_W = MLA_HEADS * MLA_DV
MLA_HEAD_PAD = 128
MLA_VT_ROWS = 80
MLA_MISC_W = 128
MLA_LAT_W = MLA_Q_LORA + MLA_KV_LORA + MLA_MISC_W
ROPE_THETA = 10000.0

ML_HEADS = 4
ML_DH = 128
ML_W = ML_HEADS * ML_DH
ML_CHUNK = 128

VMEM_LIMIT_BYTES = 56 * 1024 * 1024

TOKEN_TILE = 512
SEQ_TILE = 512
ML_SEQ_TILE = 1024
ATT_TQ = 1024
ATT_TQ_SUB = 256
ATT_TK = 512
ATT_UNROLL = 4


def _cparams(sem):
    return pltpu.CompilerParams(dimension_semantics=sem, vmem_limit_bytes=VMEM_LIMIT_BYTES)


def _const_spec(shape):
    nd = len(shape)
    return pl.BlockSpec(shape, lambda *_: (0,) * nd)


def _dot(a, b):
    return jnp.dot(a, b, preferred_element_type=F32)


def _dot_nt(a, b):
    return lax.dot_general(a, b, (((1,), (1,)), ((), ())), preferred_element_type=F32)


def _dot_tn(a, b):
    return lax.dot_general(a, b, (((0,), (0,)), ((), ())), preferred_element_type=F32)


def _sigmoid(x):
    return 1.0 / (1.0 + jnp.exp(-x))


def _silu(x):
    return x * _sigmoid(x)


def _log_sigmoid(x):
    return jnp.minimum(x, 0.0) - jnp.log(1.0 + jnp.exp(-jnp.abs(x)))


def _rms(x, g):
    return x * lax.rsqrt(jnp.mean(x * x, axis=-1, keepdims=True) + NORM_EPS) * g


def _split3(x):
    hi = x.astype(BF16)
    r1 = x - hi.astype(F32)
    mid = r1.astype(BF16)
    lo = (r1 - mid.astype(F32)).astype(BF16)
    return hi, mid, lo


def _tri_left(tri, x):
    hi, mid, lo = _split3(x)
    return _dot(tri, hi) + _dot(tri, mid) + _dot(tri, lo)


def _tri_right(x, tri):
    hi, mid, lo = _split3(x)
    return _dot(hi, tri) + _dot(mid, tri) + _dot(lo, tri)


def _norm_proj_kernel(n_out, x_ref, g_ref, *refs):
    w_refs, o_refs = refs[:n_out], refs[n_out:]
    h = _rms(x_ref[...], g_ref[...]).astype(BF16)
    for w_ref, o_ref in zip(w_refs, o_refs):
        o_ref[...] = _dot(h, w_ref[...]).astype(o_ref.dtype)


def _norm_proj(x2d, g, weights, out_dtypes):
    t = x2d.shape[0]
    tm = TOKEN_TILE
    n_out = len(weights)
    in_specs = [pl.BlockSpec((tm, D_MODEL), lambda i: (i, 0)), _const_spec((1, D_MODEL))]
    in_specs += [_const_spec(w.shape) for w in weights]
    out_specs = [pl.BlockSpec((tm, w.shape[1]), lambda i: (i, 0)) for w in weights]
    out_shape = [jax.ShapeDtypeStruct((t, w.shape[1]), dt) for w, dt in zip(weights, out_dtypes)]
    return pl.pallas_call(
        functools.partial(_norm_proj_kernel, n_out),
        grid=(t // tm,),
        in_specs=in_specs,
        out_specs=out_specs,
        out_shape=out_shape,
        compiler_params=_cparams(("parallel",)),
        name="norm_proj",
    )(x2d, g, *weights)


def _gla_block(q_ref, k_ref, v_ref, o_ref, st_ref, st_idx, row, b_all, inclusive):
    ii = lax.broadcasted_iota(jnp.int32, (GLA_BLOCK, GLA_BLOCK), 0)
    jj = lax.broadcasted_iota(jnp.int32, (GLA_BLOCK, GLA_BLOCK), 1)
    rr = lax.broadcasted_iota(jnp.int32, (GLA_BLOCK, 1), 0)
    same = (ii >= CHUNK) == (jj >= CHUNK)
    if inclusive:
        intra, cross, far_rows = same & (jj <= ii), (ii >= CHUNK) & (jj < CHUNK), rr >= CHUNK
        edge_lo, edge_hi = CHUNK - 1, GLA_BLOCK - 1
    else:
        intra, cross, far_rows = same & (jj > ii), (ii < CHUNK) & (jj >= CHUNK), rr < CHUNK
        edge_lo, edge_hi = 0, CHUNK
    for h in range(GLA_HEADS):
        ks = slice(h * GLA_DK, (h + 1) * GLA_DK)
        vs = slice(h * GLA_DV, (h + 1) * GLA_DV)
        q = q_ref[0, pl.ds(row, GLA_BLOCK), ks] * (GLA_DK ** -0.5)
        k = k_ref[0, pl.ds(row, GLA_BLOCK), ks]
        v = v_ref[0, pl.ds(row, GLA_BLOCK), vs]
        b = b_all[:, ks]
        tot_lo, tot_hi = b[edge_lo:edge_lo + 1, :], b[edge_hi:edge_hi + 1, :]
        tot_near, tot_far = (tot_lo, tot_hi) if inclusive else (tot_hi, tot_lo)
        qe = q * jnp.exp(b)
        kd = (k * jnp.exp(-b)).astype(BF16)
        kdec = k * jnp.exp(jnp.where(rr < CHUNK, tot_lo, tot_hi) - b)
        pp = _dot_nt(qe.astype(BF16), jnp.concatenate([kd, kdec.astype(BF16)], axis=0))
        a = jnp.where(intra, pp[:, :GLA_BLOCK], jnp.where(cross, pp[:, GLA_BLOCK:], 0.0)).astype(BF16)
        qe_in = jnp.where(far_rows, qe * jnp.exp(tot_near), qe).astype(BF16)
        kdec_out = jnp.where(far_rows, kdec, kdec * jnp.exp(tot_far)).astype(BF16)
        st = st_ref[st_idx + h]
        o_ref[0, pl.ds(row, GLA_BLOCK), vs] = (_dot(a, v) + _dot_nt(qe_in, st.astype(BF16))).astype(o_ref.dtype)
        st_ref[st_idx + h] = st * jnp.exp(tot_near + tot_far) + _dot_tn(v, kdec_out)


def _gla_kernel(qf, kf, vf, lrf, qb, kb, vb, lrb, aupf, aupb, biasf, biasb, tril, triu,
                of, ob, st_ref, la_ref):
    c = pl.program_id(1)

    @pl.when(c == 0)
    def _():
        st_ref[...] = jnp.zeros_like(st_ref)

    inv = 1.0 / GLA_GATE_NORM
    la_ref[0] = _log_sigmoid(_dot(lrf[0], aupf[...]) + biasf[...]) * inv
    la_ref[1] = _log_sigmoid(_dot(lrb[0], aupb[...]) + biasb[...]) * inv
    n_blocks = SEQ_TILE // GLA_BLOCK

    def body(j, carry):
        rf = pl.multiple_of(j * GLA_BLOCK, GLA_BLOCK)
        rb = pl.multiple_of((n_blocks - 1 - j) * GLA_BLOCK, GLA_BLOCK)
        b_f = _tri_left(tril[...], la_ref[0, pl.ds(rf, GLA_BLOCK), :])
        b_b = _tri_left(triu[...], la_ref[1, pl.ds(rb, GLA_BLOCK), :])
        _gla_block(qf, kf, vf, of, st_ref, 0, rf, b_f, True)
        _gla_block(qb, kb, vb, ob, st_ref, GLA_HEADS, rb, b_b, False)
        return carry

    lax.fori_loop(0, n_blocks, body, 0)


def _gla(q, k, v, lr, a_up, a_bias):
    bsz, s, _ = q.shape
    ts = SEQ_TILE
    ns = s // ts
    fwd = lambda b, c: (b, c, 0)
    bwd = lambda b, c: (b, ns - 1 - c, 0)
    idx = jnp.arange(GLA_BLOCK)
    same_chunk = (idx[None, :] // CHUNK) == (idx[:, None] // CHUNK)
    tril = (same_chunk & (idx[None, :] <= idx[:, None])).astype(BF16)
    triu = (same_chunk & (idx[None, :] >= idx[:, None])).astype(BF16)

    def seq_specs(imap):
        return [pl.BlockSpec((1, ts, GLA_K_TOT), imap), pl.BlockSpec((1, ts, GLA_K_TOT), imap),
                pl.BlockSpec((1, ts, GLA_V_TOT), imap), pl.BlockSpec((1, ts, 2 * GLA_LR), imap)]

    in_specs = seq_specs(fwd) + seq_specs(bwd) + [
        _const_spec((2 * GLA_LR, GLA_K_TOT)), _const_spec((2 * GLA_LR, GLA_K_TOT)),
        _const_spec((1, GLA_K_TOT)), _const_spec((1, GLA_K_TOT)),
        _const_spec((GLA_BLOCK, GLA_BLOCK)), _const_spec((GLA_BLOCK, GLA_BLOCK))]
    out_specs = [pl.BlockSpec((1, ts, GLA_V_TOT), fwd), pl.BlockSpec((1, ts, GLA_V_TOT), bwd)]
    out_shape = [jax.ShapeDtypeStruct((bsz, s, GLA_V_TOT), BF16)] * 2
    zeros = jnp.zeros((GLA_LR, GLA_K_TOT), F32)
    aup_f = jnp.concatenate([a_up[0], zeros], axis=0).astype(BF16)
    aup_b = jnp.concatenate([zeros, a_up[1]], axis=0).astype(BF16)
    return pl.pallas_call(
        _gla_kernel,
        grid=(bsz, ns),
        in_specs=in_specs,
        out_specs=out_specs,
        out_shape=out_shape,
        scratch_shapes=[pltpu.VMEM((2 * GLA_HEADS, GLA_DV, GLA_DK), F32),
                        pltpu.VMEM((2, ts, GLA_K_TOT), F32)],
        compiler_params=_cparams(("parallel", "arbitrary")),
        name="gla_scan",
    )(q, k, v, lr, q, k, v, lr, aup_f, aup_b, a_bias[0:1], a_bias[1:2], tril, triu)


def _even_out_kernel(seq_len, of, ob, gate, ng, pu, pprev, pnext, pgate, pw, pscale, wa, wb, x, o_ref):
    tm = of.shape[1]
    i = pl.program_id(1)
    n_i = pl.num_programs(1)

    o = of[0].astype(F32) + ob[0].astype(F32)
    g_all = gate[0].astype(F32)
    parts = []
    for h in range(GLA_HEADS):
        vs = slice(h * GLA_DV, (h + 1) * GLA_DV)
        parts.append((_rms(o[:, vs], ng[...]) * _silu(g_all[:, vs])).astype(BF16))
    gla_out = jnp.concatenate(parts, axis=-1)

    u = pu[0]
    prev = jnp.where(i > 0, pprev[0], 0.0)
    nxt = jnp.where(i < n_i - 1, pnext[0], 0.0)
    ext = jnp.concatenate([prev, u, nxt], axis=0)
    n_ext = tm + 2 * POOL_HALO
    pos = i * tm + lax.broadcasted_iota(jnp.int32, (tm, 1), 0)
    mixed = []
    for gi, w in enumerate(POOL_WINDOWS):
        cs = slice(gi * POOL_DG, (gi + 1) * POOL_DG)
        a = ext[:, cs]
        span = 1
        while span < w:
            a = a + pltpu.roll(a, span, 0)
            span *= 2
        shift = w // 2 - 1
        if shift:
            a = pltpu.roll(a, n_ext - shift, 0)
        win = a[POOL_HALO:POOL_HALO + tm]
        lo = jnp.maximum(pos - w // 2, 0)
        hi = jnp.minimum(pos + w // 2, seq_len)
        pooled = win / (hi - lo).astype(F32) - u[:, cs]
        mixed.append(_dot(pooled.astype(BF16), pw[gi]))
    pool_out = (jnp.concatenate(mixed, axis=-1) * pscale[...] * _silu(pgate[0].astype(F32))).astype(BF16)

    y = _dot(gla_out, wa[...]) + _dot(pool_out, wb[...])
    o_ref[0] = x[0] + y


def _even_out(o_f, o_b, gate, norm_g, pool_u, pool_gate, pool_w, pool_scale, w_out, x):
    bsz, s, _ = x.shape
    tm = TOKEN_TILE
    nh = tm // POOL_HALO
    n_halo = s // POOL_HALO
    cur = lambda b, i: (b, i, 0)
    in_specs = [
        pl.BlockSpec((1, tm, GLA_V_TOT), cur), pl.BlockSpec((1, tm, GLA_V_TOT), cur),
        pl.BlockSpec((1, tm, GLA_V_TOT), cur), _const_spec((1, GLA_DV)),
        pl.BlockSpec((1, tm, POOL_W), cur),
        pl.BlockSpec((1, POOL_HALO, POOL_W), lambda b, i: (b, jnp.maximum(i * nh - 1, 0), 0)),
        pl.BlockSpec((1, POOL_HALO, POOL_W), lambda b, i: (b, jnp.minimum((i + 1) * nh, n_halo - 1), 0)),
        pl.BlockSpec((1, tm, POOL_W), cur),
        _const_spec((POOL_GROUPS, POOL_DG, POOL_DG)), _const_spec((1, POOL_W)),
        _const_spec((GLA_V_TOT, D_MODEL)), _const_spec((POOL_W, D_MODEL)),
        pl.BlockSpec((1, tm, D_MODEL), cur)]
    return pl.pallas_call(
        functools.partial(_even_out_kernel, s),
        grid=(bsz, s // tm),
        in_specs=in_specs,
        out_specs=pl.BlockSpec((1, tm, D_MODEL), cur),
        out_shape=jax.ShapeDtypeStruct((bsz, s, D_MODEL), F32),
        compiler_params=_cparams(("parallel", "parallel")),
        name="even_out",
    )(o_f, o_b, gate, norm_g[None, :], pool_u, pool_u, pool_u, pool_gate,
      pool_w.astype(BF16), pool_scale[None, :],
      w_out[:GLA_V_TOT].astype(BF16), w_out[GLA_V_TOT:].astype(BF16), x)


def _mla_qkv_kernel(lat, cos_h, sin_h, rope_r, qg, kvg, wq, wkn, place, wvt, q_ref, k_ref, vt_ref):
    qk_scale = math.log2(math.e) * (MLA_NOPE + MLA_ROPE) ** -0.5
    w_all = MLA_HEADS * MLA_HEAD_PAD
    x = lat[0]
    cq = x[:, :MLA_Q_LORA]
    ckv = x[:, MLA_Q_LORA:MLA_Q_LORA + MLA_KV_LORA]
    nq = _rms(cq, qg[...]).astype(BF16)
    qq = _dot(nq, wq[...])
    cos2 = jnp.concatenate([cos_h[...], cos_h[...]], axis=-1)
    sin2 = jnp.concatenate([sin_h[...], sin_h[...]], axis=-1)
    for p in range(MLA_HEADS // 2):
        sl = slice(2 * p * MLA_HEAD_PAD, (2 * p + 2) * MLA_HEAD_PAD)
        sr = slice(w_all + 2 * p * MLA_HEAD_PAD, w_all + (2 * p + 2) * MLA_HEAD_PAD)
        q_ref[0, :, sl] = ((qq[:, sl] * cos2 + qq[:, sr] * sin2) * qk_scale).astype(BF16)

    nkv = _rms(ckv, kvg[...]).astype(BF16)
    k_rope = (x[:, MLA_Q_LORA + MLA_KV_LORA:] * rope_r[...]).astype(BF16)
    k_ref[0] = (_dot(nkv, wkn[...]) + _dot(k_rope, place[...])).astype(BF16)

    vt = _dot_nt(wvt[...], nkv)
    ones_row = lax.broadcasted_iota(jnp.int32, (MLA_VT_ROWS, 1), 0) == MLA_DV
    for h in range(MLA_HEADS):
        vh = vt[h * MLA_VT_ROWS:(h + 1) * MLA_VT_ROWS]
        vt_ref[0, h, 0] = jnp.where(ones_row, 1.0, vh).astype(BF16)


def _mla_qkv(lat, tabs, q_norm_g, kv_norm_g, wq, wkn, place, wvt):
    bsz, s, _ = lat.shape
    tm = ATT_TK
    w_all = MLA_HEADS * MLA_HEAD_PAD
    cur = lambda b, i: (b, i, 0)
    tab = lambda b, i: (i, 0)
    cos_h, sin_h, rope_r = tabs
    in_specs = [
        pl.BlockSpec((1, tm, MLA_LAT_W), cur),
        pl.BlockSpec((tm, MLA_HEAD_PAD), tab), pl.BlockSpec((tm, MLA_HEAD_PAD), tab),
        pl.BlockSpec((tm, MLA_MISC_W), tab),
        _const_spec((1, MLA_Q_LORA)), _const_spec((1, MLA_KV_LORA)),
        _const_spec(wq.shape), _const_spec(wkn.shape), _const_spec(place.shape), _const_spec(wvt.shape)]
    out_specs = [
        pl.BlockSpec((1, tm, w_all), cur), pl.BlockSpec((1, tm, w_all), cur),
        pl.BlockSpec((1, MLA_HEADS, 1, MLA_VT_ROWS, tm), lambda b, i: (b, 0, i, 0, 0))]
    out_shape = [
        jax.ShapeDtypeStruct((bsz, s, w_all), BF16), jax.ShapeDtypeStruct((bsz, s, w_all), BF16),
        jax.ShapeDtypeStruct((bsz, MLA_HEADS, s // tm, MLA_VT_ROWS, tm), BF16)]
    return pl.pallas_call(
        _mla_qkv_kernel,
        grid=(bsz, s // tm),
        in_specs=in_specs,
        out_specs=out_specs,
        out_shape=out_shape,
        compiler_params=_cparams(("parallel", "parallel")),
        name="mla_qkv",
    )(lat, cos_h, sin_h, rope_r, q_norm_g[None, :], kv_norm_g[None, :], wq, wkn, place, wvt)


def _mla_weights(q_up, kv_up):
    dq = MLA_NOPE + MLA_ROPE
    half = MLA_ROPE // 2
    qh = q_up.reshape(MLA_Q_LORA, MLA_HEADS, dq)
    zeros = jnp.zeros((MLA_Q_LORA, MLA_HEADS, MLA_HEAD_PAD - dq), F32)
    main = jnp.concatenate([qh, zeros], axis=-1)
    x1 = qh[..., MLA_NOPE:MLA_NOPE + half]
    x2 = qh[..., MLA_NOPE + half:]
    rot = jnp.concatenate([jnp.zeros((MLA_Q_LORA, MLA_HEADS, MLA_NOPE), F32), -x2, x1, zeros], axis=-1)
    w_all = MLA_HEADS * MLA_HEAD_PAD
    wq = jnp.concatenate([main.reshape(MLA_Q_LORA, w_all), rot.reshape(MLA_Q_LORA, w_all)], axis=-1)

    kvh = kv_up.reshape(MLA_KV_LORA, MLA_HEADS, MLA_NOPE + MLA_DV)
    wkn = jnp.concatenate(
        [kvh[..., :MLA_NOPE], jnp.zeros((MLA_KV_LORA, MLA_HEADS, MLA_HEAD_PAD - MLA_NOPE), F32)],
        axis=-1).reshape(MLA_KV_LORA, w_all)
    wv = jnp.transpose(kvh[..., MLA_NOPE:], (1, 2, 0))
    wvt = jnp.concatenate(
        [wv, jnp.zeros((MLA_HEADS, MLA_VT_ROWS - MLA_DV, MLA_KV_LORA), F32)],
        axis=1).reshape(MLA_HEADS * MLA_VT_ROWS, MLA_KV_LORA)
    r = jnp.arange(MLA_MISC_W)
    cols = jnp.arange(w_all)
    place = ((r[:, None] < 2 * MLA_ROPE)
             & ((cols[None, :] % MLA_HEAD_PAD) == (MLA_NOPE + r[:, None] % MLA_ROPE))).astype(BF16)
    return wq.astype(BF16), wkn.astype(BF16), place, wvt.astype(BF16)


def _rope_tables(s):
    inv = ROPE_THETA ** (-jnp.arange(0, MLA_ROPE, 2, dtype=F32) / MLA_ROPE)
    ang = jnp.arange(s, dtype=F32)[:, None] * inv[None, :]
    cos, sin = jnp.cos(ang), jnp.sin(ang)
    pad = jnp.zeros((s, MLA_HEAD_PAD - MLA_NOPE - MLA_ROPE), F32)
    cos_h = jnp.concatenate([jnp.ones((s, MLA_NOPE), F32), cos, cos, pad], axis=-1)
    sin_h = jnp.concatenate([jnp.zeros((s, MLA_NOPE), F32), sin, sin, pad], axis=-1)
    rope_r = jnp.concatenate([cos, cos, sin, sin, jnp.zeros((s, MLA_MISC_W - 2 * MLA_ROPE), F32)], axis=-1)
    return cos_h, sin_h, rope_r


def _mla_attn_kernel(q_ref, k_ref, vt_ref, o_ref, s_ref):
    n_k = k_ref.shape[1] // ATT_TK
    for qs in range(ATT_TQ // ATT_TQ_SUB):
        rows = slice(qs * ATT_TQ_SUB, (qs + 1) * ATT_TQ_SUB)
        qts = (q_ref[0, rows, 0:MLA_HEAD_PAD], q_ref[0, rows, MLA_HEAD_PAD:2 * MLA_HEAD_PAD])

        def produce(kk, buf):
            r = pl.multiple_of(kk * ATT_TK, ATT_TK)
            tile_max = []
            for hh in range(2):
                kt = k_ref[0, pl.ds(r, ATT_TK), hh * MLA_HEAD_PAD:(hh + 1) * MLA_HEAD_PAD]
                st = _dot_nt(kt, qts[hh])
                s_ref[buf, hh] = st
                tile_max.append(jnp.max(st, axis=0, keepdims=True))
            return tuple(tile_max)

        def consume(kk, buf, tile_max, carry):
            out = []
            for hh in range(2):
                m, acc = carry[2 * hh], carry[2 * hh + 1]
                m_new = jnp.maximum(m, tile_max[hh])
                alpha = jnp.exp2(m - m_new)
                p = jnp.exp2(s_ref[buf, hh] - m_new).astype(BF16)
                acc = acc * alpha + _dot(vt_ref[0, hh, kk], p)
                out += [m_new, acc]
            return tuple(out)

        def run(k0, state, produce_next):
            carry, tmax = state[:4], state[4:]
            for t in range(ATT_UNROLL):
                last = t == ATT_UNROLL - 1
                nxt = produce(k0 + t + 1, (t + 1) % 2) if (produce_next or not last) else ()
                carry = consume(k0 + t, t % 2, tmax, carry)
                tmax = nxt
            return carry + tmax

        m0 = jnp.full((1, ATT_TQ_SUB), -jnp.inf, F32)
        a0 = jnp.zeros((MLA_VT_ROWS, ATT_TQ_SUB), F32)
        state = lax.fori_loop(0, n_k // ATT_UNROLL - 1, lambda j, st: run(ATT_UNROLL * j, st, True),
                              (m0, a0, m0, a0) + produce(0, 0))
        carry = run(n_k - ATT_UNROLL, state, False)
        outs = [carry[2 * hh + 1][:MLA_DV] / carry[2 * hh + 1][MLA_DV:MLA_DV + 1] for hh in range(2)]
        o_ref[0, rows, :] = jnp.concatenate(outs, axis=0).T.astype(o_ref.dtype)


def _mla_attn(q, k, vt):
    bsz, s, _ = q.shape
    n_kt = s // ATT_TK
    assert s % (ATT_TK * ATT_UNROLL) == 0 and s % ATT_TQ == 0 and ATT_UNROLL % 2 == 0
    return pl.pallas_call(
        _mla_attn_kernel,
        grid=(bsz, MLA_HEADS // 2, s // ATT_TQ),
        in_specs=[
            pl.BlockSpec((1, ATT_TQ, 2 * MLA_HEAD_PAD), lambda b, h, i: (b, i, h)),
            pl.BlockSpec((1, s, 2 * MLA_HEAD_PAD), lambda b, h, i: (b, 0, h)),
            pl.BlockSpec((1, 2, n_kt, MLA_VT_ROWS, ATT_TK), lambda b, h, i: (b, h, 0, 0, 0))],
        out_specs=pl.BlockSpec((1, ATT_TQ, 2 * MLA_DV), lambda b, h, i: (b, i, h)),
        out_shape=jax.ShapeDtypeStruct((bsz, s, MLA_W), BF16),
        scratch_shapes=[pltpu.VMEM((2, 2, ATT_TK, ATT_TQ_SUB), F32)],
        compiler_params=_cparams(("parallel", "parallel", "arbitrary")),
        name="mla_attn",
    )(q, k, vt)


def _lane_scan(x, op, identity, reverse):
    lane = lax.broadcasted_iota(jnp.int32, x.shape, 1)
    step = 1
    while step < ML_CHUNK:
        if reverse:
            shifted = jnp.where(lane < ML_CHUNK - step, pltpu.roll(x, ML_CHUNK - step, 1), identity)
        else:
            shifted = jnp.where(lane >= step, pltpu.roll(x, step, 1), identity)
        x = op(x, shifted)
        step *= 2
    return x


def _mlstm_prologue(grf, grb, bias_r, eye, tri_pre, tri_suf, b_ref, cmax_ref, ctot_ref, gtot_ref, ccol_ref):
    n_chunks = ML_SEQ_TILE // ML_CHUNK
    n_dir = 2 * ML_HEADS
    rows16 = lax.broadcasted_iota(jnp.int32, (1, 2 * n_dir, 1), 1)
    g = jnp.where((rows16 % n_dir) < ML_HEADS, grf[0], grb[0]) + bias_r[...][None]
    li = g[:, :n_dir].reshape(n_chunks * n_dir, ML_CHUNK)
    lf = _log_sigmoid(g[:, n_dir:]).reshape(n_chunks * n_dir, ML_CHUNK)
    is_fwd = (lax.broadcasted_iota(jnp.int32, li.shape, 0) % n_dir) < ML_HEADS
    lane = lax.broadcasted_iota(jnp.int32, li.shape, 1)
    b = jnp.where(is_fwd, _tri_right(lf, tri_pre), _tri_right(lf, tri_suf))
    c = li - b
    c_next = jnp.where(lane < ML_CHUNK - 1, pltpu.roll(c, ML_CHUNK - 1, 1), -jnp.inf)
    cmax = jnp.where(is_fwd, _lane_scan(c, jnp.maximum, -jnp.inf, False),
                     _lane_scan(c_next, jnp.maximum, -jnp.inf, True))
    shape3 = (n_chunks, n_dir, ML_CHUNK)
    b_ref[...] = b.reshape(shape3)
    cmax_ref[...] = cmax.reshape(shape3)
    ctot_ref[...] = jnp.broadcast_to(jnp.max(c, axis=-1, keepdims=True), c.shape).reshape(shape3)
    gtot_ref[...] = jnp.broadcast_to(jnp.sum(lf, axis=-1, keepdims=True), c.shape).reshape(shape3)
    hi, mid, lo = _split3(c)
    for jc in range(n_chunks):
        rs = slice(jc * n_dir, (jc + 1) * n_dir)
        ccol_ref[jc] = _dot_nt(eye, hi[rs]) + _dot_nt(eye, mid[rs]) + _dot_nt(eye, lo[rs])


def _mlstm_chunk(q_ref, k_ref, v_ref, o_ref, cnt_ref, m_ref, n, row, jc, stats, inclusive):
    b_ref, cmax_ref, ctot_ref, gtot_ref, ccol_ref = stats
    h = n % ML_HEADS
    hs = slice(h * ML_DH, (h + 1) * ML_DH)
    jj = lax.broadcasted_iota(jnp.int32, (ML_CHUNK, ML_CHUNK), 0)
    ii = lax.broadcasted_iota(jnp.int32, (ML_CHUNK, ML_CHUNK), 1)
    mask = (jj <= ii) if inclusive else (jj > ii)

    m = m_ref[n]
    b_r = b_ref[jc, n:n + 1, :]
    big_m = jnp.maximum(m, cmax_ref[jc, n:n + 1, :])
    m_tot = jnp.maximum(m, ctot_ref[jc, n:n + 1, 0:1])
    g_tot = gtot_ref[jc, n:n + 1, 0:1]
    c_b = jnp.broadcast_to(ccol_ref[jc, :, n:n + 1], (ML_CHUNK, ML_CHUNK))

    q = q_ref[0, pl.ds(row, ML_CHUNK), hs]
    ks = k_ref[0, pl.ds(row, ML_CHUNK), hs] * (ML_DH ** -0.5)
    v = v_ref[0, pl.ds(row, ML_CHUNK), hs]
    ones_row = (lax.broadcasted_iota(jnp.int32, (ML_DH, ML_CHUNK), 0) == 0).astype(BF16)
    vt_ext = jnp.concatenate([v.T, ones_row], axis=0)

    cnt = cnt_ref[n]
    lhs = jnp.concatenate([ks.astype(BF16), cnt.astype(BF16)], axis=0)
    sr = _dot_nt(lhs, q)
    pt = jnp.exp(jnp.where(mask, c_b - big_m, -jnp.inf))
    st = (sr[:ML_CHUNK] * pt).astype(BF16)
    kw = (ks * jnp.exp(c_b - m_tot)).astype(BF16)
    ho = _dot(vt_ext, jnp.concatenate([st, kw], axis=-1))
    ht = ho[:, :ML_CHUNK] + jnp.exp(m - big_m) * sr[ML_CHUNK:]
    den = ht[ML_DH:ML_DH + 1]
    scale = 1.0 / jnp.maximum(jnp.abs(den), jnp.exp(-(b_r + big_m)))
    o_ref[0, pl.ds(row, ML_CHUNK), hs] = (ht[:ML_DH] * scale).T.astype(o_ref.dtype)
    cnt_ref[n] = jnp.exp(m - m_tot) * cnt + ho[:, ML_CHUNK:]
    m_ref[n] = g_tot + m_tot


def _mlstm_kernel(qf, kf, vf, grf, qb, kb, vb, grb, bias_r, eye_ref, tri_pre, tri_suf, of, ob,
                  cnt_ref, m_ref, b_ref, cmax_ref, ctot_ref, gtot_ref, ccol_ref):
    c = pl.program_id(1)

    @pl.when(c == 0)
    def _():
        cnt_ref[...] = jnp.zeros_like(cnt_ref)
        m_ref[...] = jnp.zeros_like(m_ref)

    stats = (b_ref, cmax_ref, ctot_ref, gtot_ref, ccol_ref)
    _mlstm_prologue(grf, grb, bias_r, eye_ref[...], tri_pre[...], tri_suf[...], *stats)
    n_chunks = ML_SEQ_TILE // ML_CHUNK

    def body(j, carry):
        jb = n_chunks - 1 - j
        rf = pl.multiple_of(j * ML_CHUNK, ML_CHUNK)
        rb = pl.multiple_of(jb * ML_CHUNK, ML_CHUNK)
        for h in range(ML_HEADS):
            _mlstm_chunk(qf, kf, vf, of, cnt_ref, m_ref, h, rf, j, stats, True)
            _mlstm_chunk(qb, kb, vb, ob, cnt_ref, m_ref, ML_HEADS + h, rb, jb, stats, False)
        return carry

    lax.fori_loop(0, n_chunks, body, 0)


def _mlstm(q, k, v, gates, if_bias):
    bsz, s, _ = q.shape
    ts = ML_SEQ_TILE
    ns = s // ts
    n_chunks = ts // ML_CHUNK
    n_gate = 4 * ML_HEADS
    n_dir = 2 * ML_HEADS
    gates_r = gates.reshape(bsz, s // ML_CHUNK, ML_CHUNK, n_gate).transpose(0, 1, 3, 2)
    fwd = lambda b, c: (b, c, 0)
    bwd = lambda b, c: (b, ns - 1 - c, 0)
    fwd4 = lambda b, c: (b, c, 0, 0)
    bwd4 = lambda b, c: (b, ns - 1 - c, 0, 0)

    def seq_specs(imap, imap4):
        return [pl.BlockSpec((1, ts, ML_W), imap), pl.BlockSpec((1, ts, ML_W), imap),
                pl.BlockSpec((1, ts, ML_W), imap),
                pl.BlockSpec((1, n_chunks, n_gate, ML_CHUNK), imap4)]

    in_specs = seq_specs(fwd, fwd4) + seq_specs(bwd, bwd4) + [
        _const_spec((n_gate, ML_CHUNK))] + [_const_spec((ML_CHUNK, ML_CHUNK))] * 3
    idx = jnp.arange(ML_CHUNK)
    tri_pre = (idx[:, None] <= idx[None, :]).astype(BF16)
    tri_suf = (idx[:, None] >= idx[None, :]).astype(BF16)
    out_specs = [pl.BlockSpec((1, ts, ML_W), fwd), pl.BlockSpec((1, ts, ML_W), bwd)]
    out_shape = [jax.ShapeDtypeStruct((bsz, s, ML_W), BF16)] * 2
    stat = pltpu.VMEM((n_chunks, n_dir, ML_CHUNK), F32)
    return pl.pallas_call(
        _mlstm_kernel,
        grid=(bsz, ns),
        in_specs=in_specs,
        out_specs=out_specs,
        out_shape=out_shape,
        scratch_shapes=[pltpu.VMEM((n_dir, 2 * ML_DH, ML_DH), F32), pltpu.VMEM((n_dir, 1, 1), F32),
                        stat, stat, stat, stat, pltpu.VMEM((n_chunks, ML_CHUNK, n_dir), F32)],
        compiler_params=_cparams(("parallel", "arbitrary")),
        name="mlstm_scan",
    )(q, k, v, gates_r, q, k, v, gates_r,
      jnp.broadcast_to(if_bias[:, None], (n_gate, ML_CHUNK)), jnp.eye(ML_CHUNK, dtype=BF16),
      tri_pre, tri_suf)


def _odd_out_kernel(final, att, mgate, hf, hb, mo, lgate, ng, wa, wb, x, fg, o_ref):
    mla_out = (att[0].astype(F32) * _silu(mgate[0].astype(F32))).astype(BF16)
    hm = hf[0].astype(F32) + hb[0].astype(F32)
    mo_v, lg_v = mo[0].astype(F32), lgate[0].astype(F32)
    parts = []
    for h in range(ML_HEADS):
        hs = slice(h * ML_DH, (h + 1) * ML_DH)
        y = _rms(hm[:, hs], ng[...]) * _sigmoid(mo_v[:, hs])
        parts.append((y * _silu(lg_v[:, hs])).astype(BF16))
    ml_out = jnp.concatenate(parts, axis=-1)
    xn = x[0] + (_dot(mla_out, wa[...]) + _dot(ml_out, wb[...]))
    o_ref[0] = _rms(xn, fg[...]) if final else xn


def _odd_out(att, mla_gate, h_f, h_b, mo, ml_gate, norm_g, w_out, x, final_g, final):
    bsz, s, _ = x.shape
    tm = TOKEN_TILE
    cur = lambda b, i: (b, i, 0)
    half = pl.BlockSpec((1, tm, MLA_W), cur)
    in_specs = [half] * 6 + [
        _const_spec((1, ML_DH)), _const_spec((MLA_W, D_MODEL)), _const_spec((ML_W, D_MODEL)),
        pl.BlockSpec((1, tm, D_MODEL), cur), _const_spec((1, D_MODEL))]
    return pl.pallas_call(
        functools.partial(_odd_out_kernel, final),
        grid=(bsz, s // tm),
        in_specs=in_specs,
        out_specs=pl.BlockSpec((1, tm, D_MODEL), cur),
        out_shape=jax.ShapeDtypeStruct((bsz, s, D_MODEL), F32),
        compiler_params=_cparams(("parallel", "parallel")),
        name="odd_out",
    )(att, mla_gate, h_f, h_b, mo, ml_gate, norm_g[None, :],
      w_out[:MLA_W].astype(BF16), w_out[MLA_W:].astype(BF16), x, final_g[None, :])


def _col_split(w, sizes):
    out, off = [], 0
    for n in sizes:
        out.append(w[:, off:off + n].astype(BF16))
        off += n
    return out


def _even_layer(x, g, w_in, a_up, a_bias, gla_norm_g, pool_w, pool_scale, w_out):
    bsz, s, _ = x.shape
    sizes = (GLA_K_TOT, GLA_K_TOT, GLA_V_TOT, GLA_V_TOT, 2 * GLA_LR, POOL_W, POOL_W)
    dts = (F32, F32, BF16, BF16, BF16, F32, BF16)
    outs = _norm_proj(x.reshape(bsz * s, D_MODEL), g[None, :], _col_split(w_in, sizes), dts)
    q, k, v, gate, lr, pool_u, pool_gate = [o.reshape(bsz, s, -1) for o in outs]
    o_f, o_b = _gla(q, k, v, lr, a_up, a_bias)
    return _even_out(o_f, o_b, gate, gla_norm_g, pool_u, pool_gate, pool_w, pool_scale, w_out, x)


def _odd_layer(x, g, tabs, w_in, q_norm_g, q_up, kv_norm_g, kv_up, if_bias, ml_norm_g, w_out,
               final_g, final):
    bsz, s, _ = x.shape
    half = MLA_ROPE // 2
    off = MLA_Q_LORA + MLA_KV_LORA
    kr_w = w_in[:, off:off + MLA_ROPE]
    n_gate = 4 * ML_HEADS
    gate_off = off + MLA_ROPE + MLA_W + 4 * ML_W
    w_lat = jnp.concatenate(
        [w_in[:, :off + MLA_ROPE], -kr_w[:, half:], kr_w[:, :half], w_in[:, gate_off:gate_off + n_gate],
         jnp.zeros((D_MODEL, MLA_MISC_W - 2 * MLA_ROPE - n_gate), F32)], axis=-1).astype(BF16)
    wide = _col_split(w_in[:, off + MLA_ROPE:gate_off], (MLA_W, ML_W, ML_W, ML_W, ML_W))
    wide.append(w_in[:, gate_off + n_gate:].astype(BF16))
    dts = (F32, BF16, BF16, F32, BF16, BF16, BF16)
    outs = _norm_proj(x.reshape(bsz * s, D_MODEL), g[None, :], [w_lat] + wide, dts)
    lat, mla_gate, mq, mk, mv, mo, ml_gate = [o.reshape(bsz, s, -1) for o in outs]
    mif = lat[:, :, off + 2 * MLA_ROPE:off + 2 * MLA_ROPE + n_gate]
    wq, wkn, place, wvt = _mla_weights(q_up, kv_up)
    qa, ka, vt = _mla_qkv(lat, tabs, q_norm_g, kv_norm_g, wq, wkn, place, wvt)
    att = _mla_attn(qa, ka, vt)
    h_f, h_b = _mlstm(mq, mk, mv, mif, if_bias)
    return _odd_out(att, mla_gate, h_f, h_b, mo, ml_gate, ml_norm_g, w_out, x, final_g, final)


def _trunk(x, norm_g, final_norm_g, e_w_in, e_gla_a_up, e_gla_a_bias, e_gla_norm_g, e_pool_w,
           e_pool_scale, e_w_out, o_w_in, o_q_norm_g, o_q_up, o_kv_norm_g, o_kv_up, o_if_bias,
           o_mlstm_norm_g, o_w_out):
    depth = norm_g.shape[0]
    tabs = _rope_tables(x.shape[1])
    for layer in range(depth):
        i = layer // 2
        if layer % 2 == 0:
            x = _even_layer(x, norm_g[layer], e_w_in[i], e_gla_a_up[i], e_gla_a_bias[i],
                            e_gla_norm_g[i], e_pool_w[i], e_pool_scale[i], e_w_out[i])
        else:
            x = _odd_layer(x, norm_g[layer], tabs, o_w_in[i], o_q_norm_g[i], o_q_up[i],
                           o_kv_norm_g[i], o_kv_up[i], o_if_bias[i], o_mlstm_norm_g[i], o_w_out[i],
                           final_norm_g, layer == depth - 1)
    return x


def kernel(x_prompt, x_sample, norm_g, final_norm_g, e_w_in, e_gla_a_up, e_gla_a_bias, e_gla_norm_g,
           e_pool_w, e_pool_scale, e_w_out, o_w_in, o_q_norm_g, o_q_up, o_kv_norm_g, o_kv_up,
           o_if_bias, o_mlstm_norm_g, o_w_out):
    params = (norm_g, final_norm_g, e_w_in, e_gla_a_up, e_gla_a_bias, e_gla_norm_g, e_pool_w,
              e_pool_scale, e_w_out, o_w_in, o_q_norm_g, o_q_up, o_kv_norm_g, o_kv_up, o_if_bias,
              o_mlstm_norm_g, o_w_out)
    return (_trunk(x_prompt, *params), _trunk(x_sample, *params))
```

```python
import functools
import math

import jax
import jax.numpy as jnp
from jax import lax
from jax.experimental import pallas as pl
from jax.experimental.pallas import tpu as pltpu

F32 = jnp.float32
BF16 = jnp.bfloat16

D_MODEL = 1024
NORM_EPS = 1e-6
CHUNK = 64

GLA_HEADS = 4
GLA_DK = 128
GLA_DV = 256
GLA_LR = 16
GLA_GATE_NORM = 16.0
GLA_K_TOT = GLA_HEADS * GLA_DK
GLA_V_TOT = GLA_HEADS * GLA_DV
GLA_BLOCK = 2 * CHUNK

POOL_GROUPS = 4
POOL_WINDOWS = (2, 4, 8, 16)
POOL_DG = 128
POOL_W = POOL_GROUPS * POOL_DG
POOL_HALO = 8

MLA_HEADS = 8
MLA_NOPE = 64
MLA_ROPE = 32
MLA_DV = 64
MLA_Q_LORA = 384
MLA_KV_LORA = 256
MLA_W = MLA_HEADS * MLA_DV
MLA_HEAD_PAD = 128
MLA_VT_ROWS = 80
MLA_MISC_W = 128
MLA_LAT_W = MLA_Q_LORA + MLA_KV_LORA + MLA_MISC_W
ROPE_THETA = 10000.0

ML_HEADS = 4
ML_DH = 128
ML_W = ML_HEADS * ML_DH
ML_CHUNK = 128

VMEM_LIMIT_BYTES = 56 * 1024 * 1024

TOKEN_TILE = 512
SEQ_TILE = 512
ML_SEQ_TILE = 1024
ATT_TQ = 1024
ATT_TQ_SUB = 512
ATT_TK = 512
ATT_UNROLL = 4


def _cparams(sem):
    return pltpu.CompilerParams(dimension_semantics=sem, vmem_limit_bytes=VMEM_LIMIT_BYTES)


def _const_spec(shape):
    nd = len(shape)
    return pl.BlockSpec(shape, lambda *_: (0,) * nd)


def _dot(a, b):
    return jnp.dot(a, b, preferred_element_type=F32)


def _dot_nt(a, b):
    return lax.dot_general(a, b, (((1,), (1,)), ((), ())), preferred_element_type=F32)


def _dot_tn(a, b):
    return lax.dot_general(a, b, (((0,), (0,)), ((), ())), preferred_element_type=F32)


def _sigmoid(x):
    return 1.0 / (1.0 + jnp.exp(-x))


def _silu(x):
    return x * _sigmoid(x)


def _log_sigmoid(x):
    return jnp.minimum(x, 0.0) - jnp.log(1.0 + jnp.exp(-jnp.abs(x)))


def _rms(x, g):
    return x * lax.rsqrt(jnp.mean(x * x, axis=-1, keepdims=True) + NORM_EPS) * g


def _split3(x):
    hi = x.astype(BF16)
    r1 = x - hi.astype(F32)
    mid = r1.astype(BF16)
    lo = (r1 - mid.astype(F32)).astype(BF16)
    return hi, mid, lo


def _tri_left(tri, x):
    hi, mid, lo = _split3(x)
    return _dot(tri, hi) + _dot(tri, mid) + _dot(tri, lo)


def _tri_right(x, tri):
    hi, mid, lo = _split3(x)
    return _dot(hi, tri) + _dot(mid, tri) + _dot(lo, tri)


def _norm_proj_kernel(n_out, x_ref, g_ref, *refs):
    w_refs, o_refs = refs[:n_out], refs[n_out:]
    h = _rms(x_ref[...], g_ref[...]).astype(BF16)
    for w_ref, o_ref in zip(w_refs, o_refs):
        o_ref[...] = _dot(h, w_ref[...]).astype(o_ref.dtype)


def _norm_proj(x2d, g, weights, out_dtypes):
    t = x2d.shape[0]
    tm = TOKEN_TILE
    n_out = len(weights)
    in_specs = [pl.BlockSpec((tm, D_MODEL), lambda i: (i, 0)), _const_spec((1, D_MODEL))]
    in_specs += [_const_spec(w.shape) for w in weights]
    out_specs = [pl.BlockSpec((tm, w.shape[1]), lambda i: (i, 0)) for w in weights]
    out_shape = [jax.ShapeDtypeStruct((t, w.shape[1]), dt) for w, dt in zip(weights, out_dtypes)]
    return pl.pallas_call(
        functools.partial(_norm_proj_kernel, n_out),
        grid=(t // tm,),
        in_specs=in_specs,
        out_specs=out_specs,
        out_shape=out_shape,
        compiler_params=_cparams(("parallel",)),
        name="norm_proj",
    )(x2d, g, *weights)


def _gla_block(q_ref, k_ref, v_ref, o_ref, st_ref, st_idx, row, b_all, inclusive):
    ii = lax.broadcasted_iota(jnp.int32, (GLA_BLOCK, GLA_BLOCK), 0)
    jj = lax.broadcasted_iota(jnp.int32, (GLA_BLOCK, GLA_BLOCK), 1)
    rr = lax.broadcasted_iota(jnp.int32, (GLA_BLOCK, 1), 0)
    same = (ii >= CHUNK) == (jj >= CHUNK)
    if inclusive:
        intra, cross, far_rows = same & (jj <= ii), (ii >= CHUNK) & (jj < CHUNK), rr >= CHUNK
        edge_lo, edge_hi = CHUNK - 1, GLA_BLOCK - 1
    else:
        intra, cross, far_rows = same & (jj > ii), (ii < CHUNK) & (jj >= CHUNK), rr < CHUNK
        edge_lo, edge_hi = 0, CHUNK
    for h in range(GLA_HEADS):
        ks = slice(h * GLA_DK, (h + 1) * GLA_DK)
        vs = slice(h * GLA_DV, (h + 1) * GLA_DV)
        q = q_ref[0, pl.ds(row, GLA_BLOCK), ks] * (GLA_DK ** -0.5)
        k = k_ref[0, pl.ds(row, GLA_BLOCK), ks]
        v = v_ref[0, pl.ds(row, GLA_BLOCK), vs]
        b = b_all[:, ks]
        tot_lo, tot_hi = b[edge_lo:edge_lo + 1, :], b[edge_hi:edge_hi + 1, :]
        tot_near, tot_far = (tot_lo, tot_hi) if inclusive else (tot_hi, tot_lo)
        qe = q * jnp.exp(b)
        kd = (k * jnp.exp(-b)).astype(BF16)
        kdec = k * jnp.exp(jnp.where(rr < CHUNK, tot_lo, tot_hi) - b)
        pp = _dot_nt(qe.astype(BF16), jnp.concatenate([kd, kdec.astype(BF16)], axis=0))
        a = jnp.where(intra, pp[:, :GLA_BLOCK], jnp.where(cross, pp[:, GLA_BLOCK:], 0.0)).astype(BF16)
        qe_in = jnp.where(far_rows, qe * jnp.exp(tot_near), qe).astype(BF16)
        kdec_out = jnp.where(far_rows, kdec, kdec * jnp.exp(tot_far)).astype(BF16)
        st = st_ref[st_idx + h]
        o_ref[0, pl.ds(row, GLA_BLOCK), vs] = (_dot(a, v) + _dot_nt(qe_in, st.astype(BF16))).astype(o_ref.dtype)
        st_ref[st_idx + h] = st * jnp.exp(tot_near + tot_far) + _dot_tn(v, kdec_out)


def _gla_kernel(qf, kf, vf, lrf, qb, kb, vb, lrb, aupf, aupb, biasf, biasb, tril, triu,
                of, ob, st_ref, la_ref):
    c = pl.program_id(1)

    @pl.when(c == 0)
    def _():
        st_ref[...] = jnp.zeros_like(st_ref)

    inv = 1.0 / GLA_GATE_NORM
    la_ref[0] = _log_sigmoid(_dot(lrf[0], aupf[...]) + biasf[...]) * inv
    la_ref[1] = _log_sigmoid(_dot(lrb[0], aupb[...]) + biasb[...]) * inv
    n_blocks = SEQ_TILE // GLA_BLOCK

    def body(j, carry):
        rf = pl.multiple_of(j * GLA_BLOCK, GLA_BLOCK)
        rb = pl.multiple_of((n_blocks - 1 - j) * GLA_BLOCK, GLA_BLOCK)
        b_f = _tri_left(tril[...], la_ref[0, pl.ds(rf, GLA_BLOCK), :])
        b_b = _tri_left(triu[...], la_ref[1, pl.ds(rb, GLA_BLOCK), :])
        _gla_block(qf, kf, vf, of, st_ref, 0, rf, b_f, True)
        _gla_block(qb, kb, vb, ob, st_ref, GLA_HEADS, rb, b_b, False)
        return carry

    lax.fori_loop(0, n_blocks, body, 0)


def _gla(q, k, v, lr, a_up, a_bias):
    bsz, s, _ = q.shape
    ts = SEQ_TILE
    ns = s // ts
    fwd = lambda b, c: (b, c, 0)
    bwd = lambda b, c: (b, ns - 1 - c, 0)
    idx = jnp.arange(GLA_BLOCK)
    same_chunk = (idx[None, :] // CHUNK) == (idx[:, None] // CHUNK)
    tril = (same_chunk & (idx[None, :] <= idx[:, None])).astype(BF16)
    triu = (same_chunk & (idx[None, :] >= idx[:, None])).astype(BF16)

    def seq_specs(imap):
        return [pl.BlockSpec((1, ts, GLA_K_TOT), imap), pl.BlockSpec((1, ts, GLA_K_TOT), imap),
                pl.BlockSpec((1, ts, GLA_V_TOT), imap), pl.BlockSpec((1, ts, 2 * GLA_LR), imap)]

    in_specs = seq_specs(fwd) + seq_specs(bwd) + [
        _const_spec((2 * GLA_LR, GLA_K_TOT)), _const_spec((2 * GLA_LR, GLA_K_TOT)),
        _const_spec((1, GLA_K_TOT)), _const_spec((1, GLA_K_TOT)),
        _const_spec((GLA_BLOCK, GLA_BLOCK)), _const_spec((GLA_BLOCK, GLA_BLOCK))]
    out_specs = [pl.BlockSpec((1, ts, GLA_V_TOT), fwd), pl.BlockSpec((1, ts, GLA_V_TOT), bwd)]
    out_shape = [jax.ShapeDtypeStruct((bsz, s, GLA_V_TOT), BF16)] * 2
    zeros = jnp.zeros((GLA_LR, GLA_K_TOT), F32)
    aup_f = jnp.concatenate([a_up[0], zeros], axis=0).astype(BF16)
    aup_b = jnp.concatenate([zeros, a_up[1]], axis=0).astype(BF16)
    return pl.pallas_call(
        _gla_kernel,
        grid=(bsz, ns),
        in_specs=in_specs,
        out_specs=out_specs,
        out_shape=out_shape,
        scratch_shapes=[pltpu.VMEM((2 * GLA_HEADS, GLA_DV, GLA_DK), F32),
                        pltpu.VMEM((2, ts, GLA_K_TOT), F32)],
        compiler_params=_cparams(("parallel", "arbitrary")),
        name="gla_scan",
    )(q, k, v, lr, q, k, v, lr, aup_f, aup_b, a_bias[0:1], a_bias[1:2], tril, triu)


def _even_out_kernel(seq_len, of, ob, gate, ng, pu, pprev, pnext, pgate, pw, pscale, wa, wb, x, o_ref):
    tm = of.shape[1]
    i = pl.program_id(1)
    n_i = pl.num_programs(1)

    o = of[0].astype(F32) + ob[0].astype(F32)
    g_all = gate[0].astype(F32)
    parts = []
    for h in range(GLA_HEADS):
        vs = slice(h * GLA_DV, (h + 1) * GLA_DV)
        parts.append((_rms(o[:, vs], ng[...]) * _silu(g_all[:, vs])).astype(BF16))
    gla_out = jnp.concatenate(parts, axis=-1)

    u = pu[0]
    prev = jnp.where(i > 0, pprev[0], 0.0)
    nxt = jnp.where(i < n_i - 1, pnext[0], 0.0)
    ext = jnp.concatenate([prev, u, nxt], axis=0)
    n_ext = tm + 2 * POOL_HALO
    pos = i * tm + lax.broadcasted_iota(jnp.int32, (tm, 1), 0)
    mixed = []
    for gi, w in enumerate(POOL_WINDOWS):
        cs = slice(gi * POOL_DG, (gi + 1) * POOL_DG)
        a = ext[:, cs]
        span = 1
        while span < w:
            a = a + pltpu.roll(a, span, 0)
            span *= 2
        shift = w // 2 - 1
        if shift:
            a = pltpu.roll(a, n_ext - shift, 0)
        win = a[POOL_HALO:POOL_HALO + tm]
        lo = jnp.maximum(pos - w // 2, 0)
        hi = jnp.minimum(pos + w // 2, seq_len)
        pooled = win / (hi - lo).astype(F32) - u[:, cs]
        mixed.append(_dot(pooled.astype(BF16), pw[gi]))
    pool_out = (jnp.concatenate(mixed, axis=-1) * pscale[...] * _silu(pgate[0].astype(F32))).astype(BF16)

    y = _dot(gla_out, wa[...]) + _dot(pool_out, wb[...])
    o_ref[0] = x[0] + y


def _even_out(o_f, o_b, gate, norm_g, pool_u, pool_gate, pool_w, pool_scale, w_out, x):
    bsz, s, _ = x.shape
    tm = TOKEN_TILE
    nh = tm // POOL_HALO
    n_halo = s // POOL_HALO
    cur = lambda b, i: (b, i, 0)
    in_specs = [
        pl.BlockSpec((1, tm, GLA_V_TOT), cur), pl.BlockSpec((1, tm, GLA_V_TOT), cur),
        pl.BlockSpec((1, tm, GLA_V_TOT), cur), _const_spec((1, GLA_DV)),
        pl.BlockSpec((1, tm, POOL_W), cur),
        pl.BlockSpec((1, POOL_HALO, POOL_W), lambda b, i: (b, jnp.maximum(i * nh - 1, 0), 0)),
        pl.BlockSpec((1, POOL_HALO, POOL_W), lambda b, i: (b, jnp.minimum((i + 1) * nh, n_halo - 1), 0)),
        pl.BlockSpec((1, tm, POOL_W), cur),
        _const_spec((POOL_GROUPS, POOL_DG, POOL_DG)), _const_spec((1, POOL_W)),
        _const_spec((GLA_V_TOT, D_MODEL)), _const_spec((POOL_W, D_MODEL)),
        pl.BlockSpec((1, tm, D_MODEL), cur)]
    return pl.pallas_call(
        functools.partial(_even_out_kernel, s),
        grid=(bsz, s // tm),
        in_specs=in_specs,
        out_specs=pl.BlockSpec((1, tm, D_MODEL), cur),
        out_shape=jax.ShapeDtypeStruct((bsz, s, D_MODEL), F32),
        compiler_params=_cparams(("parallel", "parallel")),
        name="even_out",
    )(o_f, o_b, gate, norm_g[None, :], pool_u, pool_u, pool_u, pool_gate,
      pool_w.astype(BF16), pool_scale[None, :],
      w_out[:GLA_V_TOT].astype(BF16), w_out[GLA_V_TOT:].astype(BF16), x)


def _mla_qkv_kernel(lat, cos_h, sin_h, rope_r, qg, kvg, wq, wkn, place, wvt, q_ref, k_ref, vt_ref):
    qk_scale = math.log2(math.e) * (MLA_NOPE + MLA_ROPE) ** -0.5
    w_all = MLA_HEADS * MLA_HEAD_PAD
    x = lat[0]
    cq = x[:, :MLA_Q_LORA]
    ckv = x[:, MLA_Q_LORA:MLA_Q_LORA + MLA_KV_LORA]
    nq = _rms(cq, qg[...]).astype(BF16)
    qq = _dot(nq, wq[...])
    cos2 = jnp.concatenate([cos_h[...], cos_h[...]], axis=-1)
    sin2 = jnp.concatenate([sin_h[...], sin_h[...]], axis=-1)
    for p in range(MLA_HEADS // 2):
        sl = slice(2 * p * MLA_HEAD_PAD, (2 * p + 2) * MLA_HEAD_PAD)
        sr = slice(w_all + 2 * p * MLA_HEAD_PAD, w_all + (2 * p + 2) * MLA_HEAD_PAD)
        q_ref[0, :, sl] = ((qq[:, sl] * cos2 + qq[:, sr] * sin2) * qk_scale).astype(BF16)

    nkv = _rms(ckv, kvg[...]).astype(BF16)
    k_rope = (x[:, MLA_Q_LORA + MLA_KV_LORA:] * rope_r[...]).astype(BF16)
    k_ref[0] = (_dot(nkv, wkn[...]) + _dot(k_rope, place[...])).astype(BF16)

    vt = _dot_nt(wvt[...], nkv)
    ones_row = lax.broadcasted_iota(jnp.int32, (MLA_VT_ROWS, 1), 0) == MLA_DV
    for h in range(MLA_HEADS):
        vh = vt[h * MLA_VT_ROWS:(h + 1) * MLA_VT_ROWS]
        vt_ref[0, h, 0] = jnp.where(ones_row, 1.0, vh).astype(BF16)


def _mla_qkv(lat, tabs, q_norm_g, kv_norm_g, wq, wkn, place, wvt):
    bsz, s, _ = lat.shape
    tm = ATT_TK
    w_all = MLA_HEADS * MLA_HEAD_PAD
    cur = lambda b, i: (b, i, 0)
    tab = lambda b, i: (i, 0)
    cos_h, sin_h, rope_r = tabs
    in_specs = [
        pl.BlockSpec((1, tm, MLA_LAT_W), cur),
        pl.BlockSpec((tm, MLA_HEAD_PAD), tab), pl.BlockSpec((tm, MLA_HEAD_PAD), tab),
        pl.BlockSpec((tm, MLA_MISC_W), tab),
        _const_spec((1, MLA_Q_LORA)), _const_spec((1, MLA_KV_LORA)),
        _const_spec(wq.shape), _const_spec(wkn.shape), _const_spec(place.shape), _const_spec(wvt.shape)]
    out_specs = [
        pl.BlockSpec((1, tm, w_all), cur), pl.BlockSpec((1, tm, w_all), cur),
        pl.BlockSpec((1, MLA_HEADS, 1, MLA_VT_ROWS, tm), lambda b, i: (b, 0, i, 0, 0))]
    out_shape = [
        jax.ShapeDtypeStruct((bsz, s, w_all), BF16), jax.ShapeDtypeStruct((bsz, s, w_all), BF16),
        jax.ShapeDtypeStruct((bsz, MLA_HEADS, s // tm, MLA_VT_ROWS, tm), BF16)]
    return pl.pallas_call(
        _mla_qkv_kernel,
        grid=(bsz, s // tm),
        in_specs=in_specs,
        out_specs=out_specs,
        out_shape=out_shape,
        compiler_params=_cparams(("parallel", "parallel")),
        name="mla_qkv",
    )(lat, cos_h, sin_h, rope_r, q_norm_g[None, :], kv_norm_g[None, :], wq, wkn, place, wvt)


def _mla_weights(q_up, kv_up):
    dq = MLA_NOPE + MLA_ROPE
    half = MLA_ROPE // 2
    qh = q_up.reshape(MLA_Q_LORA, MLA_HEADS, dq)
    zeros = jnp.zeros((MLA_Q_LORA, MLA_HEADS, MLA_HEAD_PAD - dq), F32)
    main = jnp.concatenate([qh, zeros], axis=-1)
    x1 = qh[..., MLA_NOPE:MLA_NOPE + half]
    x2 = qh[..., MLA_NOPE + half:]
    rot = jnp.concatenate([jnp.zeros((MLA_Q_LORA, MLA_HEADS, MLA_NOPE), F32), -x2, x1, zeros], axis=-1)
    w_all = MLA_HEADS * MLA_HEAD_PAD
    wq = jnp.concatenate([main.reshape(MLA_Q_LORA, w_all), rot.reshape(MLA_Q_LORA, w_all)], axis=-1)

    kvh = kv_up.reshape(MLA_KV_LORA, MLA_HEADS, MLA_NOPE + MLA_DV)
    wkn = jnp.concatenate(
        [kvh[..., :MLA_NOPE], jnp.zeros((MLA_KV_LORA, MLA_HEADS, MLA_HEAD_PAD - MLA_NOPE), F32)],
        axis=-1).reshape(MLA_KV_LORA, w_all)
    wv = jnp.transpose(kvh[..., MLA_NOPE:], (1, 2, 0))
    wvt = jnp.concatenate(
        [wv, jnp.zeros((MLA_HEADS, MLA_VT_ROWS - MLA_DV, MLA_KV_LORA), F32)],
        axis=1).reshape(MLA_HEADS * MLA_VT_ROWS, MLA_KV_LORA)
    r = jnp.arange(MLA_MISC_W)
    cols = jnp.arange(w_all)
    place = ((r[:, None] < 2 * MLA_ROPE)
             & ((cols[None, :] % MLA_HEAD_PAD) == (MLA_NOPE + r[:, None] % MLA_ROPE))).astype(BF16)
    return wq.astype(BF16), wkn.astype(BF16), place, wvt.astype(BF16)


def _rope_tables(s):
    inv = ROPE_THETA ** (-jnp.arange(0, MLA_ROPE, 2, dtype=F32) / MLA_ROPE)
    ang = jnp.arange(s, dtype=F32)[:, None] * inv[None, :]
    cos, sin = jnp.cos(ang), jnp.sin(ang)
    pad = jnp.zeros((s, MLA_HEAD_PAD - MLA_NOPE - MLA_ROPE), F32)
    cos_h = jnp.concatenate([jnp.ones((s, MLA_NOPE), F32), cos, cos, pad], axis=-1)
    sin_h = jnp.concatenate([jnp.zeros((s, MLA_NOPE), F32), sin, sin, pad], axis=-1)
    rope_r = jnp.concatenate([cos, cos, sin, sin, jnp.zeros((s, MLA_MISC_W - 2 * MLA_ROPE), F32)], axis=-1)
    return cos_h, sin_h, rope_r


def _mla_attn_kernel(q_ref, k_ref, vt_ref, o_ref, s_ref):
    n_k = k_ref.shape[1] // ATT_TK
    for qs in range(ATT_TQ // ATT_TQ_SUB):
        rows = slice(qs * ATT_TQ_SUB, (qs + 1) * ATT_TQ_SUB)
        qts = (q_ref[0, rows, 0:MLA_HEAD_PAD], q_ref[0, rows, MLA_HEAD_PAD:2 * MLA_HEAD_PAD])

        def produce(kk, buf):
            r = pl.multiple_of(kk * ATT_TK, ATT_TK)
            tile_max = []
            for hh in range(2):
                kt = k_ref[0, pl.ds(r, ATT_TK), hh * MLA_HEAD_PAD:(hh + 1) * MLA_HEAD_PAD]
                st = _dot_nt(kt, qts[hh])
                s_ref[buf, hh] = st
                tile_max.append(jnp.max(st, axis=0, keepdims=True))
            return tuple(tile_max)

        def consume(kk, buf, tile_max, carry):
            out = []
            for hh in range(2):
                m, acc = carry[2 * hh], carry[2 * hh + 1]
                m_new = jnp.maximum(m, tile_max[hh])
                alpha = jnp.exp2(m - m_new)
                p = jnp.exp2(s_ref[buf, hh] - m_new).astype(BF16)
                acc = acc * alpha + _dot(vt_ref[0, hh, kk], p)
                out += [m_new, acc]
            return tuple(out)

        def run(k0, state, produce_next):
            carry, tmax = state[:4], state[4:]
            for t in range(ATT_UNROLL):
                last = t == ATT_UNROLL - 1
                nxt = produce(k0 + t + 1, (t + 1) % 2) if (produce_next or not last) else ()
                carry = consume(k0 + t, t % 2, tmax, carry)
                tmax = nxt
            return carry + tmax

        m0 = jnp.full((1, ATT_TQ_SUB), -jnp.inf, F32)
        a0 = jnp.zeros((MLA_VT_ROWS, ATT_TQ_SUB), F32)
        state = lax.fori_loop(0, n_k // ATT_UNROLL - 1, lambda j, st: run(ATT_UNROLL * j, st, True),
                              (m0, a0, m0, a0) + produce(0, 0))
        carry = run(n_k - ATT_UNROLL, state, False)
        outs = [carry[2 * hh + 1][:MLA_DV] / carry[2 * hh + 1][MLA_DV:MLA_DV + 1] for hh in range(2)]
        o_ref[0, rows, :] = jnp.concatenate(outs, axis=0).T.astype(o_ref.dtype)


def _mla_attn(q, k, vt):
    bsz, s, _ = q.shape
    n_kt = s // ATT_TK
    assert s % (ATT_TK * ATT_UNROLL) == 0 and s % ATT_TQ == 0 and ATT_UNROLL % 2 == 0
    return pl.pallas_call(
        _mla_attn_kernel,
        grid=(bsz, MLA_HEADS // 2, s // ATT_TQ),
        in_specs=[
            pl.BlockSpec((1, ATT_TQ, 2 * MLA_HEAD_PAD), lambda b, h, i: (b, i, h)),
            pl.BlockSpec((1, s, 2 * MLA_HEAD_PAD), lambda b, h, i: (b, 0, h)),
            pl.BlockSpec((1, 2, n_kt, MLA_VT_ROWS, ATT_TK), lambda b, h, i: (b, h, 0, 0, 0))],
        out_specs=pl.BlockSpec((1, ATT_TQ, 2 * MLA_DV), lambda b, h, i: (b, i, h)),
        out_shape=jax.ShapeDtypeStruct((bsz, s, MLA_W), BF16),
        scratch_shapes=[pltpu.VMEM((2, 2, ATT_TK, ATT_TQ_SUB), F32)],
        compiler_params=_cparams(("parallel", "parallel", "arbitrary")),
        name="mla_attn",
    )(q, k, vt)


def _lane_scan(x, op, identity, reverse):
    lane = lax.broadcasted_iota(jnp.int32, x.shape, 1)
    step = 1
    while step < ML_CHUNK:
        if reverse:
            shifted = jnp.where(lane < ML_CHUNK - step, pltpu.roll(x, ML_CHUNK - step, 1), identity)
        else:
            shifted = jnp.where(lane >= step, pltpu.roll(x, step, 1), identity)
        x = op(x, shifted)
        step *= 2
    return x


def _mlstm_prologue(grf, grb, bias_r, eye, tri_pre, tri_suf, b_ref, cmax_ref, ctot_ref, gtot_ref, ccol_ref):
    n_chunks = ML_SEQ_TILE // ML_CHUNK
    n_dir = 2 * ML_HEADS
    rows16 = lax.broadcasted_iota(jnp.int32, (1, 2 * n_dir, 1), 1)
    g = jnp.where((rows16 % n_dir) < ML_HEADS, grf[0], grb[0]) + bias_r[...][None]
    li = g[:, :n_dir].reshape(n_chunks * n_dir, ML_CHUNK)
    lf = _log_sigmoid(g[:, n_dir:]).reshape(n_chunks * n_dir, ML_CHUNK)
    is_fwd = (lax.broadcasted_iota(jnp.int32, li.shape, 0) % n_dir) < ML_HEADS
    lane = lax.broadcasted_iota(jnp.int32, li.shape, 1)
    b = jnp.where(is_fwd, _tri_right(lf, tri_pre), _tri_right(lf, tri_suf))
    c = li - b
    c_next = jnp.where(lane < ML_CHUNK - 1, pltpu.roll(c, ML_CHUNK - 1, 1), -jnp.inf)
    cmax = jnp.where(is_fwd, _lane_scan(c, jnp.maximum, -jnp.inf, False),
                     _lane_scan(c_next, jnp.maximum, -jnp.inf, True))
    shape3 = (n_chunks, n_dir, ML_CHUNK)
    b_ref[...] = b.reshape(shape3)
    cmax_ref[...] = cmax.reshape(shape3)
    ctot_ref[...] = jnp.broadcast_to(jnp.max(c, axis=-1, keepdims=True), c.shape).reshape(shape3)
    gtot_ref[...] = jnp.broadcast_to(jnp.sum(lf, axis=-1, keepdims=True), c.shape).reshape(shape3)
    hi, mid, lo = _split3(c)
    for jc in range(n_chunks):
        rs = slice(jc * n_dir, (jc + 1) * n_dir)
        ccol_ref[jc] = _dot_nt(eye, hi[rs]) + _dot_nt(eye, mid[rs]) + _dot_nt(eye, lo[rs])


def _mlstm_chunk(q_ref, k_ref, v_ref, o_ref, cnt_ref, m_ref, n, row, jc, stats, inclusive):
    b_ref, cmax_ref, ctot_ref, gtot_ref, ccol_ref = stats
    h = n % ML_HEADS
    hs = slice(h * ML_DH, (h + 1) * ML_DH)
    jj = lax.broadcasted_iota(jnp.int32, (ML_CHUNK, ML_CHUNK), 0)
    ii = lax.broadcasted_iota(jnp.int32, (ML_CHUNK, ML_CHUNK), 1)
    mask = (jj <= ii) if inclusive else (jj > ii)

    m = m_ref[n]
    b_r = b_ref[jc, n:n + 1, :]
    big_m = jnp.maximum(m, cmax_ref[jc, n:n + 1, :])
    m_tot = jnp.maximum(m, ctot_ref[jc, n:n + 1, 0:1])
    g_tot = gtot_ref[jc, n:n + 1, 0:1]
    c_b = jnp.broadcast_to(ccol_ref[jc, :, n:n + 1], (ML_CHUNK, ML_CHUNK))

    q = q_ref[0, pl.ds(row, ML_CHUNK), hs]
    ks = k_ref[0, pl.ds(row, ML_CHUNK), hs] * (ML_DH ** -0.5)
    v = v_ref[0, pl.ds(row, ML_CHUNK), hs]
    ones_row = (lax.broadcasted_iota(jnp.int32, (ML_DH, ML_CHUNK), 0) == 0).astype(BF16)
    vt_ext = jnp.concatenate([v.T, ones_row], axis=0)

    cnt = cnt_ref[n]
    lhs = jnp.concatenate([ks.astype(BF16), cnt.astype(BF16)], axis=0)
    sr = _dot_nt(lhs, q)
    pt = jnp.exp(jnp.where(mask, c_b - big_m, -jnp.inf))
    st = (sr[:ML_CHUNK] * pt).astype(BF16)
    kw = (ks * jnp.exp(c_b - m_tot)).astype(BF16)
    ho = _dot(vt_ext, jnp.concatenate([st, kw], axis=-1))
    ht = ho[:, :ML_CHUNK] + jnp.exp(m - big_m) * sr[ML_CHUNK:]
    den = ht[ML_DH:ML_DH + 1]
    scale = 1.0 / jnp.maximum(jnp.abs(den), jnp.exp(-(b_r + big_m)))
    o_ref[0, pl.ds(row, ML_CHUNK), hs] = (ht[:ML_DH] * scale).T.astype(o_ref.dtype)
    cnt_ref[n] = jnp.exp(m - m_tot) * cnt + ho[:, ML_CHUNK:]
    m_ref[n] = g_tot + m_tot


def _mlstm_kernel(qf, kf, vf, grf, qb, kb, vb, grb, bias_r, eye_ref, tri_pre, tri_suf, of, ob,
                  cnt_ref, m_ref, b_ref, cmax_ref, ctot_ref, gtot_ref, ccol_ref):
    c = pl.program_id(1)

    @pl.when(c == 0)
    def _():
        cnt_ref[...] = jnp.zeros_like(cnt_ref)
        m_ref[...] = jnp.zeros_like(m_ref)

    stats = (b_ref, cmax_ref, ctot_ref, gtot_ref, ccol_ref)
    _mlstm_prologue(grf, grb, bias_r, eye_ref[...], tri_pre[...], tri_suf[...], *stats)
    n_chunks = ML_SEQ_TILE // ML_CHUNK

    def body(j, carry):
        jb = n_chunks - 1 - j
        rf = pl.multiple_of(j * ML_CHUNK, ML_CHUNK)
        rb = pl.multiple_of(jb * ML_CHUNK, ML_CHUNK)
        for h in range(ML_HEADS):
            _mlstm_chunk(qf, kf, vf, of, cnt_ref, m_ref, h, rf, j, stats, True)
            _mlstm_chunk(qb, kb, vb, ob, cnt_ref, m_ref, ML_HEADS + h, rb, jb, stats, False)
        return carry

    lax.fori_loop(0, n_chunks, body, 0)


def _mlstm(q, k, v, gates, if_bias):
    bsz, s, _ = q.shape
    ts = ML_SEQ_TILE
    ns = s // ts
    n_chunks = ts // ML_CHUNK
    n_gate = 4 * ML_HEADS
    n_dir = 2 * ML_HEADS
    gates_r = gates.reshape(bsz, s // ML_CHUNK, ML_CHUNK, n_gate).transpose(0, 1, 3, 2)
    fwd = lambda b, c: (b, c, 0)
    bwd = lambda b, c: (b, ns - 1 - c, 0)
    fwd4 = lambda b, c: (b, c, 0, 0)
    bwd4 = lambda b, c: (b, ns - 1 - c, 0, 0)

    def seq_specs(imap, imap4):
        return [pl.BlockSpec((1, ts, ML_W), imap), pl.BlockSpec((1, ts, ML_W), imap),
                pl.BlockSpec((1, ts, ML_W), imap),
                pl.BlockSpec((1, n_chunks, n_gate, ML_CHUNK), imap4)]

    in_specs = seq_specs(fwd, fwd4) + seq_specs(bwd, bwd4) + [
        _const_spec((n_gate, ML_CHUNK))] + [_const_spec((ML_CHUNK, ML_CHUNK))] * 3
    idx = jnp.arange(ML_CHUNK)
    tri_pre = (idx[:, None] <= idx[None, :]).astype(BF16)
    tri_suf = (idx[:, None] >= idx[None, :]).astype(BF16)
    out_specs = [pl.BlockSpec((1, ts, ML_W), fwd), pl.BlockSpec((1, ts, ML_W), bwd)]
    out_shape = [jax.ShapeDtypeStruct((bsz, s, ML_W), BF16)] * 2
    stat = pltpu.VMEM((n_chunks, n_dir, ML_CHUNK), F32)
    return pl.pallas_call(
        _mlstm_kernel,
        grid=(bsz, ns),
        in_specs=in_specs,
        out_specs=out_specs,
        out_shape=out_shape,
        scratch_shapes=[pltpu.VMEM((n_dir, 2 * ML_DH, ML_DH), F32), pltpu.VMEM((n_dir, 1, 1), F32),
                        stat, stat, stat, stat, pltpu.VMEM((n_chunks, ML_CHUNK, n_dir), F32)],
        compiler_params=_cparams(("parallel", "arbitrary")),
        name="mlstm_scan",
    )(q, k, v, gates_r, q, k, v, gates_r,
      jnp.broadcast_to(if_bias[:, None], (n_gate, ML_CHUNK)), jnp.eye(ML_CHUNK, dtype=BF16),
      tri_pre, tri_suf)


def _odd_out_kernel(final, att, mgate, hf, hb, mo, lgate, ng, wa, wb, x, fg, o_ref):
    mla_out = (att[0].astype(F32) * _silu(mgate[0].astype(F32))).astype(BF16)
    hm = hf[0].astype(F32) + hb[0].astype(F32)
    mo_v, lg_v = mo[0].astype(F32), lgate[0].astype(F32)
    parts = []
    for h in range(ML_HEADS):
        hs = slice(h * ML_DH, (h + 1) * ML_DH)
        y = _rms(hm[:, hs], ng[...]) * _sigmoid(mo_v[:, hs])
        parts.append((y * _silu(lg_v[:, hs])).astype(BF16))
    ml_out = jnp.concatenate(parts, axis=-1)
    xn = x[0] + (_dot(mla_out, wa[...]) + _dot(ml_out, wb[...]))
    o_ref[0] = _rms(xn, fg[...]) if final else xn


def _odd_out(att, mla_gate, h_f, h_b, mo, ml_gate, norm_g, w_out, x, final_g, final):
    bsz, s, _ = x.shape
    tm = TOKEN_TILE
    cur = lambda b, i: (b, i, 0)
    half = pl.BlockSpec((1, tm, MLA_W), cur)
    in_specs = [half] * 6 + [
        _const_spec((1, ML_DH)), _const_spec((MLA_W, D_MODEL)), _const_spec((ML_W, D_MODEL)),
        pl.BlockSpec((1, tm, D_MODEL), cur), _const_spec((1, D_MODEL))]
    return pl.pallas_call(
        functools.partial(_odd_out_kernel, final),
        grid=(bsz, s // tm),
        in_specs=in_specs,
        out_specs=pl.BlockSpec((1, tm, D_MODEL), cur),
        out_shape=jax.ShapeDtypeStruct((bsz, s, D_MODEL), F32),
        compiler_params=_cparams(("parallel", "parallel")),
        name="odd_out",
    )(att, mla_gate, h_f, h_b, mo, ml_gate, norm_g[None, :],
      w_out[:MLA_W].astype(BF16), w_out[MLA_W:].astype(BF16), x, final_g[None, :])


def _col_split(w, sizes):
    out, off = [], 0
    for n in sizes:
        out.append(w[:, off:off + n].astype(BF16))
        off += n
    return out


def _even_layer(x, g, w_in, a_up, a_bias, gla_norm_g, pool_w, pool_scale, w_out):
    bsz, s, _ = x.shape
    sizes = (GLA_K_TOT, GLA_K_TOT, GLA_V_TOT, GLA_V_TOT, 2 * GLA_LR, POOL_W, POOL_W)
    dts = (F32, F32, BF16, BF16, BF16, F32, BF16)
    outs = _norm_proj(x.reshape(bsz * s, D_MODEL), g[None, :], _col_split(w_in, sizes), dts)
    q, k, v, gate, lr, pool_u, pool_gate = [o.reshape(bsz, s, -1) for o in outs]
    o_f, o_b = _gla(q, k, v, lr, a_up, a_bias)
    return _even_out(o_f, o_b, gate, gla_norm_g, pool_u, pool_gate, pool_w, pool_scale, w_out, x)


def _odd_layer(x, g, tabs, w_in, q_norm_g, q_up, kv_norm_g, kv_up, if_bias, ml_norm_g, w_out,
               final_g, final):
    bsz, s, _ = x.shape
    half = MLA_ROPE // 2
    off = MLA_Q_LORA + MLA_KV_LORA
    kr_w = w_in[:, off:off + MLA_ROPE]
    n_gate = 4 * ML_HEADS
    gate_off = off + MLA_ROPE + MLA_W + 4 * ML_W
    w_lat = jnp.concatenate(
        [w_in[:, :off + MLA_ROPE], -kr_w[:, half:], kr_w[:, :half], w_in[:, gate_off:gate_off + n_gate],
         jnp.zeros((D_MODEL, MLA_MISC_W - 2 * MLA_ROPE - n_gate), F32)], axis=-1).astype(BF16)
    wide = _col_split(w_in[:, off + MLA_ROPE:gate_off], (MLA_W, ML_W, ML_W, ML_W, ML_W))
    wide.append(w_in[:, gate_off + n_gate:].astype(BF16))
    dts = (F32, BF16, BF16, F32, BF16, BF16, BF16)
    outs = _norm_proj(x.reshape(bsz * s, D_MODEL), g[None, :], [w_lat] + wide, dts)
    lat, mla_gate, mq, mk, mv, mo, ml_gate = [o.reshape(bsz, s, -1) for o in outs]
    mif = lat[:, :, off + 2 * MLA_ROPE:off + 2 * MLA_ROPE + n_gate]
    wq, wkn, place, wvt = _mla_weights(q_up, kv_up)
    qa, ka, vt = _mla_qkv(lat, tabs, q_norm_g, kv_norm_g, wq, wkn, place, wvt)
    att = _mla_attn(qa, ka, vt)
    h_f, h_b = _mlstm(mq, mk, mv, mif, if_bias)
    return _odd_out(att, mla_gate, h_f, h_b, mo, ml_gate, ml_norm_g, w_out, x, final_g, final)


def _trunk(x, norm_g, final_norm_g, e_w_in, e_gla_a_up, e_gla_a_bias, e_gla_norm_g, e_pool_w,
           e_pool_scale, e_w_out, o_w_in, o_q_norm_g, o_q_up, o_kv_norm_g, o_kv_up, o_if_bias,
           o_mlstm_norm_g, o_w_out):
    depth = norm_g.shape[0]
    tabs = _rope_tables(x.shape[1])
    for layer in range(depth):
        i = layer // 2
        if layer % 2 == 0:
            x = _even_layer(x, norm_g[layer], e_w_in[i], e_gla_a_up[i], e_gla_a_bias[i],
                            e_gla_norm_g[i], e_pool_w[i], e_pool_scale[i], e_w_out[i])
        else:
            x = _odd_layer(x, norm_g[layer], tabs, o_w_in[i], o_q_norm_g[i], o_q_up[i],
                           o_kv_norm_g[i], o_kv_up[i], o_if_bias[i], o_mlstm_norm_g[i], o_w_out[i],
                           final_norm_g, layer == depth - 1)
    return x


def kernel(x_prompt, x_sample, norm_g, final_norm_g, e_w_in, e_gla_a_up, e_gla_a_bias, e_gla_norm_g,
           e_pool_w, e_pool_scale, e_w_out, o_w_in, o_q_norm_g, o_q_up, o_kv_norm_g, o_kv_up,
           o_if_bias, o_mlstm_norm_g, o_w_out):
    params = (norm_g, final_norm_g, e_w_in, e_gla_a_up, e_gla_a_bias, e_gla_norm_g, e_pool_w,
              e_pool_scale, e_w_out, o_w_in, o_q_norm_g, o_q_up, o_kv_norm_g, o_kv_up, o_if_bias,
              o_mlstm_norm_g, o_w_out)
    return (_trunk(x_prompt, *params), _trunk(x_sample, *params))
```

```python
import functools
import math

import jax
import jax.numpy as jnp
from jax import lax
from jax.experimental import pallas as pl
from jax.experimental.pallas import tpu as pltpu

F32 = jnp.float32
BF16 = jnp.bfloat16

D_MODEL = 1024
NORM_EPS = 1e-6
CHUNK = 64

GLA_HEADS = 4
GLA_DK = 128
GLA_DV = 256
GLA_LR = 16
GLA_GATE_NORM = 16.0
GLA_K_TOT = GLA_HEADS * GLA_DK
GLA_V_TOT = GLA_HEADS * GLA_DV
GLA_BLOCK = 2 * CHUNK

POOL_GROUPS = 4
POOL_WINDOWS = (2, 4, 8, 16)
POOL_DG = 128
POOL_W = POOL_GROUPS * POOL_DG
POOL_HALO = 8

MLA_HEADS = 8
MLA_NOPE = 64
MLA_ROPE = 32
MLA_DV = 64
MLA_Q_LORA = 384
MLA_KV_LORA = 256
MLA_W = MLA_HEADS * MLA_DV
MLA_HEAD_PAD = 128
MLA_VT_ROWS = 80
MLA_MISC_W = 128
MLA_LAT_W = MLA_Q_LORA + MLA_KV_LORA + MLA_MISC_W
ROPE_THETA = 10000.0

ML_HEADS = 4
ML_DH = 128
ML_W = ML_HEADS * ML_DH
ML_CHUNK = 128

VMEM_LIMIT_BYTES = 56 * 1024 * 1024

TOKEN_TILE = 512
SEQ_TILE = 512
ML_SEQ_TILE = 1024
ATT_TQ = 1024
ATT_TQ_SUB = 512
ATT_TK = 256
ATT_UNROLL = 8


def _cparams(sem):
    return pltpu.CompilerParams(dimension_semantics=sem, vmem_limit_bytes=VMEM_LIMIT_BYTES)


def _const_spec(shape):
    nd = len(shape)
    return pl.BlockSpec(shape, lambda *_: (0,) * nd)


def _dot(a, b):
    return jnp.dot(a, b, preferred_element_type=F32)


def _dot_nt(a, b):
    return lax.dot_general(a, b, (((1,), (1,)), ((), ())), preferred_element_type=F32)


def _dot_tn(a, b):
    return lax.dot_general(a, b, (((0,), (0,)), ((), ())), preferred_element_type=F32)


def _sigmoid(x):
    return 1.0 / (1.0 + jnp.exp(-x))


def _silu(x):
    return x * _sigmoid(x)


def _log_sigmoid(x):
    return jnp.minimum(x, 0.0) - jnp.log(1.0 + jnp.exp(-jnp.abs(x)))


def _rms(x, g):
    return x * lax.rsqrt(jnp.mean(x * x, axis=-1, keepdims=True) + NORM_EPS) * g


def _split3(x):
    hi = x.astype(BF16)
    r1 = x - hi.astype(F32)
    mid = r1.astype(BF16)
    lo = (r1 - mid.astype(F32)).astype(BF16)
    return hi, mid, lo


def _tri_left(tri, x):
    hi, mid, lo = _split3(x)
    return _dot(tri, hi) + _dot(tri, mid) + _dot(tri, lo)


def _tri_right(x, tri):
    hi, mid, lo = _split3(x)
    return _dot(hi, tri) + _dot(mid, tri) + _dot(lo, tri)


def _norm_proj_kernel(n_out, x_ref, g_ref, *refs):
    w_refs, o_refs = refs[:n_out], refs[n_out:]
    h = _rms(x_ref[...], g_ref[...]).astype(BF16)
    for w_ref, o_ref in zip(w_refs, o_refs):
        o_ref[...] = _dot(h, w_ref[...]).astype(o_ref.dtype)


def _norm_proj(x2d, g, weights, out_dtypes):
    t = x2d.shape[0]
    tm = TOKEN_TILE
    n_out = len(weights)
    in_specs = [pl.BlockSpec((tm, D_MODEL), lambda i: (i, 0)), _const_spec((1, D_MODEL))]
    in_specs += [_const_spec(w.shape) for w in weights]
    out_specs = [pl.BlockSpec((tm, w.shape[1]), lambda i: (i, 0)) for w in weights]
    out_shape = [jax.ShapeDtypeStruct((t, w.shape[1]), dt) for w, dt in zip(weights, out_dtypes)]
    return pl.pallas_call(
        functools.partial(_norm_proj_kernel, n_out),
        grid=(t // tm,),
        in_specs=in_specs,
        out_specs=out_specs,
        out_shape=out_shape,
        compiler_params=_cparams(("parallel",)),
        name="norm_proj",
    )(x2d, g, *weights)


def _gla_block(q_ref, k_ref, v_ref, o_ref, st_ref, st_idx, row, b_all, inclusive):
    ii = lax.broadcasted_iota(jnp.int32, (GLA_BLOCK, GLA_BLOCK), 0)
    jj = lax.broadcasted_iota(jnp.int32, (GLA_BLOCK, GLA_BLOCK), 1)
    rr = lax.broadcasted_iota(jnp.int32, (GLA_BLOCK, 1), 0)
    same = (ii >= CHUNK) == (jj >= CHUNK)
    if inclusive:
        intra, cross, far_rows = same & (jj <= ii), (ii >= CHUNK) & (jj < CHUNK), rr >= CHUNK
        edge_lo, edge_hi = CHUNK - 1, GLA_BLOCK - 1
    else:
        intra, cross, far_rows = same & (jj > ii), (ii < CHUNK) & (jj >= CHUNK), rr < CHUNK
        edge_lo, edge_hi = 0, CHUNK
    for h in range(GLA_HEADS):
        ks = slice(h * GLA_DK, (h + 1) * GLA_DK)
        vs = slice(h * GLA_DV, (h + 1) * GLA_DV)
        q = q_ref[0, pl.ds(row, GLA_BLOCK), ks] * (GLA_DK ** -0.5)
        k = k_ref[0, pl.ds(row, GLA_BLOCK), ks]
        v = v_ref[0, pl.ds(row, GLA_BLOCK), vs]
        b = b_all[:, ks]
        tot_lo, tot_hi = b[edge_lo:edge_lo + 1, :], b[edge_hi:edge_hi + 1, :]
        tot_near, tot_far = (tot_lo, tot_hi) if inclusive else (tot_hi, tot_lo)
        qe = q * jnp.exp(b)
        kd = (k * jnp.exp(-b)).astype(BF16)
        kdec = k * jnp.exp(jnp.where(rr < CHUNK, tot_lo, tot_hi) - b)
        pp = _dot_nt(qe.astype(BF16), jnp.concatenate([kd, kdec.astype(BF16)], axis=0))
        a = jnp.where(intra, pp[:, :GLA_BLOCK], jnp.where(cross, pp[:, GLA_BLOCK:], 0.0)).astype(BF16)
        qe_in = jnp.where(far_rows, qe * jnp.exp(tot_near), qe).astype(BF16)
        kdec_out = jnp.where(far_rows, kdec, kdec * jnp.exp(tot_far)).astype(BF16)
        st = st_ref[st_idx + h]
        o_ref[0, pl.ds(row, GLA_BLOCK), vs] = (_dot(a, v) + _dot_nt(qe_in, st.astype(BF16))).astype(o_ref.dtype)
        st_ref[st_idx + h] = st * jnp.exp(tot_near + tot_far) + _dot_tn(v, kdec_out)


def _gla_kernel(qf, kf, vf, lrf, qb, kb, vb, lrb, aupf, aupb, biasf, biasb, tril, triu,
                of, ob, st_ref, la_ref):
    c = pl.program_id(1)

    @pl.when(c == 0)
    def _():
        st_ref[...] = jnp.zeros_like(st_ref)

    inv = 1.0 / GLA_GATE_NORM
    la_ref[0] = _log_sigmoid(_dot(lrf[0], aupf[...]) + biasf[...]) * inv
    la_ref[1] = _log_sigmoid(_dot(lrb[0], aupb[...]) + biasb[...]) * inv
    n_blocks = SEQ_TILE // GLA_BLOCK

    def body(j, carry):
        rf = pl.multiple_of(j * GLA_BLOCK, GLA_BLOCK)
        rb = pl.multiple_of((n_blocks - 1 - j) * GLA_BLOCK, GLA_BLOCK)
        b_f = _tri_left(tril[...], la_ref[0, pl.ds(rf, GLA_BLOCK), :])
        b_b = _tri_left(triu[...], la_ref[1, pl.ds(rb, GLA_BLOCK), :])
        _gla_block(qf, kf, vf, of, st_ref, 0, rf, b_f, True)
        _gla_block(qb, kb, vb, ob, st_ref, GLA_HEADS, rb, b_b, False)
        return carry

    lax.fori_loop(0, n_blocks, body, 0)


def _gla(q, k, v, lr, a_up, a_bias):
    bsz, s, _ = q.shape
    ts = SEQ_TILE
    ns = s // ts
    fwd = lambda b, c: (b, c, 0)
    bwd = lambda b, c: (b, ns - 1 - c, 0)
    idx = jnp.arange(GLA_BLOCK)
    same_chunk = (idx[None, :] // CHUNK) == (idx[:, None] // CHUNK)
    tril = (same_chunk & (idx[None, :] <= idx[:, None])).astype(BF16)
    triu = (same_chunk & (idx[None, :] >= idx[:, None])).astype(BF16)

    def seq_specs(imap):
        return [pl.BlockSpec((1, ts, GLA_K_TOT), imap), pl.BlockSpec((1, ts, GLA_K_TOT), imap),
                pl.BlockSpec((1, ts, GLA_V_TOT), imap), pl.BlockSpec((1, ts, 2 * GLA_LR), imap)]

    in_specs = seq_specs(fwd) + seq_specs(bwd) + [
        _const_spec((2 * GLA_LR, GLA_K_TOT)), _const_spec((2 * GLA_LR, GLA_K_TOT)),
        _const_spec((1, GLA_K_TOT)), _const_spec((1, GLA_K_TOT)),
        _const_spec((GLA_BLOCK, GLA_BLOCK)), _const_spec((GLA_BLOCK, GLA_BLOCK))]
    out_specs = [pl.BlockSpec((1, ts, GLA_V_TOT), fwd), pl.BlockSpec((1, ts, GLA_V_TOT), bwd)]
    out_shape = [jax.ShapeDtypeStruct((bsz, s, GLA_V_TOT), BF16)] * 2
    zeros = jnp.zeros((GLA_LR, GLA_K_TOT), F32)
    aup_f = jnp.concatenate([a_up[0], zeros], axis=0).astype(BF16)
    aup_b = jnp.concatenate([zeros, a_up[1]], axis=0).astype(BF16)
    return pl.pallas_call(
        _gla_kernel,
        grid=(bsz, ns),
        in_specs=in_specs,
        out_specs=out_specs,
        out_shape=out_shape,
        scratch_shapes=[pltpu.VMEM((2 * GLA_HEADS, GLA_DV, GLA_DK), F32),
                        pltpu.VMEM((2, ts, GLA_K_TOT), F32)],
        compiler_params=_cparams(("parallel", "arbitrary")),
        name="gla_scan",
    )(q, k, v, lr, q, k, v, lr, aup_f, aup_b, a_bias[0:1], a_bias[1:2], tril, triu)


def _even_out_kernel(seq_len, of, ob, gate, ng, pu, pprev, pnext, pgate, pw, pscale, wa, wb, x, o_ref):
    tm = of.shape[1]
    i = pl.program_id(1)
    n_i = pl.num_programs(1)

    o = of[0].astype(F32) + ob[0].astype(F32)
    g_all = gate[0].astype(F32)
    parts = []
    for h in range(GLA_HEADS):
        vs = slice(h * GLA_DV, (h + 1) * GLA_DV)
        parts.append((_rms(o[:, vs], ng[...]) * _silu(g_all[:, vs])).astype(BF16))
    gla_out = jnp.concatenate(parts, axis=-1)

    u = pu[0]
    prev = jnp.where(i > 0, pprev[0], 0.0)
    nxt = jnp.where(i < n_i - 1, pnext[0], 0.0)
    ext = jnp.concatenate([prev, u, nxt], axis=0)
    n_ext = tm + 2 * POOL_HALO
    pos = i * tm + lax.broadcasted_iota(jnp.int32, (tm, 1), 0)
    mixed = []
    for gi, w in enumerate(POOL_WINDOWS):
        cs = slice(gi * POOL_DG, (gi + 1) * POOL_DG)
        a = ext[:, cs]
        span = 1
        while span < w:
            a = a + pltpu.roll(a, span, 0)
            span *= 2
        shift = w // 2 - 1
        if shift:
            a = pltpu.roll(a, n_ext - shift, 0)
        win = a[POOL_HALO:POOL_HALO + tm]
        lo = jnp.maximum(pos - w // 2, 0)
        hi = jnp.minimum(pos + w // 2, seq_len)
        pooled = win / (hi - lo).astype(F32) - u[:, cs]
        mixed.append(_dot(pooled.astype(BF16), pw[gi]))
    pool_out = (jnp.concatenate(mixed, axis=-1) * pscale[...] * _silu(pgate[0].astype(F32))).astype(BF16)

    y = _dot(gla_out, wa[...]) + _dot(pool_out, wb[...])
    o_ref[0] = x[0] + y


def _even_out(o_f, o_b, gate, norm_g, pool_u, pool_gate, pool_w, pool_scale, w_out, x):
    bsz, s, _ = x.shape
    tm = TOKEN_TILE
    nh = tm // POOL_HALO
    n_halo = s // POOL_HALO
    cur = lambda b, i: (b, i, 0)
    in_specs = [
        pl.BlockSpec((1, tm, GLA_V_TOT), cur), pl.BlockSpec((1, tm, GLA_V_TOT), cur),
        pl.BlockSpec((1, tm, GLA_V_TOT), cur), _const_spec((1, GLA_DV)),
        pl.BlockSpec((1, tm, POOL_W), cur),
        pl.BlockSpec((1, POOL_HALO, POOL_W), lambda b, i: (b, jnp.maximum(i * nh - 1, 0), 0)),
        pl.BlockSpec((1, POOL_HALO, POOL_W), lambda b, i: (b, jnp.minimum((i + 1) * nh, n_halo - 1), 0)),
        pl.BlockSpec((1, tm, POOL_W), cur),
        _const_spec((POOL_GROUPS, POOL_DG, POOL_DG)), _const_spec((1, POOL_W)),
        _const_spec((GLA_V_TOT, D_MODEL)), _const_spec((POOL_W, D_MODEL)),
        pl.BlockSpec((1, tm, D_MODEL), cur)]
    return pl.pallas_call(
        functools.partial(_even_out_kernel, s),
        grid=(bsz, s // tm),
        in_specs=in_specs,
        out_specs=pl.BlockSpec((1, tm, D_MODEL), cur),
        out_shape=jax.ShapeDtypeStruct((bsz, s, D_MODEL), F32),
        compiler_params=_cparams(("parallel", "parallel")),
        name="even_out",
    )(o_f, o_b, gate, norm_g[None, :], pool_u, pool_u, pool_u, pool_gate,
      pool_w.astype(BF16), pool_scale[None, :],
      w_out[:GLA_V_TOT].astype(BF16), w_out[GLA_V_TOT:].astype(BF16), x)


def _mla_qkv_kernel(lat, cos_h, sin_h, rope_r, qg, kvg, wq, wkn, place, wvt, q_ref, k_ref, vt_ref):
    qk_scale = math.log2(math.e) * (MLA_NOPE + MLA_ROPE) ** -0.5
    w_all = MLA_HEADS * MLA_HEAD_PAD
    x = lat[0]
    cq = x[:, :MLA_Q_LORA]
    ckv = x[:, MLA_Q_LORA:MLA_Q_LORA + MLA_KV_LORA]
    nq = _rms(cq, qg[...]).astype(BF16)
    qq = _dot(nq, wq[...])
    cos2 = jnp.concatenate([cos_h[...], cos_h[...]], axis=-1)
    sin2 = jnp.concatenate([sin_h[...], sin_h[...]], axis=-1)
    for p in range(MLA_HEADS // 2):
        sl = slice(2 * p * MLA_HEAD_PAD, (2 * p + 2) * MLA_HEAD_PAD)
        sr = slice(w_all + 2 * p * MLA_HEAD_PAD, w_all + (2 * p + 2) * MLA_HEAD_PAD)
        q_ref[0, :, sl] = ((qq[:, sl] * cos2 + qq[:, sr] * sin2) * qk_scale).astype(BF16)

    nkv = _rms(ckv, kvg[...]).astype(BF16)
    k_rope = (x[:, MLA_Q_LORA + MLA_KV_LORA:] * rope_r[...]).astype(BF16)
    k_ref[0] = (_dot(nkv, wkn[...]) + _dot(k_rope, place[...])).astype(BF16)

    vt = _dot_nt(wvt[...], nkv)
    ones_row = lax.broadcasted_iota(jnp.int32, (MLA_VT_ROWS, 1), 0) == MLA_DV
    for h in range(MLA_HEADS):
        vh = vt[h * MLA_VT_ROWS:(h + 1) * MLA_VT_ROWS]
        vt_ref[0, h, 0] = jnp.where(ones_row, 1.0, vh).astype(BF16)


def _mla_qkv(lat, tabs, q_norm_g, kv_norm_g, wq, wkn, place, wvt):
    bsz, s, _ = lat.shape
    tm = ATT_TK
    w_all = MLA_HEADS * MLA_HEAD_PAD
    cur = lambda b, i: (b, i, 0)
    tab = lambda b, i: (i, 0)
    cos_h, sin_h, rope_r = tabs
    in_specs = [
        pl.BlockSpec((1, tm, MLA_LAT_W), cur),
        pl.BlockSpec((tm, MLA_HEAD_PAD), tab), pl.BlockSpec((tm, MLA_HEAD_PAD), tab),
        pl.BlockSpec((tm, MLA_MISC_W), tab),
        _const_spec((1, MLA_Q_LORA)), _const_spec((1, MLA_KV_LORA)),
        _const_spec(wq.shape), _const_spec(wkn.shape), _const_spec(place.shape), _const_spec(wvt.shape)]
    out_specs = [
        pl.BlockSpec((1, tm, w_all), cur), pl.BlockSpec((1, tm, w_all), cur),
        pl.BlockSpec((1, MLA_HEADS, 1, MLA_VT_ROWS, tm), lambda b, i: (b, 0, i, 0, 0))]
    out_shape = [
        jax.ShapeDtypeStruct((bsz, s, w_all), BF16), jax.ShapeDtypeStruct((bsz, s, w_all), BF16),
        jax.ShapeDtypeStruct((bsz, MLA_HEADS, s // tm, MLA_VT_ROWS, tm), BF16)]
    return pl.pallas_call(
        _mla_qkv_kernel,
        grid=(bsz, s // tm),
        in_specs=in_specs,
        out_specs=out_specs,
        out_shape=out_shape,
        compiler_params=_cparams(("parallel", "parallel")),
        name="mla_qkv",
    )(lat, cos_h, sin_h, rope_r, q_norm_g[None, :], kv_norm_g[None, :], wq, wkn, place, wvt)


def _mla_weights(q_up, kv_up):
    dq = MLA_NOPE + MLA_ROPE
    half = MLA_ROPE // 2
    qh = q_up.reshape(MLA_Q_LORA, MLA_HEADS, dq)
    zeros = jnp.zeros((MLA_Q_LORA, MLA_HEADS, MLA_HEAD_PAD - dq), F32)
    main = jnp.concatenate([qh, zeros], axis=-1)
    x1 = qh[..., MLA_NOPE:MLA_NOPE + half]
    x2 = qh[..., MLA_NOPE + half:]
    rot = jnp.concatenate([jnp.zeros((MLA_Q_LORA, MLA_HEADS, MLA_NOPE), F32), -x2, x1, zeros], axis=-1)
    w_all = MLA_HEADS * MLA_HEAD_PAD
    wq = jnp.concatenate([main.reshape(MLA_Q_LORA, w_all), rot.reshape(MLA_Q_LORA, w_all)], axis=-1)

    kvh = kv_up.reshape(MLA_KV_LORA, MLA_HEADS, MLA_NOPE + MLA_DV)
    wkn = jnp.concatenate(
        [kvh[..., :MLA_NOPE], jnp.zeros((MLA_KV_LORA, MLA_HEADS, MLA_HEAD_PAD - MLA_NOPE), F32)],
        axis=-1).reshape(MLA_KV_LORA, w_all)
    wv = jnp.transpose(kvh[..., MLA_NOPE:], (1, 2, 0))
    wvt = jnp.concatenate(
        [wv, jnp.zeros((MLA_HEADS, MLA_VT_ROWS - MLA_DV, MLA_KV_LORA), F32)],
        axis=1).reshape(MLA_HEADS * MLA_VT_ROWS, MLA_KV_LORA)
    r = jnp.arange(MLA_MISC_W)
    cols = jnp.arange(w_all)
    place = ((r[:, None] < 2 * MLA_ROPE)
             & ((cols[None, :] % MLA_HEAD_PAD) == (MLA_NOPE + r[:, None] % MLA_ROPE))).astype(BF16)
    return wq.astype(BF16), wkn.astype(BF16), place, wvt.astype(BF16)


def _rope_tables(s):
    inv = ROPE_THETA ** (-jnp.arange(0, MLA_ROPE, 2, dtype=F32) / MLA_ROPE)
    ang = jnp.arange(s, dtype=F32)[:, None] * inv[None, :]
    cos, sin = jnp.cos(ang), jnp.sin(ang)
    pad = jnp.zeros((s, MLA_HEAD_PAD - MLA_NOPE - MLA_ROPE), F32)
    cos_h = jnp.concatenate([jnp.ones((s, MLA_NOPE), F32), cos, cos, pad], axis=-1)
    sin_h = jnp.concatenate([jnp.zeros((s, MLA_NOPE), F32), sin, sin, pad], axis=-1)
    rope_r = jnp.concatenate([cos, cos, sin, sin, jnp.zeros((s, MLA_MISC_W - 2 * MLA_ROPE), F32)], axis=-1)
    return cos_h, sin_h, rope_r


def _mla_attn_kernel(q_ref, k_ref, vt_ref, o_ref, s_ref):
    n_k = k_ref.shape[1] // ATT_TK
    for qs in range(ATT_TQ // ATT_TQ_SUB):
        rows = slice(qs * ATT_TQ_SUB, (qs + 1) * ATT_TQ_SUB)
        qts = (q_ref[0, rows, 0:MLA_HEAD_PAD], q_ref[0, rows, MLA_HEAD_PAD:2 * MLA_HEAD_PAD])

        def produce(kk, buf):
            r = pl.multiple_of(kk * ATT_TK, ATT_TK)
            tile_max = []
            for hh in range(2):
                kt = k_ref[0, pl.ds(r, ATT_TK), hh * MLA_HEAD_PAD:(hh + 1) * MLA_HEAD_PAD]
                st = _dot_nt(kt, qts[hh])
                s_ref[buf, hh] = st
                tile_max.append(jnp.max(st, axis=0, keepdims=True))
            return tuple(tile_max)

        def consume(kk, buf, tile_max, carry):
            out = []
            for hh in range(2):
                m, acc = carry[2 * hh], carry[2 * hh + 1]
                m_new = jnp.maximum(m, tile_max[hh])
                alpha = jnp.exp2(m - m_new)
                p = jnp.exp2(s_ref[buf, hh] - m_new).astype(BF16)
                acc = acc * alpha + _dot(vt_ref[0, hh, kk], p)
                out += [m_new, acc]
            return tuple(out)

        def run(k0, state, produce_next):
            carry, tmax = state[:4], state[4:]
            for t in range(ATT_UNROLL):
                last = t == ATT_UNROLL - 1
                nxt = produce(k0 + t + 1, (t + 1) % 2) if (produce_next or not last) else ()
                carry = consume(k0 + t, t % 2, tmax, carry)
                tmax = nxt
            return carry + tmax

        m0 = jnp.full((1, ATT_TQ_SUB), -jnp.inf, F32)
        a0 = jnp.zeros((MLA_VT_ROWS, ATT_TQ_SUB), F32)
        state = lax.fori_loop(0, n_k // ATT_UNROLL - 1, lambda j, st: run(ATT_UNROLL * j, st, True),
                              (m0, a0, m0, a0) + produce(0, 0))
        carry = run(n_k - ATT_UNROLL, state, False)
        outs = [carry[2 * hh + 1][:MLA_DV] / carry[2 * hh + 1][MLA_DV:MLA_DV + 1] for hh in range(2)]
        o_ref[0, rows, :] = jnp.concatenate(outs, axis=0).T.astype(o_ref.dtype)


def _mla_attn(q, k, vt):
    bsz, s, _ = q.shape
    n_kt = s // ATT_TK
    assert s % (ATT_TK * ATT_UNROLL) == 0 and s % ATT_TQ == 0 and ATT_UNROLL % 2 == 0
    return pl.pallas_call(
        _mla_attn_kernel,
        grid=(bsz, MLA_HEADS // 2, s // ATT_TQ),
        in_specs=[
            pl.BlockSpec((1, ATT_TQ, 2 * MLA_HEAD_PAD), lambda b, h, i: (b, i, h)),
            pl.BlockSpec((1, s, 2 * MLA_HEAD_PAD), lambda b, h, i: (b, 0, h)),
            pl.BlockSpec((1, 2, n_kt, MLA_VT_ROWS, ATT_TK), lambda b, h, i: (b, h, 0, 0, 0))],
        out_specs=pl.BlockSpec((1, ATT_TQ, 2 * MLA_DV), lambda b, h, i: (b, i, h)),
        out_shape=jax.ShapeDtypeStruct((bsz, s, MLA_W), BF16),
        scratch_shapes=[pltpu.VMEM((2, 2, ATT_TK, ATT_TQ_SUB), F32)],
        compiler_params=_cparams(("parallel", "parallel", "arbitrary")),
        name="mla_attn",
    )(q, k, vt)


def _lane_scan(x, op, identity, reverse):
    lane = lax.broadcasted_iota(jnp.int32, x.shape, 1)
    step = 1
    while step < ML_CHUNK:
        if reverse:
            shifted = jnp.where(lane < ML_CHUNK - step, pltpu.roll(x, ML_CHUNK - step, 1), identity)
        else:
            shifted = jnp.where(lane >= step, pltpu.roll(x, step, 1), identity)
        x = op(x, shifted)
        step *= 2
    return x


def _mlstm_prologue(grf, grb, bias_r, eye, tri_pre, tri_suf, b_ref, cmax_ref, ctot_ref, gtot_ref, ccol_ref):
    n_chunks = ML_SEQ_TILE // ML_CHUNK
    n_dir = 2 * ML_HEADS
    rows16 = lax.broadcasted_iota(jnp.int32, (1, 2 * n_dir, 1), 1)
    g = jnp.where((rows16 % n_dir) < ML_HEADS, grf[0], grb[0]) + bias_r[...][None]
    li = g[:, :n_dir].reshape(n_chunks * n_dir, ML_CHUNK)
    lf = _log_sigmoid(g[:, n_dir:]).reshape(n_chunks * n_dir, ML_CHUNK)
    is_fwd = (lax.broadcasted_iota(jnp.int32, li.shape, 0) % n_dir) < ML_HEADS
    lane = lax.broadcasted_iota(jnp.int32, li.shape, 1)
    b = jnp.where(is_fwd, _tri_right(lf, tri_pre), _tri_right(lf, tri_suf))
    c = li - b
    c_next = jnp.where(lane < ML_CHUNK - 1, pltpu.roll(c, ML_CHUNK - 1, 1), -jnp.inf)
    cmax = jnp.where(is_fwd, _lane_scan(c, jnp.maximum, -jnp.inf, False),
                     _lane_scan(c_next, jnp.maximum, -jnp.inf, True))
    shape3 = (n_chunks, n_dir, ML_CHUNK)
    b_ref[...] = b.reshape(shape3)
    cmax_ref[...] = cmax.reshape(shape3)
    ctot_ref[...] = jnp.broadcast_to(jnp.max(c, axis=-1, keepdims=True), c.shape).reshape(shape3)
    gtot_ref[...] = jnp.broadcast_to(jnp.sum(lf, axis=-1, keepdims=True), c.shape).reshape(shape3)
    hi, mid, lo = _split3(c)
    for jc in range(n_chunks):
        rs = slice(jc * n_dir, (jc + 1) * n_dir)
        ccol_ref[jc] = _dot_nt(eye, hi[rs]) + _dot_nt(eye, mid[rs]) + _dot_nt(eye, lo[rs])


def _mlstm_chunk(q_ref, k_ref, v_ref, o_ref, cnt_ref, m_ref, n, row, jc, stats, inclusive):
    b_ref, cmax_ref, ctot_ref, gtot_ref, ccol_ref = stats
    h = n % ML_HEADS
    hs = slice(h * ML_DH, (h + 1) * ML_DH)
    jj = lax.broadcasted_iota(jnp.int32, (ML_CHUNK, ML_CHUNK), 0)
    ii = lax.broadcasted_iota(jnp.int32, (ML_CHUNK, ML_CHUNK), 1)
    mask = (jj <= ii) if inclusive else (jj > ii)

    m = m_ref[n]
    b_r = b_ref[jc, n:n + 1, :]
    big_m = jnp.maximum(m, cmax_ref[jc, n:n + 1, :])
    m_tot = jnp.maximum(m, ctot_ref[jc, n:n + 1, 0:1])
    g_tot = gtot_ref[jc, n:n + 1, 0:1]
    c_b = jnp.broadcast_to(ccol_ref[jc, :, n:n + 1], (ML_CHUNK, ML_CHUNK))

    q = q_ref[0, pl.ds(row, ML_CHUNK), hs]
    ks = k_ref[0, pl.ds(row, ML_CHUNK), hs] * (ML_DH ** -0.5)
    v = v_ref[0, pl.ds(row, ML_CHUNK), hs]
    ones_row = (lax.broadcasted_iota(jnp.int32, (ML_DH, ML_CHUNK), 0) == 0).astype(BF16)
    vt_ext = jnp.concatenate([v.T, ones_row], axis=0)

    cnt = cnt_ref[n]
    lhs = jnp.concatenate([ks.astype(BF16), cnt.astype(BF16)], axis=0)
    sr = _dot_nt(lhs, q)
    pt = jnp.exp(jnp.where(mask, c_b - big_m, -jnp.inf))
    st = (sr[:ML_CHUNK] * pt).astype(BF16)
    kw = (ks * jnp.exp(c_b - m_tot)).astype(BF16)
    ho = _dot(vt_ext, jnp.concatenate([st, kw], axis=-1))
    ht = ho[:, :ML_CHUNK] + jnp.exp(m - big_m) * sr[ML_CHUNK:]
    den = ht[ML_DH:ML_DH + 1]
    scale = 1.0 / jnp.maximum(jnp.abs(den), jnp.exp(-(b_r + big_m)))
    o_ref[0, pl.ds(row, ML_CHUNK), hs] = (ht[:ML_DH] * scale).T.astype(o_ref.dtype)
    cnt_ref[n] = jnp.exp(m - m_tot) * cnt + ho[:, ML_CHUNK:]
    m_ref[n] = g_tot + m_tot


def _mlstm_kernel(qf, kf, vf, grf, qb, kb, vb, grb, bias_r, eye_ref, tri_pre, tri_suf, of, ob,
                  cnt_ref, m_ref, b_ref, cmax_ref, ctot_ref, gtot_ref, ccol_ref):
    c = pl.program_id(1)

    @pl.when(c == 0)
    def _():
        cnt_ref[...] = jnp.zeros_like(cnt_ref)
        m_ref[...] = jnp.zeros_like(m_ref)

    stats = (b_ref, cmax_ref, ctot_ref, gtot_ref, ccol_ref)
    _mlstm_prologue(grf, grb, bias_r, eye_ref[...], tri_pre[...], tri_suf[...], *stats)
    n_chunks = ML_SEQ_TILE // ML_CHUNK

    def body(j, carry):
        jb = n_chunks - 1 - j
        rf = pl.multiple_of(j * ML_CHUNK, ML_CHUNK)
        rb = pl.multiple_of(jb * ML_CHUNK, ML_CHUNK)
        for h in range(ML_HEADS):
            _mlstm_chunk(qf, kf, vf, of, cnt_ref, m_ref, h, rf, j, stats, True)
            _mlstm_chunk(qb, kb, vb, ob, cnt_ref, m_ref, ML_HEADS + h, rb, jb, stats, False)
        return carry

    lax.fori_loop(0, n_chunks, body, 0)


def _mlstm(q, k, v, gates, if_bias):
    bsz, s, _ = q.shape
    ts = ML_SEQ_TILE
    ns = s // ts
    n_chunks = ts // ML_CHUNK
    n_gate = 4 * ML_HEADS
    n_dir = 2 * ML_HEADS
    gates_r = gates.reshape(bsz, s // ML_CHUNK, ML_CHUNK, n_gate).transpose(0, 1, 3, 2)
    fwd = lambda b, c: (b, c, 0)
    bwd = lambda b, c: (b, ns - 1 - c, 0)
    fwd4 = lambda b, c: (b, c, 0, 0)
    bwd4 = lambda b, c: (b, ns - 1 - c, 0, 0)

    def seq_specs(imap, imap4):
        return [pl.BlockSpec((1, ts, ML_W), imap), pl.BlockSpec((1, ts, ML_W), imap),
                pl.BlockSpec((1, ts, ML_W), imap),
                pl.BlockSpec((1, n_chunks, n_gate, ML_CHUNK), imap4)]

    in_specs = seq_specs(fwd, fwd4) + seq_specs(bwd, bwd4) + [
        _const_spec((n_gate, ML_CHUNK))] + [_const_spec((ML_CHUNK, ML_CHUNK))] * 3
    idx = jnp.arange(ML_CHUNK)
    tri_pre = (idx[:, None] <= idx[None, :]).astype(BF16)
    tri_suf = (idx[:, None] >= idx[None, :]).astype(BF16)
    out_specs = [pl.BlockSpec((1, ts, ML_W), fwd), pl.BlockSpec((1, ts, ML_W), bwd)]
    out_shape = [jax.ShapeDtypeStruct((bsz, s, ML_W), BF16)] * 2
    stat = pltpu.VMEM((n_chunks, n_dir, ML_CHUNK), F32)
    return pl.pallas_call(
        _mlstm_kernel,
        grid=(bsz, ns),
        in_specs=in_specs,
        out_specs=out_specs,
        out_shape=out_shape,
        scratch_shapes=[pltpu.VMEM((n_dir, 2 * ML_DH, ML_DH), F32), pltpu.VMEM((n_dir, 1, 1), F32),
                        stat, stat, stat, stat, pltpu.VMEM((n_chunks, ML_CHUNK, n_dir), F32)],
        compiler_params=_cparams(("parallel", "arbitrary")),
        name="mlstm_scan",
    )(q, k, v, gates_r, q, k, v, gates_r,
      jnp.broadcast_to(if_bias[:, None], (n_gate, ML_CHUNK)), jnp.eye(ML_CHUNK, dtype=BF16),
      tri_pre, tri_suf)


def _odd_out_kernel(final, att, mgate, hf, hb, mo, lgate, ng, wa, wb, x, fg, o_ref):
    mla_out = (att[0].astype(F32) * _silu(mgate[0].astype(F32))).astype(BF16)
    hm = hf[0].astype(F32) + hb[0].astype(F32)
    mo_v, lg_v = mo[0].astype(F32), lgate[0].astype(F32)
    parts = []
    for h in range(ML_HEADS):
        hs = slice(h * ML_DH, (h + 1) * ML_DH)
        y = _rms(hm[:, hs], ng[...]) * _sigmoid(mo_v[:, hs])
        parts.append((y * _silu(lg_v[:, hs])).astype(BF16))
    ml_out = jnp.concatenate(parts, axis=-1)
    xn = x[0] + (_dot(mla_out, wa[...]) + _dot(ml_out, wb[...]))
    o_ref[0] = _rms(xn, fg[...]) if final else xn


def _odd_out(att, mla_gate, h_f, h_b, mo, ml_gate, norm_g, w_out, x, final_g, final):
    bsz, s, _ = x.shape
    tm = TOKEN_TILE
    cur = lambda b, i: (b, i, 0)
    half = pl.BlockSpec((1, tm, MLA_W), cur)
    in_specs = [half] * 6 + [
        _const_spec((1, ML_DH)), _const_spec((MLA_W, D_MODEL)), _const_spec((ML_W, D_MODEL)),
        pl.BlockSpec((1, tm, D_MODEL), cur), _const_spec((1, D_MODEL))]
    return pl.pallas_call(
        functools.partial(_odd_out_kernel, final),
        grid=(bsz, s // tm),
        in_specs=in_specs,
        out_specs=pl.BlockSpec((1, tm, D_MODEL), cur),
        out_shape=jax.ShapeDtypeStruct((bsz, s, D_MODEL), F32),
        compiler_params=_cparams(("parallel", "parallel")),
        name="odd_out",
    )(att, mla_gate, h_f, h_b, mo, ml_gate, norm_g[None, :],
      w_out[:MLA_W].astype(BF16), w_out[MLA_W:].astype(BF16), x, final_g[None, :])


def _col_split(w, sizes):
    out, off = [], 0
    for n in sizes:
        out.append(w[:, off:off + n].astype(BF16))
        off += n
    return out


def _even_layer(x, g, w_in, a_up, a_bias, gla_norm_g, pool_w, pool_scale, w_out):
    bsz, s, _ = x.shape
    sizes = (GLA_K_TOT, GLA_K_TOT, GLA_V_TOT, GLA_V_TOT, 2 * GLA_LR, POOL_W, POOL_W)
    dts = (F32, F32, BF16, BF16, BF16, F32, BF16)
    outs = _norm_proj(x.reshape(bsz * s, D_MODEL), g[None, :], _col_split(w_in, sizes), dts)
    q, k, v, gate, lr, pool_u, pool_gate = [o.reshape(bsz, s, -1) for o in outs]
    o_f, o_b = _gla(q, k, v, lr, a_up, a_bias)
    return _even_out(o_f, o_b, gate, gla_norm_g, pool_u, pool_gate, pool_w, pool_scale, w_out, x)


def _odd_layer(x, g, tabs, w_in, q_norm_g, q_up, kv_norm_g, kv_up, if_bias, ml_norm_g, w_out,
               final_g, final):
    bsz, s, _ = x.shape
    half = MLA_ROPE // 2
    off = MLA_Q_LORA + MLA_KV_LORA
    kr_w = w_in[:, off:off + MLA_ROPE]
    n_gate = 4 * ML_HEADS
    gate_off = off + MLA_ROPE + MLA_W + 4 * ML_W
    w_lat = jnp.concatenate(
        [w_in[:, :off + MLA_ROPE], -kr_w[:, half:], kr_w[:, :half], w_in[:, gate_off:gate_off + n_gate],
         jnp.zeros((D_MODEL, MLA_MISC_W - 2 * MLA_ROPE - n_gate), F32)], axis=-1).astype(BF16)
    wide = _col_split(w_in[:, off + MLA_ROPE:gate_off], (MLA_W, ML_W, ML_W, ML_W, ML_W))
    wide.append(w_in[:, gate_off + n_gate:].astype(BF16))
    dts = (F32, BF16, BF16, F32, BF16, BF16, BF16)
    outs = _norm_proj(x.reshape(bsz * s, D_MODEL), g[None, :], [w_lat] + wide, dts)
    lat, mla_gate, mq, mk, mv, mo, ml_gate = [o.reshape(bsz, s, -1) for o in outs]
    mif = lat[:, :, off + 2 * MLA_ROPE:off + 2 * MLA_ROPE + n_gate]
    wq, wkn, place, wvt = _mla_weights(q_up, kv_up)
    qa, ka, vt = _mla_qkv(lat, tabs, q_norm_g, kv_norm_g, wq, wkn, place, wvt)
    att = _mla_attn(qa, ka, vt)
    h_f, h_b = _mlstm(mq, mk, mv, mif, if_bias)
    return _odd_out(att, mla_gate, h_f, h_b, mo, ml_gate, ml_norm_g, w_out, x, final_g, final)


def _trunk(x, norm_g, final_norm_g, e_w_in, e_gla_a_up, e_gla_a_bias, e_gla_norm_g, e_pool_w,
           e_pool_scale, e_w_out, o_w_in, o_q_norm_g, o_q_up, o_kv_norm_g, o_kv_up, o_if_bias,
           o_mlstm_norm_g, o_w_out):
    depth = norm_g.shape[0]
    tabs = _rope_tables(x.shape[1])
    for layer in range(depth):
        i = layer // 2
        if layer % 2 == 0:
            x = _even_layer(x, norm_g[layer], e_w_in[i], e_gla_a_up[i], e_gla_a_bias[i],
                            e_gla_norm_g[i], e_pool_w[i], e_pool_scale[i], e_w_out[i])
        else:
            x = _odd_layer(x, norm_g[layer], tabs, o_w_in[i], o_q_norm_g[i], o_q_up[i],
                           o_kv_norm_g[i], o_kv_up[i], o_if_bias[i], o_mlstm_norm_g[i], o_w_out[i],
                           final_norm_g, layer == depth - 1)
    return x


def kernel(x_prompt, x_sample, norm_g, final_norm_g, e_w_in, e_gla_a_up, e_gla_a_bias, e_gla_norm_g,
           e_pool_w, e_pool_scale, e_w_out, o_w_in, o_q_norm_g, o_q_up, o_kv_norm_g, o_kv_up,
           o_if_bias, o_mlstm_norm_g, o_w_out):
    params = (norm_g, final_norm_g, e_w_in, e_gla_a_up, e_gla_a_bias, e_gla_norm_g, e_pool_w,
              e_pool_scale, e_w_out, o_w_in, o_q_norm_g, o_q_up, o_kv_norm_g, o_kv_up, o_if_bias,
              o_mlstm_norm_g, o_w_out)
    return (_trunk(x_prompt, *params), _trunk(x_sample, *params))
```

```python
import functools
import math

import jax
import jax.numpy as jnp
from jax import lax
from jax.experimental import pallas as pl
from jax.experimental.pallas import tpu as pltpu

F32 = jnp.float32
BF16 = jnp.bfloat16

D_MODEL = 1024
NORM_EPS = 1e-6
CHUNK = 64

GLA_HEADS = 4
GLA_DK = 128
GLA_DV = 256
GLA_LR = 16
GLA_GATE_NORM = 16.0
GLA_K_TOT = GLA_HEADS * GLA_DK
GLA_V_TOT = GLA_HEADS * GLA_DV
GLA_BLOCK = 2 * CHUNK

POOL_GROUPS = 4
POOL_WINDOWS = (2, 4, 8, 16)
POOL_DG = 128
POOL_W = POOL_GROUPS * POOL_DG
POOL_HALO = 8

MLA_HEADS = 8
MLA_NOPE = 64
MLA_ROPE = 32
MLA_DV = 64
MLA_Q_LORA = 384
MLA_KV_LORA = 256
MLA_W = MLA_HEADS * MLA_DV
MLA_HEAD_PAD = 128
MLA_VT_ROWS = 80
MLA_MISC_W = 128
MLA_LAT_W = MLA_Q_LORA + MLA_KV_LORA + MLA_MISC_W
ROPE_THETA = 10000.0

ML_HEADS = 4
ML_DH = 128
ML_W = ML_HEADS * ML_DH
ML_CHUNK = 128

VMEM_LIMIT_BYTES = 56 * 1024 * 1024

TOKEN_TILE = 512
TAIL_TILE = 1024
SEQ_TILE = 1024
ML_SEQ_TILE = 1024
ATT_TQ = 1024
ATT_TQ_SUB = 512
ATT_TK = 256
ATT_UNROLL = 8


def _cparams(sem):
    return pltpu.CompilerParams(dimension_semantics=sem, vmem_limit_bytes=VMEM_LIMIT_BYTES)


def _const_spec(shape):
    nd = len(shape)
    return pl.BlockSpec(shape, lambda *_: (0,) * nd)


def _dot(a, b):
    return jnp.dot(a, b, preferred_element_type=F32)


def _dot_nt(a, b):
    return lax.dot_general(a, b, (((1,), (1,)), ((), ())), preferred_element_type=F32)


def _dot_tn(a, b):
    return lax.dot_general(a, b, (((0,), (0,)), ((), ())), preferred_element_type=F32)


def _sigmoid(x):
    return 1.0 / (1.0 + jnp.exp(-x))


def _silu(x):
    return x * _sigmoid(x)


def _log_sigmoid(x):
    return jnp.minimum(x, 0.0) - jnp.log(1.0 + jnp.exp(-jnp.abs(x)))


def _rms(x, g):
    return x * lax.rsqrt(jnp.mean(x * x, axis=-1, keepdims=True) + NORM_EPS) * g


def _split3(x):
    hi = x.astype(BF16)
    r1 = x - hi.astype(F32)
    mid = r1.astype(BF16)
    lo = (r1 - mid.astype(F32)).astype(BF16)
    return hi, mid, lo


def _tri_left(tri, x):
    hi, mid, lo = _split3(x)
    return _dot(tri, hi) + _dot(tri, mid) + _dot(tri, lo)


def _tri_right(x, tri):
    hi, mid, lo = _split3(x)
    return _dot(hi, tri) + _dot(mid, tri) + _dot(lo, tri)


def _norm_proj_kernel(n_out, x_ref, g_ref, *refs):
    w_refs, o_refs = refs[:n_out], refs[n_out:]
    h = _rms(x_ref[...], g_ref[...]).astype(BF16)
    for w_ref, o_ref in zip(w_refs, o_refs):
        o_ref[...] = _dot(h, w_ref[...]).astype(o_ref.dtype)


def _norm_proj(x2d, g, weights, out_dtypes):
    t = x2d.shape[0]
    tm = TOKEN_TILE
    n_out = len(weights)
    in_specs = [pl.BlockSpec((tm, D_MODEL), lambda i: (i, 0)), _const_spec((1, D_MODEL))]
    in_specs += [_const_spec(w.shape) for w in weights]
    out_specs = [pl.BlockSpec((tm, w.shape[1]), lambda i: (i, 0)) for w in weights]
    out_shape = [jax.ShapeDtypeStruct((t, w.shape[1]), dt) for w, dt in zip(weights, out_dtypes)]
    return pl.pallas_call(
        functools.partial(_norm_proj_kernel, n_out),
        grid=(t // tm,),
        in_specs=in_specs,
        out_specs=out_specs,
        out_shape=out_shape,
        compiler_params=_cparams(("parallel",)),
        name="norm_proj",
    )(x2d, g, *weights)


def _gla_block(q_ref, k_ref, v_ref, o_ref, st_ref, st_idx, row, b_all, inclusive):
    ii = lax.broadcasted_iota(jnp.int32, (GLA_BLOCK, GLA_BLOCK), 0)
    jj = lax.broadcasted_iota(jnp.int32, (GLA_BLOCK, GLA_BLOCK), 1)
    rr = lax.broadcasted_iota(jnp.int32, (GLA_BLOCK, 1), 0)
    same = (ii >= CHUNK) == (jj >= CHUNK)
    if inclusive:
        intra, cross, far_rows = same & (jj <= ii), (ii >= CHUNK) & (jj < CHUNK), rr >= CHUNK
        edge_lo, edge_hi = CHUNK - 1, GLA_BLOCK - 1
    else:
        intra, cross, far_rows = same & (jj > ii), (ii < CHUNK) & (jj >= CHUNK), rr < CHUNK
        edge_lo, edge_hi = 0, CHUNK
    for h in range(GLA_HEADS):
        ks = slice(h * GLA_DK, (h + 1) * GLA_DK)
        vs = slice(h * GLA_DV, (h + 1) * GLA_DV)
        q = q_ref[0, pl.ds(row, GLA_BLOCK), ks] * (GLA_DK ** -0.5)
        k = k_ref[0, pl.ds(row, GLA_BLOCK), ks]
        v = v_ref[0, pl.ds(row, GLA_BLOCK), vs]
        b = b_all[:, ks]
        tot_lo, tot_hi = b[edge_lo:edge_lo + 1, :], b[edge_hi:edge_hi + 1, :]
        tot_near, tot_far = (tot_lo, tot_hi) if inclusive else (tot_hi, tot_lo)
        qe = q * jnp.exp(b)
        kd = (k * jnp.exp(-b)).astype(BF16)
        kdec = k * jnp.exp(jnp.where(rr < CHUNK, tot_lo, tot_hi) - b)
        pp = _dot_nt(qe.astype(BF16), jnp.concatenate([kd, kdec.astype(BF16)], axis=0))
        a = jnp.where(intra, pp[:, :GLA_BLOCK], jnp.where(cross, pp[:, GLA_BLOCK:], 0.0)).astype(BF16)
        qe_in = jnp.where(far_rows, qe * jnp.exp(tot_near), qe).astype(BF16)
        kdec_out = jnp.where(far_rows, kdec, kdec * jnp.exp(tot_far)).astype(BF16)
        st = st_ref[st_idx + h]
        o_ref[0, pl.ds(row, GLA_BLOCK), vs] = (_dot(a, v) + _dot_nt(qe_in, st.astype(BF16))).astype(o_ref.dtype)
        st_ref[st_idx + h] = st * jnp.exp(tot_near + tot_far) + _dot_tn(v, kdec_out)


def _gla_kernel(qf, kf, vf, lrf, qb, kb, vb, lrb, aupf, aupb, biasf, biasb, tril, triu,
                of, ob, st_ref, la_ref):
    c = pl.program_id(1)

    @pl.when(c == 0)
    def _():
        st_ref[...] = jnp.zeros_like(st_ref)

    inv = 1.0 / GLA_GATE_NORM
    la_ref[0] = _log_sigmoid(_dot(lrf[0], aupf[...]) + biasf[...]) * inv
    la_ref[1] = _log_sigmoid(_dot(lrb[0], aupb[...]) + biasb[...]) * inv
    n_blocks = SEQ_TILE // GLA_BLOCK

    def body(j, carry):
        rf = pl.multiple_of(j * GLA_BLOCK, GLA_BLOCK)
        rb = pl.multiple_of((n_blocks - 1 - j) * GLA_BLOCK, GLA_BLOCK)
        b_f = _tri_left(tril[...], la_ref[0, pl.ds(rf, GLA_BLOCK), :])
        b_b = _tri_left(triu[...], la_ref[1, pl.ds(rb, GLA_BLOCK), :])
        _gla_block(qf, kf, vf, of, st_ref, 0, rf, b_f, True)
        _gla_block(qb, kb, vb, ob, st_ref, GLA_HEADS, rb, b_b, False)
        return carry

    lax.fori_loop(0, n_blocks, body, 0)


def _gla(q, k, v, lr, a_up, a_bias):
    bsz, s, _ = q.shape
    ts = SEQ_TILE
    ns = s // ts
    fwd = lambda b, c: (b, c, 0)
    bwd = lambda b, c: (b, ns - 1 - c, 0)
    idx = jnp.arange(GLA_BLOCK)
    same_chunk = (idx[None, :] // CHUNK) == (idx[:, None] // CHUNK)
    tril = (same_chunk & (idx[None, :] <= idx[:, None])).astype(BF16)
    triu = (same_chunk & (idx[None, :] >= idx[:, None])).astype(BF16)

    def seq_specs(imap):
        return [pl.BlockSpec((1, ts, GLA_K_TOT), imap), pl.BlockSpec((1, ts, GLA_K_TOT), imap),
                pl.BlockSpec((1, ts, GLA_V_TOT), imap), pl.BlockSpec((1, ts, 2 * GLA_LR), imap)]

    in_specs = seq_specs(fwd) + seq_specs(bwd) + [
        _const_spec((2 * GLA_LR, GLA_K_TOT)), _const_spec((2 * GLA_LR, GLA_K_TOT)),
        _const_spec((1, GLA_K_TOT)), _const_spec((1, GLA_K_TOT)),
        _const_spec((GLA_BLOCK, GLA_BLOCK)), _const_spec((GLA_BLOCK, GLA_BLOCK))]
    out_specs = [pl.BlockSpec((1, ts, GLA_V_TOT), fwd), pl.BlockSpec((1, ts, GLA_V_TOT), bwd)]
    out_shape = [jax.ShapeDtypeStruct((bsz, s, GLA_V_TOT), BF16)] * 2
    zeros = jnp.zeros((GLA_LR, GLA_K_TOT), F32)
    aup_f = jnp.concatenate([a_up[0], zeros], axis=0).astype(BF16)
    aup_b = jnp.concatenate([zeros, a_up[1]], axis=0).astype(BF16)
    return pl.pallas_call(
        _gla_kernel,
        grid=(bsz, ns),
        in_specs=in_specs,
        out_specs=out_specs,
        out_shape=out_shape,
        scratch_shapes=[pltpu.VMEM((2 * GLA_HEADS, GLA_DV, GLA_DK), F32),
                        pltpu.VMEM((2, ts, GLA_K_TOT), F32)],
        compiler_params=_cparams(("parallel", "arbitrary")),
        name="gla_scan",
    )(q, k, v, lr, q, k, v, lr, aup_f, aup_b, a_bias[0:1], a_bias[1:2], tril, triu)


def _even_out_kernel(seq_len, of, ob, gate, ng, pu, pprev, pnext, pgate, pw, pscale, wa, wb, x, o_ref):
    tm = of.shape[1]
    i = pl.program_id(1)
    n_i = pl.num_programs(1)

    o = of[0].astype(F32) + ob[0].astype(F32)
    g_all = gate[0].astype(F32)
    parts = []
    for h in range(GLA_HEADS):
        vs = slice(h * GLA_DV, (h + 1) * GLA_DV)
        parts.append((_rms(o[:, vs], ng[...]) * _silu(g_all[:, vs])).astype(BF16))
    gla_out = jnp.concatenate(parts, axis=-1)

    u = pu[0]
    prev = jnp.where(i > 0, pprev[0], 0.0)
    nxt = jnp.where(i < n_i - 1, pnext[0], 0.0)
    ext = jnp.concatenate([prev, u, nxt], axis=0)
    n_ext = tm + 2 * POOL_HALO
    pos = i * tm + lax.broadcasted_iota(jnp.int32, (tm, 1), 0)
    mixed = []
    for gi, w in enumerate(POOL_WINDOWS):
        cs = slice(gi * POOL_DG, (gi + 1) * POOL_DG)
        a = ext[:, cs]
        span = 1
        while span < w:
            a = a + pltpu.roll(a, span, 0)
            span *= 2
        shift = w // 2 - 1
        if shift:
            a = pltpu.roll(a, n_ext - shift, 0)
        win = a[POOL_HALO:POOL_HALO + tm]
        lo = jnp.maximum(pos - w // 2, 0)
        hi = jnp.minimum(pos + w // 2, seq_len)
        pooled = win / (hi - lo).astype(F32) - u[:, cs]
        mixed.append(_dot(pooled.astype(BF16), pw[gi]))
    pool_out = (jnp.concatenate(mixed, axis=-1) * pscale[...] * _silu(pgate[0].astype(F32))).astype(BF16)

    y = _dot(gla_out, wa[...]) + _dot(pool_out, wb[...])
    o_ref[0] = x[0] + y


def _even_out(o_f, o_b, gate, norm_g, pool_u, pool_gate, pool_w, pool_scale, w_out, x):
    bsz, s, _ = x.shape
    tm = TOKEN_TILE
    nh = tm // POOL_HALO
    n_halo = s // POOL_HALO
    cur = lambda b, i: (b, i, 0)
    in_specs = [
        pl.BlockSpec((1, tm, GLA_V_TOT), cur), pl.BlockSpec((1, tm, GLA_V_TOT), cur),
        pl.BlockSpec((1, tm, GLA_V_TOT), cur), _const_spec((1, GLA_DV)),
        pl.BlockSpec((1, tm, POOL_W), cur),
        pl.BlockSpec((1, POOL_HALO, POOL_W), lambda b, i: (b, jnp.maximum(i * nh - 1, 0), 0)),
        pl.BlockSpec((1, POOL_HALO, POOL_W), lambda b, i: (b, jnp.minimum((i + 1) * nh, n_halo - 1), 0)),
        pl.BlockSpec((1, tm, POOL_W), cur),
        _const_spec((POOL_GROUPS, POOL_DG, POOL_DG)), _const_spec((1, POOL_W)),
        _const_spec((GLA_V_TOT, D_MODEL)), _const_spec((POOL_W, D_MODEL)),
        pl.BlockSpec((1, tm, D_MODEL), cur)]
    return pl.pallas_call(
        functools.partial(_even_out_kernel, s),
        grid=(bsz, s // tm),
        in_specs=in_specs,
        out_specs=pl.BlockSpec((1, tm, D_MODEL), cur),
        out_shape=jax.ShapeDtypeStruct((bsz, s, D_MODEL), F32),
        compiler_params=_cparams(("parallel", "parallel")),
        name="even_out",
    )(o_f, o_b, gate, norm_g[None, :], pool_u, pool_u, pool_u, pool_gate,
      pool_w.astype(BF16), pool_scale[None, :],
      w_out[:GLA_V_TOT].astype(BF16), w_out[GLA_V_TOT:].astype(BF16), x)


def _mla_qkv_kernel(lat, cos_h, sin_h, rope_r, qg, kvg, wq, wkn, place, wvt, q_ref, k_ref, vt_ref):
    qk_scale = math.log2(math.e) * (MLA_NOPE + MLA_ROPE) ** -0.5
    w_all = MLA_HEADS * MLA_HEAD_PAD
    x = lat[0]
    cq = x[:, :MLA_Q_LORA]
    ckv = x[:, MLA_Q_LORA:MLA_Q_LORA + MLA_KV_LORA]
    nq = _rms(cq, qg[...]).astype(BF16)
    qq = _dot(nq, wq[...])
    cos2 = jnp.concatenate([cos_h[...], cos_h[...]], axis=-1)
    sin2 = jnp.concatenate([sin_h[...], sin_h[...]], axis=-1)
    for p in range(MLA_HEADS // 2):
        sl = slice(2 * p * MLA_HEAD_PAD, (2 * p + 2) * MLA_HEAD_PAD)
        sr = slice(w_all + 2 * p * MLA_HEAD_PAD, w_all + (2 * p + 2) * MLA_HEAD_PAD)
        q_ref[0, :, sl] = ((qq[:, sl] * cos2 + qq[:, sr] * sin2) * qk_scale).astype(BF16)

    nkv = _rms(ckv, kvg[...]).astype(BF16)
    k_rope = (x[:, MLA_Q_LORA + MLA_KV_LORA:] * rope_r[...]).astype(BF16)
    k_ref[0] = (_dot(nkv, wkn[...]) + _dot(k_rope, place[...])).astype(BF16)

    vt = _dot_nt(wvt[...], nkv)
    ones_row = lax.broadcasted_iota(jnp.int32, (MLA_VT_ROWS, 1), 0) == MLA_DV
    for h in range(MLA_HEADS):
        vh = vt[h * MLA_VT_ROWS:(h + 1) * MLA_VT_ROWS]
        vt_ref[0, h, 0] = jnp.where(ones_row, 1.0, vh).astype(BF16)


def _mla_qkv(lat, tabs, q_norm_g, kv_norm_g, wq, wkn, place, wvt):
    bsz, s, _ = lat.shape
    tm = ATT_TK
    w_all = MLA_HEADS * MLA_HEAD_PAD
    cur = lambda b, i: (b, i, 0)
    tab = lambda b, i: (i, 0)
    cos_h, sin_h, rope_r = tabs
    in_specs = [
        pl.BlockSpec((1, tm, MLA_LAT_W), cur),
        pl.BlockSpec((tm, MLA_HEAD_PAD), tab), pl.BlockSpec((tm, MLA_HEAD_PAD), tab),
        pl.BlockSpec((tm, MLA_MISC_W), tab),
        _const_spec((1, MLA_Q_LORA)), _const_spec((1, MLA_KV_LORA)),
        _const_spec(wq.shape), _const_spec(wkn.shape), _const_spec(place.shape), _const_spec(wvt.shape)]
    out_specs = [
        pl.BlockSpec((1, tm, w_all), cur), pl.BlockSpec((1, tm, w_all), cur),
        pl.BlockSpec((1, MLA_HEADS, 1, MLA_VT_ROWS, tm), lambda b, i: (b, 0, i, 0, 0))]
    out_shape = [
        jax.ShapeDtypeStruct((bsz, s, w_all), BF16), jax.ShapeDtypeStruct((bsz, s, w_all), BF16),
        jax.ShapeDtypeStruct((bsz, MLA_HEADS, s // tm, MLA_VT_ROWS, tm), BF16)]
    return pl.pallas_call(
        _mla_qkv_kernel,
        grid=(bsz, s // tm),
        in_specs=in_specs,
        out_specs=out_specs,
        out_shape=out_shape,
        compiler_params=_cparams(("parallel", "parallel")),
        name="mla_qkv",
    )(lat, cos_h, sin_h, rope_r, q_norm_g[None, :], kv_norm_g[None, :], wq, wkn, place, wvt)


def _mla_weights(q_up, kv_up):
    dq = MLA_NOPE + MLA_ROPE
    half = MLA_ROPE // 2
    qh = q_up.reshape(MLA_Q_LORA, MLA_HEADS, dq)
    zeros = jnp.zeros((MLA_Q_LORA, MLA_HEADS, MLA_HEAD_PAD - dq), F32)
    main = jnp.concatenate([qh, zeros], axis=-1)
    x1 = qh[..., MLA_NOPE:MLA_NOPE + half]
    x2 = qh[..., MLA_NOPE + half:]
    rot = jnp.concatenate([jnp.zeros((MLA_Q_LORA, MLA_HEADS, MLA_NOPE), F32), -x2, x1, zeros], axis=-1)
    w_all = MLA_HEADS * MLA_HEAD_PAD
    wq = jnp.concatenate([main.reshape(MLA_Q_LORA, w_all), rot.reshape(MLA_Q_LORA, w_all)], axis=-1)

    kvh = kv_up.reshape(MLA_KV_LORA, MLA_HEADS, MLA_NOPE + MLA_DV)
    wkn = jnp.concatenate(
        [kvh[..., :MLA_NOPE], jnp.zeros((MLA_KV_LORA, MLA_HEADS, MLA_HEAD_PAD - MLA_NOPE), F32)],
        axis=-1).reshape(MLA_KV_LORA, w_all)
    wv = jnp.transpose(kvh[..., MLA_NOPE:], (1, 2, 0))
    wvt = jnp.concatenate(
        [wv, jnp.zeros((MLA_HEADS, MLA_VT_ROWS - MLA_DV, MLA_KV_LORA), F32)],
        axis=1).reshape(MLA_HEADS * MLA_VT_ROWS, MLA_KV_LORA)
    r = jnp.arange(MLA_MISC_W)
    cols = jnp.arange(w_all)
    place = ((r[:, None] < 2 * MLA_ROPE)
             & ((cols[None, :] % MLA_HEAD_PAD) == (MLA_NOPE + r[:, None] % MLA_ROPE))).astype(BF16)
    return wq.astype(BF16), wkn.astype(BF16), place, wvt.astype(BF16)


def _rope_tables(s):
    inv = ROPE_THETA ** (-jnp.arange(0, MLA_ROPE, 2, dtype=F32) / MLA_ROPE)
    ang = jnp.arange(s, dtype=F32)[:, None] * inv[None, :]
    cos, sin = jnp.cos(ang), jnp.sin(ang)
    pad = jnp.zeros((s, MLA_HEAD_PAD - MLA_NOPE - MLA_ROPE), F32)
    cos_h = jnp.concatenate([jnp.ones((s, MLA_NOPE), F32), cos, cos, pad], axis=-1)
    sin_h = jnp.concatenate([jnp.zeros((s, MLA_NOPE), F32), sin, sin, pad], axis=-1)
    rope_r = jnp.concatenate([cos, cos, sin, sin, jnp.zeros((s, MLA_MISC_W - 2 * MLA_ROPE), F32)], axis=-1)
    return cos_h, sin_h, rope_r


def _mla_attn_kernel(q_ref, k_ref, vt_ref, o_ref, s_ref):
    n_k = k_ref.shape[1] // ATT_TK
    for qs in range(ATT_TQ // ATT_TQ_SUB):
        rows = slice(qs * ATT_TQ_SUB, (qs + 1) * ATT_TQ_SUB)
        qts = (q_ref[0, rows, 0:MLA_HEAD_PAD], q_ref[0, rows, MLA_HEAD_PAD:2 * MLA_HEAD_PAD])

        def produce(kk, buf):
            r = pl.multiple_of(kk * ATT_TK, ATT_TK)
            tile_max = []
            for hh in range(2):
                kt = k_ref[0, pl.ds(r, ATT_TK), hh * MLA_HEAD_PAD:(hh + 1) * MLA_HEAD_PAD]
                st = _dot_nt(kt, qts[hh])
                s_ref[buf, hh] = st
                tile_max.append(jnp.max(st, axis=0, keepdims=True))
            return tuple(tile_max)

        def consume(kk, buf, tile_max, carry):
            out = []
            for hh in range(2):
                m, acc = carry[2 * hh], carry[2 * hh + 1]
                m_new = jnp.maximum(m, tile_max[hh])
                alpha = jnp.exp2(m - m_new)
                p = jnp.exp2(s_ref[buf, hh] - m_new).astype(BF16)
                acc = acc * alpha + _dot(vt_ref[0, hh, kk], p)
                out += [m_new, acc]
            return tuple(out)

        def run(k0, state, produce_next):
            carry, tmax = state[:4], state[4:]
            for t in range(ATT_UNROLL):
                last = t == ATT_UNROLL - 1
                nxt = produce(k0 + t + 1, (t + 1) % 2) if (produce_next or not last) else ()
                carry = consume(k0 + t, t % 2, tmax, carry)
                tmax = nxt
            return carry + tmax

        m0 = jnp.full((1, ATT_TQ_SUB), -jnp.inf, F32)
        a0 = jnp.zeros((MLA_VT_ROWS, ATT_TQ_SUB), F32)
        state = lax.fori_loop(0, n_k // ATT_UNROLL - 1, lambda j, st: run(ATT_UNROLL * j, st, True),
                              (m0, a0, m0, a0) + produce(0, 0))
        carry = run(n_k - ATT_UNROLL, state, False)
        outs = [carry[2 * hh + 1][:MLA_DV] / carry[2 * hh + 1][MLA_DV:MLA_DV + 1] for hh in range(2)]
        o_ref[0, rows, :] = jnp.concatenate(outs, axis=0).T.astype(o_ref.dtype)


def _mla_attn(q, k, vt):
    bsz, s, _ = q.shape
    n_kt = s // ATT_TK
    assert s % (ATT_TK * ATT_UNROLL) == 0 and s % ATT_TQ == 0 and ATT_UNROLL % 2 == 0
    return pl.pallas_call(
        _mla_attn_kernel,
        grid=(bsz, MLA_HEADS // 2, s // ATT_TQ),
        in_specs=[
            pl.BlockSpec((1, ATT_TQ, 2 * MLA_HEAD_PAD), lambda b, h, i: (b, i, h)),
            pl.BlockSpec((1, s, 2 * MLA_HEAD_PAD), lambda b, h, i: (b, 0, h)),
            pl.BlockSpec((1, 2, n_kt, MLA_VT_ROWS, ATT_TK), lambda b, h, i: (b, h, 0, 0, 0))],
        out_specs=pl.BlockSpec((1, ATT_TQ, 2 * MLA_DV), lambda b, h, i: (b, i, h)),
        out_shape=jax.ShapeDtypeStruct((bsz, s, MLA_W), BF16),
        scratch_shapes=[pltpu.VMEM((2, 2, ATT_TK, ATT_TQ_SUB), F32)],
        compiler_params=_cparams(("parallel", "parallel", "arbitrary")),
        name="mla_attn",
    )(q, k, vt)


def _lane_scan(x, op, identity, reverse):
    lane = lax.broadcasted_iota(jnp.int32, x.shape, 1)
    step = 1
    while step < ML_CHUNK:
        if reverse:
            shifted = jnp.where(lane < ML_CHUNK - step, pltpu.roll(x, ML_CHUNK - step, 1), identity)
        else:
            shifted = jnp.where(lane >= step, pltpu.roll(x, step, 1), identity)
        x = op(x, shifted)
        step *= 2
    return x


def _mlstm_prologue(grf, grb, bias_r, eye, tri_pre, tri_suf, b_ref, cmax_ref, ctot_ref, gtot_ref, ccol_ref):
    n_chunks = ML_SEQ_TILE // ML_CHUNK
    n_dir = 2 * ML_HEADS
    rows16 = lax.broadcasted_iota(jnp.int32, (1, 2 * n_dir, 1), 1)
    g = jnp.where((rows16 % n_dir) < ML_HEADS, grf[0], grb[0]) + bias_r[...][None]
    li = g[:, :n_dir].reshape(n_chunks * n_dir, ML_CHUNK)
    lf = _log_sigmoid(g[:, n_dir:]).reshape(n_chunks * n_dir, ML_CHUNK)
    is_fwd = (lax.broadcasted_iota(jnp.int32, li.shape, 0) % n_dir) < ML_HEADS
    lane = lax.broadcasted_iota(jnp.int32, li.shape, 1)
    b = jnp.where(is_fwd, _tri_right(lf, tri_pre), _tri_right(lf, tri_suf))
    c = li - b
    c_next = jnp.where(lane < ML_CHUNK - 1, pltpu.roll(c, ML_CHUNK - 1, 1), -jnp.inf)
    cmax = jnp.where(is_fwd, _lane_scan(c, jnp.maximum, -jnp.inf, False),
                     _lane_scan(c_next, jnp.maximum, -jnp.inf, True))
    shape3 = (n_chunks, n_dir, ML_CHUNK)
    b_ref[...] = b.reshape(shape3)
    cmax_ref[...] = cmax.reshape(shape3)
    ctot_ref[...] = jnp.broadcast_to(jnp.max(c, axis=-1, keepdims=True), c.shape).reshape(shape3)
    gtot_ref[...] = jnp.broadcast_to(jnp.sum(lf, axis=-1, keepdims=True), c.shape).reshape(shape3)
    hi, mid, lo = _split3(c)
    for jc in range(n_chunks):
        rs = slice(jc * n_dir, (jc + 1) * n_dir)
        ccol_ref[jc] = _dot_nt(eye, hi[rs]) + _dot_nt(eye, mid[rs]) + _dot_nt(eye, lo[rs])


def _mlstm_chunk(q_ref, k_ref, v_ref, o_ref, cnt_ref, m_ref, n, row, jc, stats, inclusive):
    b_ref, cmax_ref, ctot_ref, gtot_ref, ccol_ref = stats
    h = n % ML_HEADS
    hs = slice(h * ML_DH, (h + 1) * ML_DH)
    jj = lax.broadcasted_iota(jnp.int32, (ML_CHUNK, ML_CHUNK), 0)
    ii = lax.broadcasted_iota(jnp.int32, (ML_CHUNK, ML_CHUNK), 1)
    mask = (jj <= ii) if inclusive else (jj > ii)

    m = m_ref[n]
    b_r = b_ref[jc, n:n + 1, :]
    big_m = jnp.maximum(m, cmax_ref[jc, n:n + 1, :])
    m_tot = jnp.maximum(m, ctot_ref[jc, n:n + 1, 0:1])
    g_tot = gtot_ref[jc, n:n + 1, 0:1]
    c_b = jnp.broadcast_to(ccol_ref[jc, :, n:n + 1], (ML_CHUNK, ML_CHUNK))

    q = q_ref[0, pl.ds(row, ML_CHUNK), hs]
    ks = k_ref[0, pl.ds(row, ML_CHUNK), hs] * (ML_DH ** -0.5)
    v = v_ref[0, pl.ds(row, ML_CHUNK), hs]
    ones_row = (lax.broadcasted_iota(jnp.int32, (ML_DH, ML_CHUNK), 0) == 0).astype(BF16)
    vt_ext = jnp.concatenate([v.T, ones_row], axis=0)

    cnt = cnt_ref[n]
    lhs = jnp.concatenate([ks.astype(BF16), cnt.astype(BF16)], axis=0)
    sr = _dot_nt(lhs, q)
    pt = jnp.exp(jnp.where(mask, c_b - big_m, -jnp.inf))
    st = (sr[:ML_CHUNK] * pt).astype(BF16)
    kw = (ks * jnp.exp(c_b - m_tot)).astype(BF16)
    ho = _dot(vt_ext, jnp.concatenate([st, kw], axis=-1))
    ht = ho[:, :ML_CHUNK] + jnp.exp(m - big_m) * sr[ML_CHUNK:]
    den = ht[ML_DH:ML_DH + 1]
    scale = 1.0 / jnp.maximum(jnp.abs(den), jnp.exp(-(b_r + big_m)))
    o_ref[0, pl.ds(row, ML_CHUNK), hs] = (ht[:ML_DH] * scale).T.astype(o_ref.dtype)
    cnt_ref[n] = jnp.exp(m - m_tot) * cnt + ho[:, ML_CHUNK:]
    m_ref[n] = g_tot + m_tot


def _mlstm_kernel(qf, kf, vf, grf, qb, kb, vb, grb, bias_r, eye_ref, tri_pre, tri_suf, of, ob,
                  cnt_ref, m_ref, b_ref, cmax_ref, ctot_ref, gtot_ref, ccol_ref):
    c = pl.program_id(1)

    @pl.when(c == 0)
    def _():
        cnt_ref[...] = jnp.zeros_like(cnt_ref)
        m_ref[...] = jnp.zeros_like(m_ref)

    stats = (b_ref, cmax_ref, ctot_ref, gtot_ref, ccol_ref)
    _mlstm_prologue(grf, grb, bias_r, eye_ref[...], tri_pre[...], tri_suf[...], *stats)
    n_chunks = ML_SEQ_TILE // ML_CHUNK

    def body(j, carry):
        jb = n_chunks - 1 - j
        rf = pl.multiple_of(j * ML_CHUNK, ML_CHUNK)
        rb = pl.multiple_of(jb * ML_CHUNK, ML_CHUNK)
        for h in range(ML_HEADS):
            _mlstm_chunk(qf, kf, vf, of, cnt_ref, m_ref, h, rf, j, stats, True)
            _mlstm_chunk(qb, kb, vb, ob, cnt_ref, m_ref, ML_HEADS + h, rb, jb, stats, False)
        return carry

    lax.fori_loop(0, n_chunks, body, 0)


def _mlstm(q, k, v, gates, if_bias):
    bsz, s, _ = q.shape
    ts = ML_SEQ_TILE
    ns = s // ts
    n_chunks = ts // ML_CHUNK
    n_gate = 4 * ML_HEADS
    n_dir = 2 * ML_HEADS
    gates_r = gates.reshape(bsz, s // ML_CHUNK, ML_CHUNK, n_gate).transpose(0, 1, 3, 2)
    fwd = lambda b, c: (b, c, 0)
    bwd = lambda b, c: (b, ns - 1 - c, 0)
    fwd4 = lambda b, c: (b, c, 0, 0)
    bwd4 = lambda b, c: (b, ns - 1 - c, 0, 0)

    def seq_specs(imap, imap4):
        return [pl.BlockSpec((1, ts, ML_W), imap), pl.BlockSpec((1, ts, ML_W), imap),
                pl.BlockSpec((1, ts, ML_W), imap),
                pl.BlockSpec((1, n_chunks, n_gate, ML_CHUNK), imap4)]

    in_specs = seq_specs(fwd, fwd4) + seq_specs(bwd, bwd4) + [
        _const_spec((n_gate, ML_CHUNK))] + [_const_spec((ML_CHUNK, ML_CHUNK))] * 3
    idx = jnp.arange(ML_CHUNK)
    tri_pre = (idx[:, None] <= idx[None, :]).astype(BF16)
    tri_suf = (idx[:, None] >= idx[None, :]).astype(BF16)
    out_specs = [pl.BlockSpec((1, ts, ML_W), fwd), pl.BlockSpec((1, ts, ML_W), bwd)]
    out_shape = [jax.ShapeDtypeStruct((bsz, s, ML_W), BF16)] * 2
    stat = pltpu.VMEM((n_chunks, n_dir, ML_CHUNK), F32)
    return pl.pallas_call(
        _mlstm_kernel,
        grid=(bsz, ns),
        in_specs=in_specs,
        out_specs=out_specs,
        out_shape=out_shape,
        scratch_shapes=[pltpu.VMEM((n_dir, 2 * ML_DH, ML_DH), F32), pltpu.VMEM((n_dir, 1, 1), F32),
                        stat, stat, stat, stat, pltpu.VMEM((n_chunks, ML_CHUNK, n_dir), F32)],
        compiler_params=_cparams(("parallel", "arbitrary")),
        name="mlstm_scan",
    )(q, k, v, gates_r, q, k, v, gates_r,
      jnp.broadcast_to(if_bias[:, None], (n_gate, ML_CHUNK)), jnp.eye(ML_CHUNK, dtype=BF16),
      tri_pre, tri_suf)


def _odd_out_kernel(final, att, mgate, hf, hb, mo, lgate, ng, wa, wb, x, fg, o_ref):
    mla_out = (att[0].astype(F32) * _silu(mgate[0].astype(F32))).astype(BF16)
    hm = hf[0].astype(F32) + hb[0].astype(F32)
    mo_v, lg_v = mo[0].astype(F32), lgate[0].astype(F32)
    parts = []
    for h in range(ML_HEADS):
        hs = slice(h * ML_DH, (h + 1) * ML_DH)
        y = _rms(hm[:, hs], ng[...]) * _sigmoid(mo_v[:, hs])
        parts.append((y * _silu(lg_v[:, hs])).astype(BF16))
    ml_out = jnp.concatenate(parts, axis=-1)
    xn = x[0] + (_dot(mla_out, wa[...]) + _dot(ml_out, wb[...]))
    o_ref[0] = _rms(xn, fg[...]) if final else xn


def _odd_out(att, mla_gate, h_f, h_b, mo, ml_gate, norm_g, w_out, x, final_g, final):
    bsz, s, _ = x.shape
    tm = TAIL_TILE
    cur = lambda b, i: (b, i, 0)
    half = pl.BlockSpec((1, tm, MLA_W), cur)
    in_specs = [half] * 6 + [
        _const_spec((1, ML_DH)), _const_spec((MLA_W, D_MODEL)), _const_spec((ML_W, D_MODEL)),
        pl.BlockSpec((1, tm, D_MODEL), cur), _const_spec((1, D_MODEL))]
    return pl.pallas_call(
        functools.partial(_odd_out_kernel, final),
        grid=(bsz, s // tm),
        in_specs=in_specs,
        out_specs=pl.BlockSpec((1, tm, D_MODEL), cur),
        out_shape=jax.ShapeDtypeStruct((bsz, s, D_MODEL), F32),
        compiler_params=_cparams(("parallel", "parallel")),
        name="odd_out",
    )(att, mla_gate, h_f, h_b, mo, ml_gate, norm_g[None, :],
      w_out[:MLA_W].astype(BF16), w_out[MLA_W:].astype(BF16), x, final_g[None, :])


def _col_split(w, sizes):
    out, off = [], 0
    for n in sizes:
        out.append(w[:, off:off + n].astype(BF16))
        off += n
    return out


def _even_layer(x, g, w_in, a_up, a_bias, gla_norm_g, pool_w, pool_scale, w_out):
    bsz, s, _ = x.shape
    sizes = (GLA_K_TOT, GLA_K_TOT, GLA_V_TOT, GLA_V_TOT, 2 * GLA_LR, POOL_W, POOL_W)
    dts = (F32, F32, BF16, BF16, BF16, F32, BF16)
    outs = _norm_proj(x.reshape(bsz * s, D_MODEL), g[None, :], _col_split(w_in, sizes), dts)
    q, k, v, gate, lr, pool_u, pool_gate = [o.reshape(bsz, s, -1) for o in outs]
    o_f, o_b = _gla(q, k, v, lr, a_up, a_bias)
    return _even_out(o_f, o_b, gate, gla_norm_g, pool_u, pool_gate, pool_w, pool_scale, w_out, x)


def _odd_layer(x, g, tabs, w_in, q_norm_g, q_up, kv_norm_g, kv_up, if_bias, ml_norm_g, w_out,
               final_g, final):
    bsz, s, _ = x.shape
    half = MLA_ROPE // 2
    off = MLA_Q_LORA + MLA_KV_LORA
    kr_w = w_in[:, off:off + MLA_ROPE]
    n_gate = 4 * ML_HEADS
    gate_off = off + MLA_ROPE + MLA_W + 4 * ML_W
    w_lat = jnp.concatenate(
        [w_in[:, :off + MLA_ROPE], -kr_w[:, half:], kr_w[:, :half], w_in[:, gate_off:gate_off + n_gate],
         jnp.zeros((D_MODEL, MLA_MISC_W - 2 * MLA_ROPE - n_gate), F32)], axis=-1).astype(BF16)
    wide = _col_split(w_in[:, off + MLA_ROPE:gate_off], (MLA_W, ML_W, ML_W, ML_W, ML_W))
    wide.append(w_in[:, gate_off + n_gate:].astype(BF16))
    dts = (F32, BF16, BF16, F32, BF16, BF16, BF16)
    outs = _norm_proj(x.reshape(bsz * s, D_MODEL), g[None, :], [w_lat] + wide, dts)
    lat, mla_gate, mq, mk, mv, mo, ml_gate = [o.reshape(bsz, s, -1) for o in outs]
    mif = lat[:, :, off + 2 * MLA_ROPE:off + 2 * MLA_ROPE + n_gate]
    wq, wkn, place, wvt = _mla_weights(q_up, kv_up)
    qa, ka, vt = _mla_qkv(lat, tabs, q_norm_g, kv_norm_g, wq, wkn, place, wvt)
    att = _mla_attn(qa, ka, vt)
    h_f, h_b = _mlstm(mq, mk, mv, mif, if_bias)
    return _odd_out(att, mla_gate, h_f, h_b, mo, ml_gate, ml_norm_g, w_out, x, final_g, final)


def _trunk(x, norm_g, final_norm_g, e_w_in, e_gla_a_up, e_gla_a_bias, e_gla_norm_g, e_pool_w,
           e_pool_scale, e_w_out, o_w_in, o_q_norm_g, o_q_up, o_kv_norm_g, o_kv_up, o_if_bias,
           o_mlstm_norm_g, o_w_out):
    depth = norm_g.shape[0]
    assert depth % 2 == 0, "the final RMSNorm is fused into the last (odd) layer's tail"
    tabs = _rope_tables(x.shape[1])
    for layer in range(depth):
        i = layer // 2
        if layer % 2 == 0:
            x = _even_layer(x, norm_g[layer], e_w_in[i], e_gla_a_up[i], e_gla_a_bias[i],
                            e_gla_norm_g[i], e_pool_w[i], e_pool_scale[i], e_w_out[i])
        else:
            x = _odd_layer(x, norm_g[layer], tabs, o_w_in[i], o_q_norm_g[i], o_q_up[i],
                           o_kv_norm_g[i], o_kv_up[i], o_if_bias[i], o_mlstm_norm_g[i], o_w_out[i],
                           final_norm_g, layer == depth - 1)
    return x


def kernel(x_prompt, x_sample, norm_g, final_norm_g, e_w_in, e_gla_a_up, e_gla_a_bias, e_gla_norm_g,
           e_pool_w, e_pool_scale, e_w_out, o_w_in, o_q_norm_g, o_q_up, o_kv_norm_g, o_kv_up,
           o_if_bias, o_mlstm_norm_g, o_w_out):
    params = (norm_g, final_norm_g, e_w_in, e_gla_a_up, e_gla_a_bias, e_gla_norm_g, e_pool_w,
              e_pool_scale, e_w_out, o_w_in, o_q_norm_g, o_q_up, o_kv_norm_g, o_kv_up, o_if_bias,
              o_mlstm_norm_g, o_w_out)
    return (_trunk(x_prompt, *params), _trunk(x_sample, *params))
```

```python
import functools
import math

import jax
import jax.numpy as jnp
from jax import lax
from jax.experimental import pallas as pl
from jax.experimental.pallas import tpu as pltpu

F32 = jnp.float32
BF16 = jnp.bfloat16

D_MODEL = 1024
NORM_EPS = 1e-6
CHUNK = 64

GLA_HEADS = 4
GLA_DK = 128
GLA_DV = 256
GLA_LR = 16
GLA_GATE_NORM = 16.0
GLA_K_TOT = GLA_HEADS * GLA_DK
GLA_V_TOT = GLA_HEADS * GLA_DV
GLA_BLOCK = 2 * CHUNK

POOL_GROUPS = 4
POOL_WINDOWS = (2, 4, 8, 16)
POOL_DG = 128
POOL_W = POOL_GROUPS * POOL_DG
POOL_HALO = 8

MLA_HEADS = 8
MLA_NOPE = 64
MLA_ROPE = 32
MLA_DV = 64
MLA_Q_LORA = 384
MLA_KV_LORA = 256
MLA_W = MLA_HEADS * MLA_DV
MLA_HEAD_PAD = 128
MLA_VT_ROWS = 80
MLA_MISC_W = 128
MLA_LAT_W = MLA_Q_LORA + MLA_KV_LORA + MLA_MISC_W
ROPE_THETA = 10000.0

ML_HEADS = 4
ML_DH = 128
ML_W = ML_HEADS * ML_DH
ML_CHUNK = 128

VMEM_LIMIT_BYTES = 56 * 1024 * 1024

TOKEN_TILE = 512
TAIL_TILE = 1024
SEQ_TILE = 1024
ML_SEQ_TILE = 1024
ATT_TQ = 1024
ATT_TQ_SUB = 512
ATT_TK = 256
ATT_UNROLL = 8


def _cparams(sem):
    return pltpu.CompilerParams(dimension_semantics=sem, vmem_limit_bytes=VMEM_LIMIT_BYTES)


def _const_spec(shape):
    nd = len(shape)
    return pl.BlockSpec(shape, lambda *_: (0,) * nd)


def _dot(a, b):
    return jnp.dot(a, b, preferred_element_type=F32)


def _dot_nt(a, b):
    return lax.dot_general(a, b, (((1,), (1,)), ((), ())), preferred_element_type=F32)


def _dot_tn(a, b):
    return lax.dot_general(a, b, (((0,), (0,)), ((), ())), preferred_element_type=F32)


def _sigmoid(x):
    return 1.0 / (1.0 + jnp.exp(-x))


def _silu(x):
    return x * _sigmoid(x)


def _log_sigmoid(x):
    return jnp.minimum(x, 0.0) - jnp.log(1.0 + jnp.exp(-jnp.abs(x)))


def _rms(x, g):
    return x * lax.rsqrt(jnp.mean(x * x, axis=-1, keepdims=True) + NORM_EPS) * g


def _split3(x):
    hi = x.astype(BF16)
    r1 = x - hi.astype(F32)
    mid = r1.astype(BF16)
    lo = (r1 - mid.astype(F32)).astype(BF16)
    return hi, mid, lo


def _tri_left(tri, x):
    hi, mid, lo = _split3(x)
    return _dot(tri, hi) + _dot(tri, mid) + _dot(tri, lo)


def _tri_right(x, tri):
    hi, mid, lo = _split3(x)
    return _dot(hi, tri) + _dot(mid, tri) + _dot(lo, tri)


def _norm_proj_kernel(n_out, x_ref, g_ref, *refs):
    w_refs, o_refs = refs[:n_out], refs[n_out:]
    h = _rms(x_ref[...], g_ref[...]).astype(BF16)
    for w_ref, o_ref in zip(w_refs, o_refs):
        o_ref[...] = _dot(h, w_ref[...]).astype(o_ref.dtype)


def _norm_proj(x2d, g, weights, out_dtypes):
    t = x2d.shape[0]
    tm = TOKEN_TILE
    n_out = len(weights)
    in_specs = [pl.BlockSpec((tm, D_MODEL), lambda i: (i, 0)), _const_spec((1, D_MODEL))]
    in_specs += [_const_spec(w.shape) for w in weights]
    out_specs = [pl.BlockSpec((tm, w.shape[1]), lambda i: (i, 0)) for w in weights]
    out_shape = [jax.ShapeDtypeStruct((t, w.shape[1]), dt) for w, dt in zip(weights, out_dtypes)]
    return pl.pallas_call(
        functools.partial(_norm_proj_kernel, n_out),
        grid=(t // tm,),
        in_specs=in_specs,
        out_specs=out_specs,
        out_shape=out_shape,
        compiler_params=_cparams(("parallel",)),
        name="norm_proj",
    )(x2d, g, *weights)


def _gla_block(q_ref, k_ref, v_ref, o_ref, st_ref, st_idx, row, b_all, inclusive):
    ii = lax.broadcasted_iota(jnp.int32, (GLA_BLOCK, GLA_BLOCK), 0)
    jj = lax.broadcasted_iota(jnp.int32, (GLA_BLOCK, GLA_BLOCK), 1)
    rr = lax.broadcasted_iota(jnp.int32, (GLA_BLOCK, 1), 0)
    same = (ii >= CHUNK) == (jj >= CHUNK)
    if inclusive:
        intra, cross, far_rows = same & (jj <= ii), (ii >= CHUNK) & (jj < CHUNK), rr >= CHUNK
        edge_lo, edge_hi = CHUNK - 1, GLA_BLOCK - 1
    else:
        intra, cross, far_rows = same & (jj > ii), (ii < CHUNK) & (jj >= CHUNK), rr < CHUNK
        edge_lo, edge_hi = 0, CHUNK
    for h in range(GLA_HEADS):
        ks = slice(h * GLA_DK, (h + 1) * GLA_DK)
        vs = slice(h * GLA_DV, (h + 1) * GLA_DV)
        q = q_ref[0, pl.ds(row, GLA_BLOCK), ks] * (GLA_DK ** -0.5)
        k = k_ref[0, pl.ds(row, GLA_BLOCK), ks]
        v = v_ref[0, pl.ds(row, GLA_BLOCK), vs]
        b = b_all[:, ks]
        tot_lo, tot_hi = b[edge_lo:edge_lo + 1, :], b[edge_hi:edge_hi + 1, :]
        tot_near, tot_far = (tot_lo, tot_hi) if inclusive else (tot_hi, tot_lo)
        qe = q * jnp.exp(b)
        kd = (k * jnp.exp(-b)).astype(BF16)
        kdec = k * jnp.exp(jnp.where(rr < CHUNK, tot_lo, tot_hi) - b)
        pp = _dot_nt(qe.astype(BF16), jnp.concatenate([kd, kdec.astype(BF16)], axis=0))
        a = jnp.where(intra, pp[:, :GLA_BLOCK], jnp.where(cross, pp[:, GLA_BLOCK:], 0.0)).astype(BF16)
        qe_in = jnp.where(far_rows, qe * jnp.exp(tot_near), qe).astype(BF16)
        kdec_out = jnp.where(far_rows, kdec, kdec * jnp.exp(tot_far)).astype(BF16)
        st = st_ref[st_idx + h]
        o_ref[0, pl.ds(row, GLA_BLOCK), vs] = (_dot(a, v) + _dot_nt(qe_in, st.astype(BF16))).astype(o_ref.dtype)
        st_ref[st_idx + h] = st * jnp.exp(tot_near + tot_far) + _dot_tn(v, kdec_out)


def _gla_kernel(qf, kf, vf, lrf, qb, kb, vb, lrb, aupf, aupb, biasf, biasb, tril, triu,
                of, ob, st_ref, la_ref):
    c = pl.program_id(1)

    @pl.when(c == 0)
    def _():
        st_ref[...] = jnp.zeros_like(st_ref)

    inv = 1.0 / GLA_GATE_NORM
    la_ref[0] = _log_sigmoid(_dot(lrf[0], aupf[...]) + biasf[...]) * inv
    la_ref[1] = _log_sigmoid(_dot(lrb[0], aupb[...]) + biasb[...]) * inv
    n_blocks = SEQ_TILE // GLA_BLOCK

    def body(j, carry):
        rf = pl.multiple_of(j * GLA_BLOCK, GLA_BLOCK)
        rb = pl.multiple_of((n_blocks - 1 - j) * GLA_BLOCK, GLA_BLOCK)
        b_f = _tri_left(tril[...], la_ref[0, pl.ds(rf, GLA_BLOCK), :])
        b_b = _tri_left(triu[...], la_ref[1, pl.ds(rb, GLA_BLOCK), :])
        _gla_block(qf, kf, vf, of, st_ref, 0, rf, b_f, True)
        _gla_block(qb, kb, vb, ob, st_ref, GLA_HEADS, rb, b_b, False)
        return carry

    lax.fori_loop(0, n_blocks, body, 0)


def _gla(q, k, v, lr, a_up, a_bias):
    bsz, s, _ = q.shape
    ts = SEQ_TILE
    ns = s // ts
    fwd = lambda b, c: (b, c, 0)
    bwd = lambda b, c: (b, ns - 1 - c, 0)
    idx = jnp.arange(GLA_BLOCK)
    same_chunk = (idx[None, :] // CHUNK) == (idx[:, None] // CHUNK)
    tril = (same_chunk & (idx[None, :] <= idx[:, None])).astype(BF16)
    triu = (same_chunk & (idx[None, :] >= idx[:, None])).astype(BF16)

    def seq_specs(imap):
        return [pl.BlockSpec((1, ts, GLA_K_TOT), imap), pl.BlockSpec((1, ts, GLA_K_TOT), imap),
                pl.BlockSpec((1, ts, GLA_V_TOT), imap), pl.BlockSpec((1, ts, 2 * GLA_LR), imap)]

    in_specs = seq_specs(fwd) + seq_specs(bwd) + [
        _const_spec((2 * GLA_LR, GLA_K_TOT)), _const_spec((2 * GLA_LR, GLA_K_TOT)),
        _const_spec((1, GLA_K_TOT)), _const_spec((1, GLA_K_TOT)),
        _const_spec((GLA_BLOCK, GLA_BLOCK)), _const_spec((GLA_BLOCK, GLA_BLOCK))]
    out_specs = [pl.BlockSpec((1, ts, GLA_V_TOT), fwd), pl.BlockSpec((1, ts, GLA_V_TOT), bwd)]
    out_shape = [jax.ShapeDtypeStruct((bsz, s, GLA_V_TOT), BF16)] * 2
    zeros = jnp.zeros((GLA_LR, GLA_K_TOT), F32)
    aup_f = jnp.concatenate([a_up[0], zeros], axis=0).astype(BF16)
    aup_b = jnp.concatenate([zeros, a_up[1]], axis=0).astype(BF16)
    return pl.pallas_call(
        _gla_kernel,
        grid=(bsz, ns),
        in_specs=in_specs,
        out_specs=out_specs,
        out_shape=out_shape,
        scratch_shapes=[pltpu.VMEM((2 * GLA_HEADS, GLA_DV, GLA_DK), F32),
                        pltpu.VMEM((2, ts, GLA_K_TOT), F32)],
        compiler_params=_cparams(("parallel", "arbitrary")),
        name="gla_scan",
    )(q, k, v, lr, q, k, v, lr, aup_f, aup_b, a_bias[0:1], a_bias[1:2], tril, triu)


def _even_out_kernel(seq_len, of, ob, gate, ng, pu, pprev, pnext, pgate, pw, pscale, wa, wb, x, o_ref):
    tm = of.shape[1]
    i = pl.program_id(1)
    n_i = pl.num_programs(1)

    o = of[0].astype(F32) + ob[0].astype(F32)
    g_all = gate[0].astype(F32)
    parts = []
    for h in range(GLA_HEADS):
        vs = slice(h * GLA_DV, (h + 1) * GLA_DV)
        parts.append((_rms(o[:, vs], ng[...]) * _silu(g_all[:, vs])).astype(BF16))
    gla_out = jnp.concatenate(parts, axis=-1)

    u = pu[0]
    prev = jnp.where(i > 0, pprev[0], 0.0)
    nxt = jnp.where(i < n_i - 1, pnext[0], 0.0)
    ext = jnp.concatenate([prev, u, nxt], axis=0)
    n_ext = tm + 2 * POOL_HALO
    pos = i * tm + lax.broadcasted_iota(jnp.int32, (tm, 1), 0)
    mixed = []
    for gi, w in enumerate(POOL_WINDOWS):
        cs = slice(gi * POOL_DG, (gi + 1) * POOL_DG)
        a = ext[:, cs]
        span = 1
        while span < w:
            a = a + pltpu.roll(a, span, 0)
            span *= 2
        shift = w // 2 - 1
        if shift:
            a = pltpu.roll(a, n_ext - shift, 0)
        win = a[POOL_HALO:POOL_HALO + tm]
        lo = jnp.maximum(pos - w // 2, 0)
        hi = jnp.minimum(pos + w // 2, seq_len)
        pooled = win / (hi - lo).astype(F32) - u[:, cs]
        mixed.append(_dot(pooled.astype(BF16), pw[gi]))
    pool_out = (jnp.concatenate(mixed, axis=-1) * pscale[...] * _silu(pgate[0].astype(F32))).astype(BF16)

    y = _dot(gla_out, wa[...]) + _dot(pool_out, wb[...])
    o_ref[0] = x[0] + y


def _even_out(o_f, o_b, gate, norm_g, pool_u, pool_gate, pool_w, pool_scale, w_out, x):
    bsz, s, _ = x.shape
    tm = TOKEN_TILE
    nh = tm // POOL_HALO
    n_halo = s // POOL_HALO
    cur = lambda b, i: (b, i, 0)
    in_specs = [
        pl.BlockSpec((1, tm, GLA_V_TOT), cur), pl.BlockSpec((1, tm, GLA_V_TOT), cur),
        pl.BlockSpec((1, tm, GLA_V_TOT), cur), _const_spec((1, GLA_DV)),
        pl.BlockSpec((1, tm, POOL_W), cur),
        pl.BlockSpec((1, POOL_HALO, POOL_W), lambda b, i: (b, jnp.maximum(i * nh - 1, 0), 0)),
        pl.BlockSpec((1, POOL_HALO, POOL_W), lambda b, i: (b, jnp.minimum((i + 1) * nh, n_halo - 1), 0)),
        pl.BlockSpec((1, tm, POOL_W), cur),
        _const_spec((POOL_GROUPS, POOL_DG, POOL_DG)), _const_spec((1, POOL_W)),
        _const_spec((GLA_V_TOT, D_MODEL)), _const_spec((POOL_W, D_MODEL)),
        pl.BlockSpec((1, tm, D_MODEL), cur)]
    return pl.pallas_call(
        functools.partial(_even_out_kernel, s),
        grid=(bsz, s // tm),
        in_specs=in_specs,
        out_specs=pl.BlockSpec((1, tm, D_MODEL), cur),
        out_shape=jax.ShapeDtypeStruct((bsz, s, D_MODEL), F32),
        compiler_params=_cparams(("parallel", "parallel")),
        name="even_out",
    )(o_f, o_b, gate, norm_g[None, :], pool_u, pool_u, pool_u, pool_gate,
      pool_w.astype(BF16), pool_scale[None, :],
      w_out[:GLA_V_TOT].astype(BF16), w_out[GLA_V_TOT:].astype(BF16), x)


def _mla_qkv_kernel(lat, cos_h, sin_h, rope_r, qg, kvg, wq, wkn, place, wvt, q_ref, k_ref, vt_ref):
    qk_scale = math.log2(math.e) * (MLA_NOPE + MLA_ROPE) ** -0.5
    w_all = MLA_HEADS * MLA_HEAD_PAD
    x = lat[0]
    cq = x[:, :MLA_Q_LORA]
    ckv = x[:, MLA_Q_LORA:MLA_Q_LORA + MLA_KV_LORA]
    nq = _rms(cq, qg[...]).astype(BF16)
    qq = _dot(nq, wq[...])
    cos2 = jnp.concatenate([cos_h[...], cos_h[...]], axis=-1)
    sin2 = jnp.concatenate([sin_h[...], sin_h[...]], axis=-1)
    for p in range(MLA_HEADS // 2):
        sl = slice(2 * p * MLA_HEAD_PAD, (2 * p + 2) * MLA_HEAD_PAD)
        sr = slice(w_all + 2 * p * MLA_HEAD_PAD, w_all + (2 * p + 2) * MLA_HEAD_PAD)
        q_ref[0, :, sl] = ((qq[:, sl] * cos2 + qq[:, sr] * sin2) * qk_scale).astype(BF16)

    nkv = _rms(ckv, kvg[...]).astype(BF16)
    k_rope = (x[:, MLA_Q_LORA + MLA_KV_LORA:] * rope_r[...]).astype(BF16)
    k_ref[0] = (_dot(nkv, wkn[...]) + _dot(k_rope, place[...])).astype(BF16)

    vt = _dot_nt(wvt[...], nkv)
    ones_row = lax.broadcasted_iota(jnp.int32, (MLA_VT_ROWS, 1), 0) == MLA_DV
    for h in range(MLA_HEADS):
        vh = vt[h * MLA_VT_ROWS:(h + 1) * MLA_VT_ROWS]
        vh = jnp.where(ones_row, 1.0, vh).astype(BF16)
        for c in range(vt_ref.shape[2]):
            vt_ref[0, h, c] = vh[:, c * ATT_TK:(c + 1) * ATT_TK]


def _mla_qkv(lat, tabs, q_norm_g, kv_norm_g, wq, wkn, place, wvt):
    bsz, s, _ = lat.shape
    tm = TOKEN_TILE
    n_sub = tm // ATT_TK
    w_all = MLA_HEADS * MLA_HEAD_PAD
    cur = lambda b, i: (b, i, 0)
    tab = lambda b, i: (i, 0)
    cos_h, sin_h, rope_r = tabs
    in_specs = [
        pl.BlockSpec((1, tm, MLA_LAT_W), cur),
        pl.BlockSpec((tm, MLA_HEAD_PAD), tab), pl.BlockSpec((tm, MLA_HEAD_PAD), tab),
        pl.BlockSpec((tm, MLA_MISC_W), tab),
        _const_spec((1, MLA_Q_LORA)), _const_spec((1, MLA_KV_LORA)),
        _const_spec(wq.shape), _const_spec(wkn.shape), _const_spec(place.shape), _const_spec(wvt.shape)]
    out_specs = [
        pl.BlockSpec((1, tm, w_all), cur), pl.BlockSpec((1, tm, w_all), cur),
        pl.BlockSpec((1, MLA_HEADS, n_sub, MLA_VT_ROWS, ATT_TK), lambda b, i: (b, 0, i, 0, 0))]
    out_shape = [
        jax.ShapeDtypeStruct((bsz, s, w_all), BF16), jax.ShapeDtypeStruct((bsz, s, w_all), BF16),
        jax.ShapeDtypeStruct((bsz, MLA_HEADS, s // ATT_TK, MLA_VT_ROWS, ATT_TK), BF16)]
    return pl.pallas_call(
        _mla_qkv_kernel,
        grid=(bsz, s // tm),
        in_specs=in_specs,
        out_specs=out_specs,
        out_shape=out_shape,
        compiler_params=_cparams(("parallel", "parallel")),
        name="mla_qkv",
    )(lat, cos_h, sin_h, rope_r, q_norm_g[None, :], kv_norm_g[None, :], wq, wkn, place, wvt)


def _mla_weights(q_up, kv_up):
    dq = MLA_NOPE + MLA_ROPE
    half = MLA_ROPE // 2
    qh = q_up.reshape(MLA_Q_LORA, MLA_HEADS, dq)
    zeros = jnp.zeros((MLA_Q_LORA, MLA_HEADS, MLA_HEAD_PAD - dq), F32)
    main = jnp.concatenate([qh, zeros], axis=-1)
    x1 = qh[..., MLA_NOPE:MLA_NOPE + half]
    x2 = qh[..., MLA_NOPE + half:]
    rot = jnp.concatenate([jnp.zeros((MLA_Q_LORA, MLA_HEADS, MLA_NOPE), F32), -x2, x1, zeros], axis=-1)
    w_all = MLA_HEADS * MLA_HEAD_PAD
    wq = jnp.concatenate([main.reshape(MLA_Q_LORA, w_all), rot.reshape(MLA_Q_LORA, w_all)], axis=-1)

    kvh = kv_up.reshape(MLA_KV_LORA, MLA_HEADS, MLA_NOPE + MLA_DV)
    wkn = jnp.concatenate(
        [kvh[..., :MLA_NOPE], jnp.zeros((MLA_KV_LORA, MLA_HEADS, MLA_HEAD_PAD - MLA_NOPE), F32)],
        axis=-1).reshape(MLA_KV_LORA, w_all)
    wv = jnp.transpose(kvh[..., MLA_NOPE:], (1, 2, 0))
    wvt = jnp.concatenate(
        [wv, jnp.zeros((MLA_HEADS, MLA_VT_ROWS - MLA_DV, MLA_KV_LORA), F32)],
        axis=1).reshape(MLA_HEADS * MLA_VT_ROWS, MLA_KV_LORA)
    r = jnp.arange(MLA_MISC_W)
    cols = jnp.arange(w_all)
    place = ((r[:, None] < 2 * MLA_ROPE)
             & ((cols[None, :] % MLA_HEAD_PAD) == (MLA_NOPE + r[:, None] % MLA_ROPE))).astype(BF16)
    return wq.astype(BF16), wkn.astype(BF16), place, wvt.astype(BF16)


def _rope_tables(s):
    inv = ROPE_THETA ** (-jnp.arange(0, MLA_ROPE, 2, dtype=F32) / MLA_ROPE)
    ang = jnp.arange(s, dtype=F32)[:, None] * inv[None, :]
    cos, sin = jnp.cos(ang), jnp.sin(ang)
    pad = jnp.zeros((s, MLA_HEAD_PAD - MLA_NOPE - MLA_ROPE), F32)
    cos_h = jnp.concatenate([jnp.ones((s, MLA_NOPE), F32), cos, cos, pad], axis=-1)
    sin_h = jnp.concatenate([jnp.zeros((s, MLA_NOPE), F32), sin, sin, pad], axis=-1)
    rope_r = jnp.concatenate([cos, cos, sin, sin, jnp.zeros((s, MLA_MISC_W - 2 * MLA_ROPE), F32)], axis=-1)
    return cos_h, sin_h, rope_r


def _mla_attn_kernel(q_ref, k_ref, vt_ref, o_ref, s_ref):
    n_k = k_ref.shape[1] // ATT_TK
    for qs in range(ATT_TQ // ATT_TQ_SUB):
        rows = slice(qs * ATT_TQ_SUB, (qs + 1) * ATT_TQ_SUB)
        qts = (q_ref[0, rows, 0:MLA_HEAD_PAD], q_ref[0, rows, MLA_HEAD_PAD:2 * MLA_HEAD_PAD])

        def produce(kk, buf):
            r = pl.multiple_of(kk * ATT_TK, ATT_TK)
            tile_max = []
            for hh in range(2):
                kt = k_ref[0, pl.ds(r, ATT_TK), hh * MLA_HEAD_PAD:(hh + 1) * MLA_HEAD_PAD]
                st = _dot_nt(kt, qts[hh])
                s_ref[buf, hh] = st
                tile_max.append(jnp.max(st, axis=0, keepdims=True))
            return tuple(tile_max)

        def consume(kk, buf, tile_max, carry):
            out = []
            for hh in range(2):
                m, acc = carry[2 * hh], carry[2 * hh + 1]
                m_new = jnp.maximum(m, tile_max[hh])
                alpha = jnp.exp2(m - m_new)
                p = jnp.exp2(s_ref[buf, hh] - m_new).astype(BF16)
                acc = acc * alpha + _dot(vt_ref[0, hh, kk], p)
                out += [m_new, acc]
            return tuple(out)

        def run(k0, state, produce_next):
            carry, tmax = state[:4], state[4:]
            for t in range(ATT_UNROLL):
                last = t == ATT_UNROLL - 1
                nxt = produce(k0 + t + 1, (t + 1) % 2) if (produce_next or not last) else ()
                carry = consume(k0 + t, t % 2, tmax, carry)
                tmax = nxt
            return carry + tmax

        m0 = jnp.full((1, ATT_TQ_SUB), -jnp.inf, F32)
        a0 = jnp.zeros((MLA_VT_ROWS, ATT_TQ_SUB), F32)
        state = lax.fori_loop(0, n_k // ATT_UNROLL - 1, lambda j, st: run(ATT_UNROLL * j, st, True),
                              (m0, a0, m0, a0) + produce(0, 0))
        carry = run(n_k - ATT_UNROLL, state, False)
        outs = [carry[2 * hh + 1][:MLA_DV] / carry[2 * hh + 1][MLA_DV:MLA_DV + 1] for hh in range(2)]
        o_ref[0, rows, :] = jnp.concatenate(outs, axis=0).T.astype(o_ref.dtype)


def _mla_attn(q, k, vt):
    bsz, s, _ = q.shape
    n_kt = s // ATT_TK
    assert s % (ATT_TK * ATT_UNROLL) == 0 and s % ATT_TQ == 0 and ATT_UNROLL % 2 == 0
    return pl.pallas_call(
        _mla_attn_kernel,
        grid=(bsz, MLA_HEADS // 2, s // ATT_TQ),
        in_specs=[
            pl.BlockSpec((1, ATT_TQ, 2 * MLA_HEAD_PAD), lambda b, h, i: (b, i, h)),
            pl.BlockSpec((1, s, 2 * MLA_HEAD_PAD), lambda b, h, i: (b, 0, h)),
            pl.BlockSpec((1, 2, n_kt, MLA_VT_ROWS, ATT_TK), lambda b, h, i: (b, h, 0, 0, 0))],
        out_specs=pl.BlockSpec((1, ATT_TQ, 2 * MLA_DV), lambda b, h, i: (b, i, h)),
        out_shape=jax.ShapeDtypeStruct((bsz, s, MLA_W), BF16),
        scratch_shapes=[pltpu.VMEM((2, 2, ATT_TK, ATT_TQ_SUB), F32)],
        compiler_params=_cparams(("parallel", "parallel", "arbitrary")),
        name="mla_attn",
    )(q, k, vt)


def _lane_scan(x, op, identity, reverse):
    lane = lax.broadcasted_iota(jnp.int32, x.shape, 1)
    step = 1
    while step < ML_CHUNK:
        if reverse:
            shifted = jnp.where(lane < ML_CHUNK - step, pltpu.roll(x, ML_CHUNK - step, 1), identity)
        else:
            shifted = jnp.where(lane >= step, pltpu.roll(x, step, 1), identity)
        x = op(x, shifted)
        step *= 2
    return x


def _mlstm_prologue(grf, grb, bias_r, eye, tri_pre, tri_suf, b_ref, cmax_ref, ctot_ref, gtot_ref, ccol_ref):
    n_chunks = ML_SEQ_TILE // ML_CHUNK
    n_dir = 2 * ML_HEADS
    rows16 = lax.broadcasted_iota(jnp.int32, (1, 2 * n_dir, 1), 1)
    g = jnp.where((rows16 % n_dir) < ML_HEADS, grf[0], grb[0]) + bias_r[...][None]
    li = g[:, :n_dir].reshape(n_chunks * n_dir, ML_CHUNK)
    lf = _log_sigmoid(g[:, n_dir:]).reshape(n_chunks * n_dir, ML_CHUNK)
    is_fwd = (lax.broadcasted_iota(jnp.int32, li.shape, 0) % n_dir) < ML_HEADS
    lane = lax.broadcasted_iota(jnp.int32, li.shape, 1)
    b = jnp.where(is_fwd, _tri_right(lf, tri_pre), _tri_right(lf, tri_suf))
    c = li - b
    c_next = jnp.where(lane < ML_CHUNK - 1, pltpu.roll(c, ML_CHUNK - 1, 1), -jnp.inf)
    cmax = jnp.where(is_fwd, _lane_scan(c, jnp.maximum, -jnp.inf, False),
                     _lane_scan(c_next, jnp.maximum, -jnp.inf, True))
    shape3 = (n_chunks, n_dir, ML_CHUNK)
    b_ref[...] = b.reshape(shape3)
    cmax_ref[...] = cmax.reshape(shape3)
    ctot_ref[...] = jnp.broadcast_to(jnp.max(c, axis=-1, keepdims=True), c.shape).reshape(shape3)
    gtot_ref[...] = jnp.broadcast_to(jnp.sum(lf, axis=-1, keepdims=True), c.shape).reshape(shape3)
    hi, mid, lo = _split3(c)
    for jc in range(n_chunks):
        rs = slice(jc * n_dir, (jc + 1) * n_dir)
        ccol_ref[jc] = _dot_nt(eye, hi[rs]) + _dot_nt(eye, mid[rs]) + _dot_nt(eye, lo[rs])


def _mlstm_chunk(q_ref, k_ref, v_ref, o_ref, cnt_ref, m_ref, n, row, jc, stats, inclusive):
    b_ref, cmax_ref, ctot_ref, gtot_ref, ccol_ref = stats
    h = n % ML_HEADS
    hs = slice(h * ML_DH, (h + 1) * ML_DH)
    jj = lax.broadcasted_iota(jnp.int32, (ML_CHUNK, ML_CHUNK), 0)
    ii = lax.broadcasted_iota(jnp.int32, (ML_CHUNK, ML_CHUNK), 1)
    mask = (jj <= ii) if inclusive else (jj > ii)

    m = m_ref[n]
    b_r = b_ref[jc, n:n + 1, :]
    big_m = jnp.maximum(m, cmax_ref[jc, n:n + 1, :])
    m_tot = jnp.maximum(m, ctot_ref[jc, n:n + 1, 0:1])
    g_tot = gtot_ref[jc, n:n + 1, 0:1]
    c_b = jnp.broadcast_to(ccol_ref[jc, :, n:n + 1], (ML_CHUNK, ML_CHUNK))

    q = q_ref[0, pl.ds(row, ML_CHUNK), hs]
    ks = k_ref[0, pl.ds(row, ML_CHUNK), hs] * (ML_DH ** -0.5)
    v = v_ref[0, pl.ds(row, ML_CHUNK), hs]
    ones_row = (lax.broadcasted_iota(jnp.int32, (ML_DH, ML_CHUNK), 0) == 0).astype(BF16)
    vt_ext = jnp.concatenate([v.T, ones_row], axis=0)

    cnt = cnt_ref[n]
    lhs = jnp.concatenate([ks.astype(BF16), cnt.astype(BF16)], axis=0)
    sr = _dot_nt(lhs, q)
    pt = jnp.exp(jnp.where(mask, c_b - big_m, -jnp.inf))
    st = (sr[:ML_CHUNK] * pt).astype(BF16)
    kw = (ks * jnp.exp(c_b - m_tot)).astype(BF16)
    ho = _dot(vt_ext, jnp.concatenate([st, kw], axis=-1))
    ht = ho[:, :ML_CHUNK] + jnp.exp(m - big_m) * sr[ML_CHUNK:]
    den = ht[ML_DH:ML_DH + 1]
    scale = 1.0 / jnp.maximum(jnp.abs(den), jnp.exp(-(b_r + big_m)))
    o_ref[0, pl.ds(row, ML_CHUNK), hs] = (ht[:ML_DH] * scale).T.astype(o_ref.dtype)
    cnt_ref[n] = jnp.exp(m - m_tot) * cnt + ho[:, ML_CHUNK:]
    m_ref[n] = g_tot + m_tot


def _mlstm_kernel(qf, kf, vf, grf, qb, kb, vb, grb, bias_r, eye_ref, tri_pre, tri_suf, of, ob,
                  cnt_ref, m_ref, b_ref, cmax_ref, ctot_ref, gtot_ref, ccol_ref):
    c = pl.program_id(1)

    @pl.when(c == 0)
    def _():
        cnt_ref[...] = jnp.zeros_like(cnt_ref)
        m_ref[...] = jnp.zeros_like(m_ref)

    stats = (b_ref, cmax_ref, ctot_ref, gtot_ref, ccol_ref)
    _mlstm_prologue(grf, grb, bias_r, eye_ref[...], tri_pre[...], tri_suf[...], *stats)
    n_chunks = ML_SEQ_TILE // ML_CHUNK

    def body(j, carry):
        jb = n_chunks - 1 - j
        rf = pl.multiple_of(j * ML_CHUNK, ML_CHUNK)
        rb = pl.multiple_of(jb * ML_CHUNK, ML_CHUNK)
        for h in range(ML_HEADS):
            _mlstm_chunk(qf, kf, vf, of, cnt_ref, m_ref, h, rf, j, stats, True)
            _mlstm_chunk(qb, kb, vb, ob, cnt_ref, m_ref, ML_HEADS + h, rb, jb, stats, False)
        return carry

    lax.fori_loop(0, n_chunks, body, 0)


def _mlstm(q, k, v, gates, if_bias):
    bsz, s, _ = q.shape
    ts = ML_SEQ_TILE
    ns = s // ts
    n_chunks = ts // ML_CHUNK
    n_gate = 4 * ML_HEADS
    n_dir = 2 * ML_HEADS
    gates_r = gates.reshape(bsz, s // ML_CHUNK, ML_CHUNK, n_gate).transpose(0, 1, 3, 2)
    fwd = lambda b, c: (b, c, 0)
    bwd = lambda b, c: (b, ns - 1 - c, 0)
    fwd4 = lambda b, c: (b, c, 0, 0)
    bwd4 = lambda b, c: (b, ns - 1 - c, 0, 0)

    def seq_specs(imap, imap4):
        return [pl.BlockSpec((1, ts, ML_W), imap), pl.BlockSpec((1, ts, ML_W), imap),
                pl.BlockSpec((1, ts, ML_W), imap),
                pl.BlockSpec((1, n_chunks, n_gate, ML_CHUNK), imap4)]

    in_specs = seq_specs(fwd, fwd4) + seq_specs(bwd, bwd4) + [
        _const_spec((n_gate, ML_CHUNK))] + [_const_spec((ML_CHUNK, ML_CHUNK))] * 3
    idx = jnp.arange(ML_CHUNK)
    tri_pre = (idx[:, None] <= idx[None, :]).astype(BF16)
    tri_suf = (idx[:, None] >= idx[None, :]).astype(BF16)
    out_specs = [pl.BlockSpec((1, ts, ML_W), fwd), pl.BlockSpec((1, ts, ML_W), bwd)]
    out_shape = [jax.ShapeDtypeStruct((bsz, s, ML_W), BF16)] * 2
    stat = pltpu.VMEM((n_chunks, n_dir, ML_CHUNK), F32)
    return pl.pallas_call(
        _mlstm_kernel,
        grid=(bsz, ns),
        in_specs=in_specs,
        out_specs=out_specs,
        out_shape=out_shape,
        scratch_shapes=[pltpu.VMEM((n_dir, 2 * ML_DH, ML_DH), F32), pltpu.VMEM((n_dir, 1, 1), F32),
                        stat, stat, stat, stat, pltpu.VMEM((n_chunks, ML_CHUNK, n_dir), F32)],
        compiler_params=_cparams(("parallel", "arbitrary")),
        name="mlstm_scan",
    )(q, k, v, gates_r, q, k, v, gates_r,
      jnp.broadcast_to(if_bias[:, None], (n_gate, ML_CHUNK)), jnp.eye(ML_CHUNK, dtype=BF16),
      tri_pre, tri_suf)


def _odd_out_kernel(final, att, mgate, hf, hb, mo, lgate, ng, wa, wb, x, fg, o_ref):
    mla_out = (att[0].astype(F32) * _silu(mgate[0].astype(F32))).astype(BF16)
    hm = hf[0].astype(F32) + hb[0].astype(F32)
    mo_v, lg_v = mo[0].astype(F32), lgate[0].astype(F32)
    parts = []
    for h in range(ML_HEADS):
        hs = slice(h * ML_DH, (h + 1) * ML_DH)
        y = _rms(hm[:, hs], ng[...]) * _sigmoid(mo_v[:, hs])
        parts.append((y * _silu(lg_v[:, hs])).astype(BF16))
    ml_out = jnp.concatenate(parts, axis=-1)
    xn = x[0] + (_dot(mla_out, wa[...]) + _dot(ml_out, wb[...]))
    o_ref[0] = _rms(xn, fg[...]) if final else xn


def _odd_out(att, mla_gate, h_f, h_b, mo, ml_gate, norm_g, w_out, x, final_g, final):
    bsz, s, _ = x.shape
    tm = TAIL_TILE
    cur = lambda b, i: (b, i, 0)
    half = pl.BlockSpec((1, tm, MLA_W), cur)
    in_specs = [half] * 6 + [
        _const_spec((1, ML_DH)), _const_spec((MLA_W, D_MODEL)), _const_spec((ML_W, D_MODEL)),
        pl.BlockSpec((1, tm, D_MODEL), cur), _const_spec((1, D_MODEL))]
    return pl.pallas_call(
        functools.partial(_odd_out_kernel, final),
        grid=(bsz, s // tm),
        in_specs=in_specs,
        out_specs=pl.BlockSpec((1, tm, D_MODEL), cur),
        out_shape=jax.ShapeDtypeStruct((bsz, s, D_MODEL), F32),
        compiler_params=_cparams(("parallel", "parallel")),
        name="odd_out",
    )(att, mla_gate, h_f, h_b, mo, ml_gate, norm_g[None, :],
      w_out[:MLA_W].astype(BF16), w_out[MLA_W:].astype(BF16), x, final_g[None, :])


def _col_split(w, sizes):
    out, off = [], 0
    for n in sizes:
        out.append(w[:, off:off + n].astype(BF16))
        off += n
    return out


def _even_layer(x, g, w_in, a_up, a_bias, gla_norm_g, pool_w, pool_scale, w_out):
    bsz, s, _ = x.shape
    sizes = (GLA_K_TOT, GLA_K_TOT, GLA_V_TOT, GLA_V_TOT, 2 * GLA_LR, POOL_W, POOL_W)
    dts = (F32, F32, BF16, BF16, BF16, F32, BF16)
    outs = _norm_proj(x.reshape(bsz * s, D_MODEL), g[None, :], _col_split(w_in, sizes), dts)
    q, k, v, gate, lr, pool_u, pool_gate = [o.reshape(bsz, s, -1) for o in outs]
    o_f, o_b = _gla(q, k, v, lr, a_up, a_bias)
    return _even_out(o_f, o_b, gate, gla_norm_g, pool_u, pool_gate, pool_w, pool_scale, w_out, x)


def _odd_layer(x, g, tabs, w_in, q_norm_g, q_up, kv_norm_g, kv_up, if_bias, ml_norm_g, w_out,
               final_g, final):
    bsz, s, _ = x.shape
    half = MLA_ROPE // 2
    off = MLA_Q_LORA + MLA_KV_LORA
    kr_w = w_in[:, off:off + MLA_ROPE]
    n_gate = 4 * ML_HEADS
    gate_off = off + MLA_ROPE + MLA_W + 4 * ML_W
    w_lat = jnp.concatenate(
        [w_in[:, :off + MLA_ROPE], -kr_w[:, half:], kr_w[:, :half], w_in[:, gate_off:gate_off + n_gate],
         jnp.zeros((D_MODEL, MLA_MISC_W - 2 * MLA_ROPE - n_gate), F32)], axis=-1).astype(BF16)
    wide = _col_split(w_in[:, off + MLA_ROPE:gate_off], (MLA_W, ML_W, ML_W, ML_W, ML_W))
    wide.append(w_in[:, gate_off + n_gate:].astype(BF16))
    dts = (F32, BF16, BF16, F32, BF16, BF16, BF16)
    outs = _norm_proj(x.reshape(bsz * s, D_MODEL), g[None, :], [w_lat] + wide, dts)
    lat, mla_gate, mq, mk, mv, mo, ml_gate = [o.reshape(bsz, s, -1) for o in outs]
    mif = lat[:, :, off + 2 * MLA_ROPE:off + 2 * MLA_ROPE + n_gate]
    wq, wkn, place, wvt = _mla_weights(q_up, kv_up)
    qa, ka, vt = _mla_qkv(lat, tabs, q_norm_g, kv_norm_g, wq, wkn, place, wvt)
    att = _mla_attn(qa, ka, vt)
    h_f, h_b = _mlstm(mq, mk, mv, mif, if_bias)
    return _odd_out(att, mla_gate, h_f, h_b, mo, ml_gate, ml_norm_g, w_out, x, final_g, final)


def _trunk(x, norm_g, final_norm_g, e_w_in, e_gla_a_up, e_gla_a_bias, e_gla_norm_g, e_pool_w,
           e_pool_scale, e_w_out, o_w_in, o_q_norm_g, o_q_up, o_kv_norm_g, o_kv_up, o_if_bias,
           o_mlstm_norm_g, o_w_out):
    depth = norm_g.shape[0]
    assert depth % 2 == 0, "the final RMSNorm is fused into the last (odd) layer's tail"
    tabs = _rope_tables(x.shape[1])
    for layer in range(depth):
        i = layer // 2
        if layer % 2 == 0:
            x = _even_layer(x, norm_g[layer], e_w_in[i], e_gla_a_up[i], e_gla_a_bias[i],
                            e_gla_norm_g[i], e_pool_w[i], e_pool_scale[i], e_w_out[i])
        else:
            x = _odd_layer(x, norm_g[layer], tabs, o_w_in[i], o_q_norm_g[i], o_q_up[i],
                           o_kv_norm_g[i], o_kv_up[i], o_if_bias[i], o_mlstm_norm_g[i], o_w_out[i],
                           final_norm_g, layer == depth - 1)
    return x


def kernel(x_prompt, x_sample, norm_g, final_norm_g, e_w_in, e_gla_a_up, e_gla_a_bias, e_gla_norm_g,
           e_pool_w, e_pool_scale, e_w_out, o_w_in, o_q_norm_g, o_q_up, o_kv_norm_g, o_kv_up,
           o_if_bias, o_mlstm_norm_g, o_w_out):
    params = (norm_g, final_norm_g, e_w_in, e_gla_a_up, e_gla_a_bias, e_gla_norm_g, e_pool_w,
              e_pool_scale, e_w_out, o_w_in, o_q_norm_g, o_q_up, o_kv_norm_g, o_kv_up, o_if_bias,
              o_mlstm_norm_g, o_w_out)
    return (_trunk(x_prompt, *params), _trunk(x_sample, *params))
```

```python
import functools
import math

import jax
import jax.numpy as jnp
from jax import lax
from jax.experimental import pallas as pl
from jax.experimental.pallas import tpu as pltpu

F32 = jnp.float32
BF16 = jnp.bfloat16

D_MODEL = 1024
NORM_EPS = 1e-6
CHUNK = 64

GLA_HEADS = 4
GLA_DK = 128
GLA_DV = 256
GLA_LR = 16
GLA_GATE_NORM = 16.0
GLA_K_TOT = GLA_HEADS * GLA_DK
GLA_V_TOT = GLA_HEADS * GLA_DV
GLA_BLOCK = 2 * CHUNK

POOL_GROUPS = 4
POOL_WINDOWS = (2, 4, 8, 16)
POOL_DG = 128
POOL_W = POOL_GROUPS * POOL_DG
POOL_HALO = 8

MLA_HEADS = 8
MLA_NOPE = 64
MLA_ROPE = 32
MLA_DV = 64
MLA_Q_LORA = 384
MLA_KV_LORA = 256
MLA_W = MLA_HEADS * MLA_DV
MLA_HEAD_PAD = 128
MLA_VT_ROWS = 80
MLA_MISC_W = 128
MLA_LAT_W = MLA_Q_LORA + MLA_KV_LORA + MLA_MISC_W
ROPE_THETA = 10000.0

ML_HEADS = 4
ML_DH = 128
ML_W = ML_HEADS * ML_DH
ML_CHUNK = 128

VMEM_LIMIT_BYTES = 56 * 1024 * 1024

TOKEN_TILE = 512
PROJ_TILE = 1024
TAIL_TILE = 1024
SEQ_TILE = 1024
ML_SEQ_TILE = 1024
ATT_TQ = 1024
ATT_TQ_SUB = 512
ATT_TK = 256
ATT_UNROLL = 8


def _cparams(sem):
    return pltpu.CompilerParams(dimension_semantics=sem, vmem_limit_bytes=VMEM_LIMIT_BYTES)


def _const_spec(shape):
    nd = len(shape)
    return pl.BlockSpec(shape, lambda *_: (0,) * nd)


def _dot(a, b):
    return jnp.dot(a, b, preferred_element_type=F32)


def _dot_nt(a, b):
    return lax.dot_general(a, b, (((1,), (1,)), ((), ())), preferred_element_type=F32)


def _dot_tn(a, b):
    return lax.dot_general(a, b, (((0,), (0,)), ((), ())), preferred_element_type=F32)


def _sigmoid(x):
    return 1.0 / (1.0 + jnp.exp(-x))


def _silu(x):
    return x * _sigmoid(x)


def _log_sigmoid(x):
    return jnp.minimum(x, 0.0) - jnp.log(1.0 + jnp.exp(-jnp.abs(x)))


def _rms(x, g):
    return x * lax.rsqrt(jnp.mean(x * x, axis=-1, keepdims=True) + NORM_EPS) * g


def _split3(x):
    hi = x.astype(BF16)
    r1 = x - hi.astype(F32)
    mid = r1.astype(BF16)
    lo = (r1 - mid.astype(F32)).astype(BF16)
    return hi, mid, lo


def _tri_left(tri, x):
    hi, mid, lo = _split3(x)
    return _dot(tri, hi) + _dot(tri, mid) + _dot(tri, lo)


def _tri_right(x, tri):
    hi, mid, lo = _split3(x)
    return _dot(hi, tri) + _dot(mid, tri) + _dot(lo, tri)


def _norm_proj_kernel(n_out, x_ref, g_ref, *refs):
    w_refs, o_refs = refs[:n_out], refs[n_out:]
    h = _rms(x_ref[...], g_ref[...]).astype(BF16)
    for w_ref, o_ref in zip(w_refs, o_refs):
        o_ref[...] = _dot(h, w_ref[...]).astype(o_ref.dtype)


def _norm_proj(x2d, g, weights, out_dtypes):
    t = x2d.shape[0]
    tm = PROJ_TILE
    n_out = len(weights)
    in_specs = [pl.BlockSpec((tm, D_MODEL), lambda i: (i, 0)), _const_spec((1, D_MODEL))]
    in_specs += [_const_spec(w.shape) for w in weights]
    out_specs = [pl.BlockSpec((tm, w.shape[1]), lambda i: (i, 0)) for w in weights]
    out_shape = [jax.ShapeDtypeStruct((t, w.shape[1]), dt) for w, dt in zip(weights, out_dtypes)]
    return pl.pallas_call(
        functools.partial(_norm_proj_kernel, n_out),
        grid=(t // tm,),
        in_specs=in_specs,
        out_specs=out_specs,
        out_shape=out_shape,
        compiler_params=_cparams(("parallel",)),
        name="norm_proj",
    )(x2d, g, *weights)


def _gla_block(q_ref, k_ref, v_ref, o_ref, st_ref, st_idx, row, b_all, inclusive):
    ii = lax.broadcasted_iota(jnp.int32, (GLA_BLOCK, GLA_BLOCK), 0)
    jj = lax.broadcasted_iota(jnp.int32, (GLA_BLOCK, GLA_BLOCK), 1)
    rr = lax.broadcasted_iota(jnp.int32, (GLA_BLOCK, 1), 0)
    same = (ii >= CHUNK) == (jj >= CHUNK)
    if inclusive:
        intra, cross, far_rows = same & (jj <= ii), (ii >= CHUNK) & (jj < CHUNK), rr >= CHUNK
        edge_lo, edge_hi = CHUNK - 1, GLA_BLOCK - 1
    else:
        intra, cross, far_rows = same & (jj > ii), (ii < CHUNK) & (jj >= CHUNK), rr < CHUNK
        edge_lo, edge_hi = 0, CHUNK
    for h in range(GLA_HEADS):
        ks = slice(h * GLA_DK, (h + 1) * GLA_DK)
        vs = slice(h * GLA_DV, (h + 1) * GLA_DV)
        q = q_ref[0, pl.ds(row, GLA_BLOCK), ks] * (GLA_DK ** -0.5)
        k = k_ref[0, pl.ds(row, GLA_BLOCK), ks]
        v = v_ref[0, pl.ds(row, GLA_BLOCK), vs]
        b = b_all[:, ks]
        tot_lo, tot_hi = b[edge_lo:edge_lo + 1, :], b[edge_hi:edge_hi + 1, :]
        tot_near, tot_far = (tot_lo, tot_hi) if inclusive else (tot_hi, tot_lo)
        qe = q * jnp.exp(b)
        kd = (k * jnp.exp(-b)).astype(BF16)
        kdec = k * jnp.exp(jnp.where(rr < CHUNK, tot_lo, tot_hi) - b)
        pp = _dot_nt(qe.astype(BF16), jnp.concatenate([kd, kdec.astype(BF16)], axis=0))
        a = jnp.where(intra, pp[:, :GLA_BLOCK], jnp.where(cross, pp[:, GLA_BLOCK:], 0.0)).astype(BF16)
        qe_in = jnp.where(far_rows, qe * jnp.exp(tot_near), qe).astype(BF16)
        kdec_out = jnp.where(far_rows, kdec, kdec * jnp.exp(tot_far)).astype(BF16)
        st = st_ref[st_idx + h]
        o_ref[0, pl.ds(row, GLA_BLOCK), vs] = (_dot(a, v) + _dot_nt(qe_in, st.astype(BF16))).astype(o_ref.dtype)
        st_ref[st_idx + h] = st * jnp.exp(tot_near + tot_far) + _dot_tn(v, kdec_out)


def _gla_kernel(qf, kf, vf, lrf, qb, kb, vb, lrb, aupf, aupb, biasf, biasb, tril, triu,
                of, ob, st_ref, la_ref):
    c = pl.program_id(1)

    @pl.when(c == 0)
    def _():
        st_ref[...] = jnp.zeros_like(st_ref)

    inv = 1.0 / GLA_GATE_NORM
    la_ref[0] = _log_sigmoid(_dot(lrf[0], aupf[...]) + biasf[...]) * inv
    la_ref[1] = _log_sigmoid(_dot(lrb[0], aupb[...]) + biasb[...]) * inv
    n_blocks = SEQ_TILE // GLA_BLOCK

    def body(j, carry):
        rf = pl.multiple_of(j * GLA_BLOCK, GLA_BLOCK)
        rb = pl.multiple_of((n_blocks - 1 - j) * GLA_BLOCK, GLA_BLOCK)
        b_f = _tri_left(tril[...], la_ref[0, pl.ds(rf, GLA_BLOCK), :])
        b_b = _tri_left(triu[...], la_ref[1, pl.ds(rb, GLA_BLOCK), :])
        _gla_block(qf, kf, vf, of, st_ref, 0, rf, b_f, True)
        _gla_block(qb, kb, vb, ob, st_ref, GLA_HEADS, rb, b_b, False)
        return carry

    lax.fori_loop(0, n_blocks, body, 0)


def _gla(q, k, v, lr, a_up, a_bias):
    bsz, s, _ = q.shape
    ts = SEQ_TILE
    ns = s // ts
    fwd = lambda b, c: (b, c, 0)
    bwd = lambda b, c: (b, ns - 1 - c, 0)
    idx = jnp.arange(GLA_BLOCK)
    same_chunk = (idx[None, :] // CHUNK) == (idx[:, None] // CHUNK)
    tril = (same_chunk & (idx[None, :] <= idx[:, None])).astype(BF16)
    triu = (same_chunk & (idx[None, :] >= idx[:, None])).astype(BF16)

    def seq_specs(imap):
        return [pl.BlockSpec((1, ts, GLA_K_TOT), imap), pl.BlockSpec((1, ts, GLA_K_TOT), imap),
                pl.BlockSpec((1, ts, GLA_V_TOT), imap), pl.BlockSpec((1, ts, 2 * GLA_LR), imap)]

    in_specs = seq_specs(fwd) + seq_specs(bwd) + [
        _const_spec((2 * GLA_LR, GLA_K_TOT)), _const_spec((2 * GLA_LR, GLA_K_TOT)),
        _const_spec((1, GLA_K_TOT)), _const_spec((1, GLA_K_TOT)),
        _const_spec((GLA_BLOCK, GLA_BLOCK)), _const_spec((GLA_BLOCK, GLA_BLOCK))]
    out_specs = [pl.BlockSpec((1, ts, GLA_V_TOT), fwd), pl.BlockSpec((1, ts, GLA_V_TOT), bwd)]
    out_shape = [jax.ShapeDtypeStruct((bsz, s, GLA_V_TOT), BF16)] * 2
    zeros = jnp.zeros((GLA_LR, GLA_K_TOT), F32)
    aup_f = jnp.concatenate([a_up[0], zeros], axis=0).astype(BF16)
    aup_b = jnp.concatenate([zeros, a_up[1]], axis=0).astype(BF16)
    return pl.pallas_call(
        _gla_kernel,
        grid=(bsz, ns),
        in_specs=in_specs,
        out_specs=out_specs,
        out_shape=out_shape,
        scratch_shapes=[pltpu.VMEM((2 * GLA_HEADS, GLA_DV, GLA_DK), F32),
                        pltpu.VMEM((2, ts, GLA_K_TOT), F32)],
        compiler_params=_cparams(("parallel", "arbitrary")),
        name="gla_scan",
    )(q, k, v, lr, q, k, v, lr, aup_f, aup_b, a_bias[0:1], a_bias[1:2], tril, triu)


def _even_out_kernel(seq_len, of, ob, gate, ng, pu, pprev, pnext, pgate, pw, pscale, wa, wb, x, o_ref):
    tm = of.shape[1]
    i = pl.program_id(1)
    n_i = pl.num_programs(1)

    o = of[0].astype(F32) + ob[0].astype(F32)
    g_all = gate[0].astype(F32)
    parts = []
    for h in range(GLA_HEADS):
        vs = slice(h * GLA_DV, (h + 1) * GLA_DV)
        parts.append((_rms(o[:, vs], ng[...]) * _silu(g_all[:, vs])).astype(BF16))
    gla_out = jnp.concatenate(parts, axis=-1)

    u = pu[0]
    prev = jnp.where(i > 0, pprev[0], 0.0)
    nxt = jnp.where(i < n_i - 1, pnext[0], 0.0)
    ext = jnp.concatenate([prev, u, nxt], axis=0)
    n_ext = tm + 2 * POOL_HALO
    pos = i * tm + lax.broadcasted_iota(jnp.int32, (tm, 1), 0)
    mixed = []
    for gi, w in enumerate(POOL_WINDOWS):
        cs = slice(gi * POOL_DG, (gi + 1) * POOL_DG)
        a = ext[:, cs]
        span = 1
        while span < w:
            a = a + pltpu.roll(a, span, 0)
            span *= 2
        shift = w // 2 - 1
        if shift:
            a = pltpu.roll(a, n_ext - shift, 0)
        win = a[POOL_HALO:POOL_HALO + tm]
        lo = jnp.maximum(pos - w // 2, 0)
        hi = jnp.minimum(pos + w // 2, seq_len)
        pooled = win / (hi - lo).astype(F32) - u[:, cs]
        mixed.append(_dot(pooled.astype(BF16), pw[gi]))
    pool_out = (jnp.concatenate(mixed, axis=-1) * pscale[...] * _silu(pgate[0].astype(F32))).astype(BF16)

    y = _dot(gla_out, wa[...]) + _dot(pool_out, wb[...])
    o_ref[0] = x[0] + y


def _even_out(o_f, o_b, gate, norm_g, pool_u, pool_gate, pool_w, pool_scale, w_out, x):
    bsz, s, _ = x.shape
    tm = TOKEN_TILE
    nh = tm // POOL_HALO
    n_halo = s // POOL_HALO
    cur = lambda b, i: (b, i, 0)
    in_specs = [
        pl.BlockSpec((1, tm, GLA_V_TOT), cur), pl.BlockSpec((1, tm, GLA_V_TOT), cur),
        pl.BlockSpec((1, tm, GLA_V_TOT), cur), _const_spec((1, GLA_DV)),
        pl.BlockSpec((1, tm, POOL_W), cur),
        pl.BlockSpec((1, POOL_HALO, POOL_W), lambda b, i: (b, jnp.maximum(i * nh - 1, 0), 0)),
        pl.BlockSpec((1, POOL_HALO, POOL_W), lambda b, i: (b, jnp.minimum((i + 1) * nh, n_halo - 1), 0)),
        pl.BlockSpec((1, tm, POOL_W), cur),
        _const_spec((POOL_GROUPS, POOL_DG, POOL_DG)), _const_spec((1, POOL_W)),
        _const_spec((GLA_V_TOT, D_MODEL)), _const_spec((POOL_W, D_MODEL)),
        pl.BlockSpec((1, tm, D_MODEL), cur)]
    return pl.pallas_call(
        functools.partial(_even_out_kernel, s),
        grid=(bsz, s // tm),
        in_specs=in_specs,
        out_specs=pl.BlockSpec((1, tm, D_MODEL), cur),
        out_shape=jax.ShapeDtypeStruct((bsz, s, D_MODEL), F32),
        compiler_params=_cparams(("parallel", "parallel")),
        name="even_out",
    )(o_f, o_b, gate, norm_g[None, :], pool_u, pool_u, pool_u, pool_gate,
      pool_w.astype(BF16), pool_scale[None, :],
      w_out[:GLA_V_TOT].astype(BF16), w_out[GLA_V_TOT:].astype(BF16), x)


def _mla_qkv_kernel(lat, cos_h, sin_h, rope_r, qg, kvg, wq, wkn, place, wvt, q_ref, k_ref, vt_ref):
    qk_scale = math.log2(math.e) * (MLA_NOPE + MLA_ROPE) ** -0.5
    w_all = MLA_HEADS * MLA_HEAD_PAD
    x = lat[0]
    cq = x[:, :MLA_Q_LORA]
    ckv = x[:, MLA_Q_LORA:MLA_Q_LORA + MLA_KV_LORA]
    nq = _rms(cq, qg[...]).astype(BF16)
    qq = _dot(nq, wq[...])
    cos2 = jnp.concatenate([cos_h[...], cos_h[...]], axis=-1)
    sin2 = jnp.concatenate([sin_h[...], sin_h[...]], axis=-1)
    for p in range(MLA_HEADS // 2):
        sl = slice(2 * p * MLA_HEAD_PAD, (2 * p + 2) * MLA_HEAD_PAD)
        sr = slice(w_all + 2 * p * MLA_HEAD_PAD, w_all + (2 * p + 2) * MLA_HEAD_PAD)
        q_ref[0, :, sl] = ((qq[:, sl] * cos2 + qq[:, sr] * sin2) * qk_scale).astype(BF16)

    nkv = _rms(ckv, kvg[...]).astype(BF16)
    k_rope = (x[:, MLA_Q_LORA + MLA_KV_LORA:] * rope_r[...]).astype(BF16)
    k_ref[0] = (_dot(nkv, wkn[...]) + _dot(k_rope, place[...])).astype(BF16)

    vt = _dot_nt(wvt[...], nkv)
    ones_row = lax.broadcasted_iota(jnp.int32, (MLA_VT_ROWS, 1), 0) == MLA_DV
    for h in range(MLA_HEADS):
        vh = vt[h * MLA_VT_ROWS:(h + 1) * MLA_VT_ROWS]
        vh = jnp.where(ones_row, 1.0, vh).astype(BF16)
        for c in range(vt_ref.shape[2]):
            vt_ref[0, h, c] = vh[:, c * ATT_TK:(c + 1) * ATT_TK]


def _mla_qkv(lat, tabs, q_norm_g, kv_norm_g, wq, wkn, place, wvt):
    bsz, s, _ = lat.shape
    tm = PROJ_TILE
    n_sub = tm // ATT_TK
    w_all = MLA_HEADS * MLA_HEAD_PAD
    cur = lambda b, i: (b, i, 0)
    tab = lambda b, i: (i, 0)
    cos_h, sin_h, rope_r = tabs
    in_specs = [
        pl.BlockSpec((1, tm, MLA_LAT_W), cur),
        pl.BlockSpec((tm, MLA_HEAD_PAD), tab), pl.BlockSpec((tm, MLA_HEAD_PAD), tab),
        pl.BlockSpec((tm, MLA_MISC_W), tab),
        _const_spec((1, MLA_Q_LORA)), _const_spec((1, MLA_KV_LORA)),
        _const_spec(wq.shape), _const_spec(wkn.shape), _const_spec(place.shape), _const_spec(wvt.shape)]
    out_specs = [
        pl.BlockSpec((1, tm, w_all), cur), pl.BlockSpec((1, tm, w_all), cur),
        pl.BlockSpec((1, MLA_HEADS, n_sub, MLA_VT_ROWS, ATT_TK), lambda b, i: (b, 0, i, 0, 0))]
    out_shape = [
        jax.ShapeDtypeStruct((bsz, s, w_all), BF16), jax.ShapeDtypeStruct((bsz, s, w_all), BF16),
        jax.ShapeDtypeStruct((bsz, MLA_HEADS, s // ATT_TK, MLA_VT_ROWS, ATT_TK), BF16)]
    return pl.pallas_call(
        _mla_qkv_kernel,
        grid=(bsz, s // tm),
        in_specs=in_specs,
        out_specs=out_specs,
        out_shape=out_shape,
        compiler_params=_cparams(("parallel", "parallel")),
        name="mla_qkv",
    )(lat, cos_h, sin_h, rope_r, q_norm_g[None, :], kv_norm_g[None, :], wq, wkn, place, wvt)


def _mla_weights(q_up, kv_up):
    dq = MLA_NOPE + MLA_ROPE
    half = MLA_ROPE // 2
    qh = q_up.reshape(MLA_Q_LORA, MLA_HEADS, dq)
    zeros = jnp.zeros((MLA_Q_LORA, MLA_HEADS, MLA_HEAD_PAD - dq), F32)
    main = jnp.concatenate([qh, zeros], axis=-1)
    x1 = qh[..., MLA_NOPE:MLA_NOPE + half]
    x2 = qh[..., MLA_NOPE + half:]
    rot = jnp.concatenate([jnp.zeros((MLA_Q_LORA, MLA_HEADS, MLA_NOPE), F32), -x2, x1, zeros], axis=-1)
    w_all = MLA_HEADS * MLA_HEAD_PAD
    wq = jnp.concatenate([main.reshape(MLA_Q_LORA, w_all), rot.reshape(MLA_Q_LORA, w_all)], axis=-1)

    kvh = kv_up.reshape(MLA_KV_LORA, MLA_HEADS, MLA_NOPE + MLA_DV)
    wkn = jnp.concatenate(
        [kvh[..., :MLA_NOPE], jnp.zeros((MLA_KV_LORA, MLA_HEADS, MLA_HEAD_PAD - MLA_NOPE), F32)],
        axis=-1).reshape(MLA_KV_LORA, w_all)
    wv = jnp.transpose(kvh[..., MLA_NOPE:], (1, 2, 0))
    wvt = jnp.concatenate(
        [wv, jnp.zeros((MLA_HEADS, MLA_VT_ROWS - MLA_DV, MLA_KV_LORA), F32)],
        axis=1).reshape(MLA_HEADS * MLA_VT_ROWS, MLA_KV_LORA)
    r = jnp.arange(MLA_MISC_W)
    cols = jnp.arange(w_all)
    place = ((r[:, None] < 2 * MLA_ROPE)
             & ((cols[None, :] % MLA_HEAD_PAD) == (MLA_NOPE + r[:, None] % MLA_ROPE))).astype(BF16)
    return wq.astype(BF16), wkn.astype(BF16), place, wvt.astype(BF16)


def _rope_tables(s):
    inv = ROPE_THETA ** (-jnp.arange(0, MLA_ROPE, 2, dtype=F32) / MLA_ROPE)
    ang = jnp.arange(s, dtype=F32)[:, None] * inv[None, :]
    cos, sin = jnp.cos(ang), jnp.sin(ang)
    pad = jnp.zeros((s, MLA_HEAD_PAD - MLA_NOPE - MLA_ROPE), F32)
    cos_h = jnp.concatenate([jnp.ones((s, MLA_NOPE), F32), cos, cos, pad], axis=-1)
    sin_h = jnp.concatenate([jnp.zeros((s, MLA_NOPE), F32), sin, sin, pad], axis=-1)
    rope_r = jnp.concatenate([cos, cos, sin, sin, jnp.zeros((s, MLA_MISC_W - 2 * MLA_ROPE), F32)], axis=-1)
    return cos_h, sin_h, rope_r


def _mla_attn_kernel(q_ref, k_ref, vt_ref, o_ref, s_ref):
    n_k = k_ref.shape[1] // ATT_TK
    for qs in range(ATT_TQ // ATT_TQ_SUB):
        rows = slice(qs * ATT_TQ_SUB, (qs + 1) * ATT_TQ_SUB)
        qts = (q_ref[0, rows, 0:MLA_HEAD_PAD], q_ref[0, rows, MLA_HEAD_PAD:2 * MLA_HEAD_PAD])

        def produce(kk, buf):
            r = pl.multiple_of(kk * ATT_TK, ATT_TK)
            tile_max = []
            for hh in range(2):
                kt = k_ref[0, pl.ds(r, ATT_TK), hh * MLA_HEAD_PAD:(hh + 1) * MLA_HEAD_PAD]
                st = _dot_nt(kt, qts[hh])
                s_ref[buf, hh] = st
                tile_max.append(jnp.max(st, axis=0, keepdims=True))
            return tuple(tile_max)

        def consume(kk, buf, tile_max, carry):
            out = []
            for hh in range(2):
                m, acc = carry[2 * hh], carry[2 * hh + 1]
                m_new = jnp.maximum(m, tile_max[hh])
                alpha = jnp.exp2(m - m_new)
                p = jnp.exp2(s_ref[buf, hh] - m_new).astype(BF16)
                acc = acc * alpha + _dot(vt_ref[0, hh, kk], p)
                out += [m_new, acc]
            return tuple(out)

        def run(k0, state, produce_next):
            carry, tmax = state[:4], state[4:]
            for t in range(ATT_UNROLL):
                last = t == ATT_UNROLL - 1
                nxt = produce(k0 + t + 1, (t + 1) % 2) if (produce_next or not last) else ()
                carry = consume(k0 + t, t % 2, tmax, carry)
                tmax = nxt
            return carry + tmax

        m0 = jnp.full((1, ATT_TQ_SUB), -jnp.inf, F32)
        a0 = jnp.zeros((MLA_VT_ROWS, ATT_TQ_SUB), F32)
        state = lax.fori_loop(0, n_k // ATT_UNROLL - 1, lambda j, st: run(ATT_UNROLL * j, st, True),
                              (m0, a0, m0, a0) + produce(0, 0))
        carry = run(n_k - ATT_UNROLL, state, False)
        outs = [carry[2 * hh + 1][:MLA_DV] / carry[2 * hh + 1][MLA_DV:MLA_DV + 1] for hh in range(2)]
        o_ref[0, rows, :] = jnp.concatenate(outs, axis=0).T.astype(o_ref.dtype)


def _mla_attn(q, k, vt):
    bsz, s, _ = q.shape
    n_kt = s // ATT_TK
    assert s % (ATT_TK * ATT_UNROLL) == 0 and s % ATT_TQ == 0 and ATT_UNROLL % 2 == 0
    return pl.pallas_call(
        _mla_attn_kernel,
        grid=(bsz, MLA_HEADS // 2, s // ATT_TQ),
        in_specs=[
            pl.BlockSpec((1, ATT_TQ, 2 * MLA_HEAD_PAD), lambda b, h, i: (b, i, h)),
            pl.BlockSpec((1, s, 2 * MLA_HEAD_PAD), lambda b, h, i: (b, 0, h)),
            pl.BlockSpec((1, 2, n_kt, MLA_VT_ROWS, ATT_TK), lambda b, h, i: (b, h, 0, 0, 0))],
        out_specs=pl.BlockSpec((1, ATT_TQ, 2 * MLA_DV), lambda b, h, i: (b, i, h)),
        out_shape=jax.ShapeDtypeStruct((bsz, s, MLA_W), BF16),
        scratch_shapes=[pltpu.VMEM((2, 2, ATT_TK, ATT_TQ_SUB), F32)],
        compiler_params=_cparams(("parallel", "parallel", "arbitrary")),
        name="mla_attn",
    )(q, k, vt)


def _lane_scan(x, op, identity, reverse):
    lane = lax.broadcasted_iota(jnp.int32, x.shape, 1)
    step = 1
    while step < ML_CHUNK:
        if reverse:
            shifted = jnp.where(lane < ML_CHUNK - step, pltpu.roll(x, ML_CHUNK - step, 1), identity)
        else:
            shifted = jnp.where(lane >= step, pltpu.roll(x, step, 1), identity)
        x = op(x, shifted)
        step *= 2
    return x


def _mlstm_prologue(grf, grb, bias_r, eye, tri_pre, tri_suf, b_ref, cmax_ref, ctot_ref, gtot_ref, ccol_ref):
    n_chunks = ML_SEQ_TILE // ML_CHUNK
    n_dir = 2 * ML_HEADS
    rows16 = lax.broadcasted_iota(jnp.int32, (1, 2 * n_dir, 1), 1)
    g = jnp.where((rows16 % n_dir) < ML_HEADS, grf[0], grb[0]) + bias_r[...][None]
    li = g[:, :n_dir].reshape(n_chunks * n_dir, ML_CHUNK)
    lf = _log_sigmoid(g[:, n_dir:]).reshape(n_chunks * n_dir, ML_CHUNK)
    is_fwd = (lax.broadcasted_iota(jnp.int32, li.shape, 0) % n_dir) < ML_HEADS
    lane = lax.broadcasted_iota(jnp.int32, li.shape, 1)
    b = jnp.where(is_fwd, _tri_right(lf, tri_pre), _tri_right(lf, tri_suf))
    c = li - b
    c_next = jnp.where(lane < ML_CHUNK - 1, pltpu.roll(c, ML_CHUNK - 1, 1), -jnp.inf)
    cmax = jnp.where(is_fwd, _lane_scan(c, jnp.maximum, -jnp.inf, False),
                     _lane_scan(c_next, jnp.maximum, -jnp.inf, True))
    shape3 = (n_chunks, n_dir, ML_CHUNK)
    b_ref[...] = b.reshape(shape3)
    cmax_ref[...] = cmax.reshape(shape3)
    ctot_ref[...] = jnp.broadcast_to(jnp.max(c, axis=-1, keepdims=True), c.shape).reshape(shape3)
    gtot_ref[...] = jnp.broadcast_to(jnp.sum(lf, axis=-1, keepdims=True), c.shape).reshape(shape3)
    hi, mid, lo = _split3(c)
    for jc in range(n_chunks):
        rs = slice(jc * n_dir, (jc + 1) * n_dir)
        ccol_ref[jc] = _dot_nt(eye, hi[rs]) + _dot_nt(eye, mid[rs]) + _dot_nt(eye, lo[rs])


def _mlstm_chunk(q_ref, k_ref, v_ref, o_ref, cnt_ref, m_ref, n, row, jc, stats, inclusive):
    b_ref, cmax_ref, ctot_ref, gtot_ref, ccol_ref = stats
    h = n % ML_HEADS
    hs = slice(h * ML_DH, (h + 1) * ML_DH)
    jj = lax.broadcasted_iota(jnp.int32, (ML_CHUNK, ML_CHUNK), 0)
    ii = lax.broadcasted_iota(jnp.int32, (ML_CHUNK, ML_CHUNK), 1)
    mask = (jj <= ii) if inclusive else (jj > ii)

    m = m_ref[n]
    b_r = b_ref[jc, n:n + 1, :]
    big_m = jnp.maximum(m, cmax_ref[jc, n:n + 1, :])
    m_tot = jnp.maximum(m, ctot_ref[jc, n:n + 1, 0:1])
    g_tot = gtot_ref[jc, n:n + 1, 0:1]
    c_b = jnp.broadcast_to(ccol_ref[jc, :, n:n + 1], (ML_CHUNK, ML_CHUNK))

    q = q_ref[0, pl.ds(row, ML_CHUNK), hs]
    ks = k_ref[0, pl.ds(row, ML_CHUNK), hs] * (ML_DH ** -0.5)
    v = v_ref[0, pl.ds(row, ML_CHUNK), hs]
    ones_row = (lax.broadcasted_iota(jnp.int32, (ML_DH, ML_CHUNK), 0) == 0).astype(BF16)
    vt_ext = jnp.concatenate([v.T, ones_row], axis=0)

    cnt = cnt_ref[n]
    lhs = jnp.concatenate([ks.astype(BF16), cnt.astype(BF16)], axis=0)
    sr = _dot_nt(lhs, q)
    pt = jnp.exp(jnp.where(mask, c_b - big_m, -jnp.inf))
    st = (sr[:ML_CHUNK] * pt).astype(BF16)
    kw = (ks * jnp.exp(c_b - m_tot)).astype(BF16)
    ho = _dot(vt_ext, jnp.concatenate([st, kw], axis=-1))
    ht = ho[:, :ML_CHUNK] + jnp.exp(m - big_m) * sr[ML_CHUNK:]
    den = ht[ML_DH:ML_DH + 1]
    scale = 1.0 / jnp.maximum(jnp.abs(den), jnp.exp(-(b_r + big_m)))
    o_ref[0, pl.ds(row, ML_CHUNK), hs] = (ht[:ML_DH] * scale).T.astype(o_ref.dtype)
    cnt_ref[n] = jnp.exp(m - m_tot) * cnt + ho[:, ML_CHUNK:]
    m_ref[n] = g_tot + m_tot


def _mlstm_kernel(qf, kf, vf, grf, qb, kb, vb, grb, bias_r, eye_ref, tri_pre, tri_suf, of, ob,
                  cnt_ref, m_ref, b_ref, cmax_ref, ctot_ref, gtot_ref, ccol_ref):
    c = pl.program_id(1)

    @pl.when(c == 0)
    def _():
        cnt_ref[...] = jnp.zeros_like(cnt_ref)
        m_ref[...] = jnp.zeros_like(m_ref)

    stats = (b_ref, cmax_ref, ctot_ref, gtot_ref, ccol_ref)
    _mlstm_prologue(grf, grb, bias_r, eye_ref[...], tri_pre[...], tri_suf[...], *stats)
    n_chunks = ML_SEQ_TILE // ML_CHUNK

    def body(j, carry):
        jb = n_chunks - 1 - j
        rf = pl.multiple_of(j * ML_CHUNK, ML_CHUNK)
        rb = pl.multiple_of(jb * ML_CHUNK, ML_CHUNK)
        for h in range(ML_HEADS):
            _mlstm_chunk(qf, kf, vf, of, cnt_ref, m_ref, h, rf, j, stats, True)
            _mlstm_chunk(qb, kb, vb, ob, cnt_ref, m_ref, ML_HEADS + h, rb, jb, stats, False)
        return carry

    lax.fori_loop(0, n_chunks, body, 0)


def _mlstm(q, k, v, gates, if_bias):
    bsz, s, _ = q.shape
    ts = ML_SEQ_TILE
    ns = s // ts
    n_chunks = ts // ML_CHUNK
    n_gate = 4 * ML_HEADS
    n_dir = 2 * ML_HEADS
    gates_r = gates.reshape(bsz, s // ML_CHUNK, ML_CHUNK, n_gate).transpose(0, 1, 3, 2)
    fwd = lambda b, c: (b, c, 0)
    bwd = lambda b, c: (b, ns - 1 - c, 0)
    fwd4 = lambda b, c: (b, c, 0, 0)
    bwd4 = lambda b, c: (b, ns - 1 - c, 0, 0)

    def seq_specs(imap, imap4):
        return [pl.BlockSpec((1, ts, ML_W), imap), pl.BlockSpec((1, ts, ML_W), imap),
                pl.BlockSpec((1, ts, ML_W), imap),
                pl.BlockSpec((1, n_chunks, n_gate, ML_CHUNK), imap4)]

    in_specs = seq_specs(fwd, fwd4) + seq_specs(bwd, bwd4) + [
        _const_spec((n_gate, ML_CHUNK))] + [_const_spec((ML_CHUNK, ML_CHUNK))] * 3
    idx = jnp.arange(ML_CHUNK)
    tri_pre = (idx[:, None] <= idx[None, :]).astype(BF16)
    tri_suf = (idx[:, None] >= idx[None, :]).astype(BF16)
    out_specs = [pl.BlockSpec((1, ts, ML_W), fwd), pl.BlockSpec((1, ts, ML_W), bwd)]
    out_shape = [jax.ShapeDtypeStruct((bsz, s, ML_W), BF16)] * 2
    stat = pltpu.VMEM((n_chunks, n_dir, ML_CHUNK), F32)
    return pl.pallas_call(
        _mlstm_kernel,
        grid=(bsz, ns),
        in_specs=in_specs,
        out_specs=out_specs,
        out_shape=out_shape,
        scratch_shapes=[pltpu.VMEM((n_dir, 2 * ML_DH, ML_DH), F32), pltpu.VMEM((n_dir, 1, 1), F32),
                        stat, stat, stat, stat, pltpu.VMEM((n_chunks, ML_CHUNK, n_dir), F32)],
        compiler_params=_cparams(("parallel", "arbitrary")),
        name="mlstm_scan",
    )(q, k, v, gates_r, q, k, v, gates_r,
      jnp.broadcast_to(if_bias[:, None], (n_gate, ML_CHUNK)), jnp.eye(ML_CHUNK, dtype=BF16),
      tri_pre, tri_suf)


def _odd_out_kernel(final, att, mgate, hf, hb, mo, lgate, ng, wa, wb, x, fg, o_ref):
    mla_out = (att[0].astype(F32) * _silu(mgate[0].astype(F32))).astype(BF16)
    hm = hf[0].astype(F32) + hb[0].astype(F32)
    mo_v, lg_v = mo[0].astype(F32), lgate[0].astype(F32)
    parts = []
    for h in range(ML_HEADS):
        hs = slice(h * ML_DH, (h + 1) * ML_DH)
        y = _rms(hm[:, hs], ng[...]) * _sigmoid(mo_v[:, hs])
        parts.append((y * _silu(lg_v[:, hs])).astype(BF16))
    ml_out = jnp.concatenate(parts, axis=-1)
    xn = x[0] + (_dot(mla_out, wa[...]) + _dot(ml_out, wb[...]))
    o_ref[0] = _rms(xn, fg[...]) if final else xn


def _odd_out(att, mla_gate, h_f, h_b, mo, ml_gate, norm_g, w_out, x, final_g, final):
    bsz, s, _ = x.shape
    tm = TAIL_TILE
    cur = lambda b, i: (b, i, 0)
    half = pl.BlockSpec((1, tm, MLA_W), cur)
    in_specs = [half] * 6 + [
        _const_spec((1, ML_DH)), _const_spec((MLA_W, D_MODEL)), _const_spec((ML_W, D_MODEL)),
        pl.BlockSpec((1, tm, D_MODEL), cur), _const_spec((1, D_MODEL))]
    return pl.pallas_call(
        functools.partial(_odd_out_kernel, final),
        grid=(bsz, s // tm),
        in_specs=in_specs,
        out_specs=pl.BlockSpec((1, tm, D_MODEL), cur),
        out_shape=jax.ShapeDtypeStruct((bsz, s, D_MODEL), F32),
        compiler_params=_cparams(("parallel", "parallel")),
        name="odd_out",
    )(att, mla_gate, h_f, h_b, mo, ml_gate, norm_g[None, :],
      w_out[:MLA_W].astype(BF16), w_out[MLA_W:].astype(BF16), x, final_g[None, :])


def _col_split(w, sizes):
    out, off = [], 0
    for n in sizes:
        out.append(w[:, off:off + n].astype(BF16))
        off += n
    return out


def _even_layer(x, g, w_in, a_up, a_bias, gla_norm_g, pool_w, pool_scale, w_out):
    bsz, s, _ = x.shape
    sizes = (GLA_K_TOT, GLA_K_TOT, GLA_V_TOT, GLA_V_TOT, 2 * GLA_LR, POOL_W, POOL_W)
    dts = (F32, F32, BF16, BF16, BF16, F32, BF16)
    outs = _norm_proj(x.reshape(bsz * s, D_MODEL), g[None, :], _col_split(w_in, sizes), dts)
    q, k, v, gate, lr, pool_u, pool_gate = [o.reshape(bsz, s, -1) for o in outs]
    o_f, o_b = _gla(q, k, v, lr, a_up, a_bias)
    return _even_out(o_f, o_b, gate, gla_norm_g, pool_u, pool_gate, pool_w, pool_scale, w_out, x)


def _odd_layer(x, g, tabs, w_in, q_norm_g, q_up, kv_norm_g, kv_up, if_bias, ml_norm_g, w_out,
               final_g, final):
    bsz, s, _ = x.shape
    half = MLA_ROPE // 2
    off = MLA_Q_LORA + MLA_KV_LORA
    kr_w = w_in[:, off:off + MLA_ROPE]
    n_gate = 4 * ML_HEADS
    gate_off = off + MLA_ROPE + MLA_W + 4 * ML_W
    w_lat = jnp.concatenate(
        [w_in[:, :off + MLA_ROPE], -kr_w[:, half:], kr_w[:, :half], w_in[:, gate_off:gate_off + n_gate],
         jnp.zeros((D_MODEL, MLA_MISC_W - 2 * MLA_ROPE - n_gate), F32)], axis=-1).astype(BF16)
    wide = _col_split(w_in[:, off + MLA_ROPE:gate_off], (MLA_W, ML_W, ML_W, ML_W, ML_W))
    wide.append(w_in[:, gate_off + n_gate:].astype(BF16))
    dts = (F32, BF16, BF16, F32, BF16, BF16, BF16)
    outs = _norm_proj(x.reshape(bsz * s, D_MODEL), g[None, :], [w_lat] + wide, dts)
    lat, mla_gate, mq, mk, mv, mo, ml_gate = [o.reshape(bsz, s, -1) for o in outs]
    mif = lat[:, :, off + 2 * MLA_ROPE:off + 2 * MLA_ROPE + n_gate]
    wq, wkn, place, wvt = _mla_weights(q_up, kv_up)
    qa, ka, vt = _mla_qkv(lat, tabs, q_norm_g, kv_norm_g, wq, wkn, place, wvt)
    att = _mla_attn(qa, ka, vt)
    h_f, h_b = _mlstm(mq, mk, mv, mif, if_bias)
    return _odd_out(att, mla_gate, h_f, h_b, mo, ml_gate, ml_norm_g, w_out, x, final_g, final)


def _trunk(x, norm_g, final_norm_g, e_w_in, e_gla_a_up, e_gla_a_bias, e_gla_norm_g, e_pool_w,
           e_pool_scale, e_w_out, o_w_in, o_q_norm_g, o_q_up, o_kv_norm_g, o_kv_up, o_if_bias,
           o_mlstm_norm_g, o_w_out):
    depth = norm_g.shape[0]
    assert depth % 2 == 0, "the final RMSNorm is fused into the last (odd) layer's tail"
    tabs = _rope_tables(x.shape[1])
    for layer in range(depth):
        i = layer // 2
        if layer % 2 == 0:
            x = _even_layer(x, norm_g[layer], e_w_in[i], e_gla_a_up[i], e_gla_a_bias[i],
                            e_gla_norm_g[i], e_pool_w[i], e_pool_scale[i], e_w_out[i])
        else:
            x = _odd_layer(x, norm_g[layer], tabs, o_w_in[i], o_q_norm_g[i], o_q_up[i],
                           o_kv_norm_g[i], o_kv_up[i], o_if_bias[i], o_mlstm_norm_g[i], o_w_out[i],
                           final_norm_g, layer == depth - 1)
    return x


def kernel(x_prompt, x_sample, norm_g, final_norm_g, e_w_in, e_gla_a_up, e_gla_a_bias, e_gla_norm_g,
           e_pool_w, e_pool_scale, e_w_out, o_w_in, o_q_norm_g, o_q_up, o_kv_norm_g, o_kv_up,
           o_if_bias, o_mlstm_norm_g, o_w_out):
    params = (norm_g, final_norm_g, e_w_in, e_gla_a_up, e_gla_a_bias, e_gla_norm_g, e_pool_w,
              e_pool_scale, e_w_out, o_w_in, o_q_norm_g, o_q_up, o_kv_norm_g, o_kv_up, o_if_bias,
              o_mlstm_norm_g, o_w_out)
    return (_trunk(x_prompt, *params), _trunk(x_sample, *params))
```

```python
import functools
import math

import jax
import jax.numpy as jnp
from jax import lax
from jax.experimental import pallas as pl
from jax.experimental.pallas import tpu as pltpu

F32 = jnp.float32
BF16 = jnp.bfloat16

D_MODEL = 1024
NORM_EPS = 1e-6
CHUNK = 64

GLA_HEADS = 4
GLA_DK = 128
GLA_DV = 256
GLA_LR = 16
GLA_GATE_NORM = 16.0
GLA_K_TOT = GLA_HEADS * GLA_DK
GLA_V_TOT = GLA_HEADS * GLA_DV
GLA_BLOCK = 2 * CHUNK

POOL_GROUPS = 4
POOL_WINDOWS = (2, 4, 8, 16)
POOL_DG = 128
POOL_W = POOL_GROUPS * POOL_DG
POOL_HALO = 8

MLA_HEADS = 8
MLA_NOPE = 64
MLA_ROPE = 32
MLA_DV = 64
MLA_Q_LORA = 384
MLA_KV_LORA = 256
MLA_W = MLA_HEADS * MLA_DV
MLA_HEAD_PAD = 128
MLA_VT_ROWS = 80
MLA_MISC_W = 128
MLA_LAT_W = MLA_Q_LORA + MLA_KV_LORA + MLA_MISC_W
ROPE_THETA = 10000.0

ML_HEADS = 4
ML_DH = 128
ML_W = ML_HEADS * ML_DH
ML_CHUNK = 128

VMEM_LIMIT_BYTES = 56 * 1024 * 1024

TOKEN_TILE = 512
PROJ_TILE = 1024
TAIL_TILE = 1024
SEQ_TILE = 1024
ML_SEQ_TILE = 2048
ATT_TQ = 1024
ATT_TQ_SUB = 512
ATT_TK = 256
ATT_UNROLL = 8


def _cparams(sem):
    return pltpu.CompilerParams(dimension_semantics=sem, vmem_limit_bytes=VMEM_LIMIT_BYTES)


def _const_spec(shape):
    nd = len(shape)
    return pl.BlockSpec(shape, lambda *_: (0,) * nd)


def _dot(a, b):
    return jnp.dot(a, b, preferred_element_type=F32)


def _dot_nt(a, b):
    return lax.dot_general(a, b, (((1,), (1,)), ((), ())), preferred_element_type=F32)


def _dot_tn(a, b):
    return lax.dot_general(a, b, (((0,), (0,)), ((), ())), preferred_element_type=F32)


def _sigmoid(x):
    return 1.0 / (1.0 + jnp.exp(-x))


def _silu(x):
    return x * _sigmoid(x)


def _log_sigmoid(x):
    return jnp.minimum(x, 0.0) - jnp.log(1.0 + jnp.exp(-jnp.abs(x)))


def _rms(x, g):
    return x * lax.rsqrt(jnp.mean(x * x, axis=-1, keepdims=True) + NORM_EPS) * g


def _split3(x):
    hi = x.astype(BF16)
    r1 = x - hi.astype(F32)
    mid = r1.astype(BF16)
    lo = (r1 - mid.astype(F32)).astype(BF16)
    return hi, mid, lo


def _tri_left(tri, x):
    hi, mid, lo = _split3(x)
    return _dot(tri, hi) + _dot(tri, mid) + _dot(tri, lo)


def _tri_right(x, tri):
    hi, mid, lo = _split3(x)
    return _dot(hi, tri) + _dot(mid, tri) + _dot(lo, tri)


def _norm_proj_kernel(n_out, x_ref, g_ref, *refs):
    w_refs, o_refs = refs[:n_out], refs[n_out:]
    h = _rms(x_ref[...], g_ref[...]).astype(BF16)
    for w_ref, o_ref in zip(w_refs, o_refs):
        o_ref[...] = _dot(h, w_ref[...]).astype(o_ref.dtype)


def _norm_proj(x2d, g, weights, out_dtypes):
    t = x2d.shape[0]
    tm = PROJ_TILE
    n_out = len(weights)
    in_specs = [pl.BlockSpec((tm, D_MODEL), lambda i: (i, 0)), _const_spec((1, D_MODEL))]
    in_specs += [_const_spec(w.shape) for w in weights]
    out_specs = [pl.BlockSpec((tm, w.shape[1]), lambda i: (i, 0)) for w in weights]
    out_shape = [jax.ShapeDtypeStruct((t, w.shape[1]), dt) for w, dt in zip(weights, out_dtypes)]
    return pl.pallas_call(
        functools.partial(_norm_proj_kernel, n_out),
        grid=(t // tm,),
        in_specs=in_specs,
        out_specs=out_specs,
        out_shape=out_shape,
        compiler_params=_cparams(("parallel",)),
        name="norm_proj",
    )(x2d, g, *weights)


def _gla_block(q_ref, k_ref, v_ref, o_ref, st_ref, st_idx, row, b_all, inclusive):
    ii = lax.broadcasted_iota(jnp.int32, (GLA_BLOCK, GLA_BLOCK), 0)
    jj = lax.broadcasted_iota(jnp.int32, (GLA_BLOCK, GLA_BLOCK), 1)
    rr = lax.broadcasted_iota(jnp.int32, (GLA_BLOCK, 1), 0)
    same = (ii >= CHUNK) == (jj >= CHUNK)
    if inclusive:
        intra, cross, far_rows = same & (jj <= ii), (ii >= CHUNK) & (jj < CHUNK), rr >= CHUNK
        edge_lo, edge_hi = CHUNK - 1, GLA_BLOCK - 1
    else:
        intra, cross, far_rows = same & (jj > ii), (ii < CHUNK) & (jj >= CHUNK), rr < CHUNK
        edge_lo, edge_hi = 0, CHUNK
    for h in range(GLA_HEADS):
        ks = slice(h * GLA_DK, (h + 1) * GLA_DK)
        vs = slice(h * GLA_DV, (h + 1) * GLA_DV)
        q = q_ref[0, pl.ds(row, GLA_BLOCK), ks] * (GLA_DK ** -0.5)
        k = k_ref[0, pl.ds(row, GLA_BLOCK), ks]
        v = v_ref[0, pl.ds(row, GLA_BLOCK), vs]
        b = b_all[:, ks]
        tot_lo, tot_hi = b[edge_lo:edge_lo + 1, :], b[edge_hi:edge_hi + 1, :]
        tot_near, tot_far = (tot_lo, tot_hi) if inclusive else (tot_hi, tot_lo)
        qe = q * jnp.exp(b)
        kd = (k * jnp.exp(-b)).astype(BF16)
        kdec = k * jnp.exp(jnp.where(rr < CHUNK, tot_lo, tot_hi) - b)
        pp = _dot_nt(qe.astype(BF16), jnp.concatenate([kd, kdec.astype(BF16)], axis=0))
        a = jnp.where(intra, pp[:, :GLA_BLOCK], jnp.where(cross, pp[:, GLA_BLOCK:], 0.0)).astype(BF16)
        qe_in = jnp.where(far_rows, qe * jnp.exp(tot_near), qe).astype(BF16)
        kdec_out = jnp.where(far_rows, kdec, kdec * jnp.exp(tot_far)).astype(BF16)
        st = st_ref[st_idx + h]
        o_ref[0, pl.ds(row, GLA_BLOCK), vs] = (_dot(a, v) + _dot_nt(qe_in, st.astype(BF16))).astype(o_ref.dtype)
        st_ref[st_idx + h] = st * jnp.exp(tot_near + tot_far) + _dot_tn(v, kdec_out)


def _gla_kernel(qf, kf, vf, lrf, qb, kb, vb, lrb, aupf, aupb, biasf, biasb, tril, triu,
                of, ob, st_ref, la_ref):
    c = pl.program_id(1)

    @pl.when(c == 0)
    def _():
        st_ref[...] = jnp.zeros_like(st_ref)

    inv = 1.0 / GLA_GATE_NORM
    la_ref[0] = _log_sigmoid(_dot(lrf[0], aupf[...]) + biasf[...]) * inv
    la_ref[1] = _log_sigmoid(_dot(lrb[0], aupb[...]) + biasb[...]) * inv
    n_blocks = SEQ_TILE // GLA_BLOCK

    def body(j, carry):
        rf = pl.multiple_of(j * GLA_BLOCK, GLA_BLOCK)
        rb = pl.multiple_of((n_blocks - 1 - j) * GLA_BLOCK, GLA_BLOCK)
        b_f = _tri_left(tril[...], la_ref[0, pl.ds(rf, GLA_BLOCK), :])
        b_b = _tri_left(triu[...], la_ref[1, pl.ds(rb, GLA_BLOCK), :])
        _gla_block(qf, kf, vf, of, st_ref, 0, rf, b_f, True)
        _gla_block(qb, kb, vb, ob, st_ref, GLA_HEADS, rb, b_b, False)
        return carry

    lax.fori_loop(0, n_blocks, body, 0)


def _gla(q, k, v, lr, a_up, a_bias):
    bsz, s, _ = q.shape
    ts = SEQ_TILE
    ns = s // ts
    fwd = lambda b, c: (b, c, 0)
    bwd = lambda b, c: (b, ns - 1 - c, 0)
    idx = jnp.arange(GLA_BLOCK)
    same_chunk = (idx[None, :] // CHUNK) == (idx[:, None] // CHUNK)
    tril = (same_chunk & (idx[None, :] <= idx[:, None])).astype(BF16)
    triu = (same_chunk & (idx[None, :] >= idx[:, None])).astype(BF16)

    def seq_specs(imap):
        return [pl.BlockSpec((1, ts, GLA_K_TOT), imap), pl.BlockSpec((1, ts, GLA_K_TOT), imap),
                pl.BlockSpec((1, ts, GLA_V_TOT), imap), pl.BlockSpec((1, ts, 2 * GLA_LR), imap)]

    in_specs = seq_specs(fwd) + seq_specs(bwd) + [
        _const_spec((2 * GLA_LR, GLA_K_TOT)), _const_spec((2 * GLA_LR, GLA_K_TOT)),
        _const_spec((1, GLA_K_TOT)), _const_spec((1, GLA_K_TOT)),
        _const_spec((GLA_BLOCK, GLA_BLOCK)), _const_spec((GLA_BLOCK, GLA_BLOCK))]
    out_specs = [pl.BlockSpec((1, ts, GLA_V_TOT), fwd), pl.BlockSpec((1, ts, GLA_V_TOT), bwd)]
    out_shape = [jax.ShapeDtypeStruct((bsz, s, GLA_V_TOT), BF16)] * 2
    zeros = jnp.zeros((GLA_LR, GLA_K_TOT), F32)
    aup_f = jnp.concatenate([a_up[0], zeros], axis=0).astype(BF16)
    aup_b = jnp.concatenate([zeros, a_up[1]], axis=0).astype(BF16)
    return pl.pallas_call(
        _gla_kernel,
        grid=(bsz, ns),
        in_specs=in_specs,
        out_specs=out_specs,
        out_shape=out_shape,
        scratch_shapes=[pltpu.VMEM((2 * GLA_HEADS, GLA_DV, GLA_DK), F32),
                        pltpu.VMEM((2, ts, GLA_K_TOT), F32)],
        compiler_params=_cparams(("parallel", "arbitrary")),
        name="gla_scan",
    )(q, k, v, lr, q, k, v, lr, aup_f, aup_b, a_bias[0:1], a_bias[1:2], tril, triu)


def _even_out_kernel(seq_len, of, ob, gate, ng, pu, pprev, pnext, pgate, pw, pscale, wa, wb, x, o_ref):
    tm = of.shape[1]
    i = pl.program_id(1)
    n_i = pl.num_programs(1)

    o = of[0].astype(F32) + ob[0].astype(F32)
    g_all = gate[0].astype(F32)
    parts = []
    for h in range(GLA_HEADS):
        vs = slice(h * GLA_DV, (h + 1) * GLA_DV)
        parts.append((_rms(o[:, vs], ng[...]) * _silu(g_all[:, vs])).astype(BF16))
    gla_out = jnp.concatenate(parts, axis=-1)

    u = pu[0]
    prev = jnp.where(i > 0, pprev[0], 0.0)
    nxt = jnp.where(i < n_i - 1, pnext[0], 0.0)
    ext = jnp.concatenate([prev, u, nxt], axis=0)
    n_ext = tm + 2 * POOL_HALO
    pos = i * tm + lax.broadcasted_iota(jnp.int32, (tm, 1), 0)
    mixed = []
    for gi, w in enumerate(POOL_WINDOWS):
        cs = slice(gi * POOL_DG, (gi + 1) * POOL_DG)
        a = ext[:, cs]
        span = 1
        while span < w:
            a = a + pltpu.roll(a, span, 0)
            span *= 2
        shift = w // 2 - 1
        if shift:
            a = pltpu.roll(a, n_ext - shift, 0)
        win = a[POOL_HALO:POOL_HALO + tm]
        lo = jnp.maximum(pos - w // 2, 0)
        hi = jnp.minimum(pos + w // 2, seq_len)
        pooled = win / (hi - lo).astype(F32) - u[:, cs]
        mixed.append(_dot(pooled.astype(BF16), pw[gi]))
    pool_out = (jnp.concatenate(mixed, axis=-1) * pscale[...] * _silu(pgate[0].astype(F32))).astype(BF16)

    y = _dot(gla_out, wa[...]) + _dot(pool_out, wb[...])
    o_ref[0] = x[0] + y


def _even_out(o_f, o_b, gate, norm_g, pool_u, pool_gate, pool_w, pool_scale, w_out, x):
    bsz, s, _ = x.shape
    tm = TOKEN_TILE
    nh = tm // POOL_HALO
    n_halo = s // POOL_HALO
    cur = lambda b, i: (b, i, 0)
    in_specs = [
        pl.BlockSpec((1, tm, GLA_V_TOT), cur), pl.BlockSpec((1, tm, GLA_V_TOT), cur),
        pl.BlockSpec((1, tm, GLA_V_TOT), cur), _const_spec((1, GLA_DV)),
        pl.BlockSpec((1, tm, POOL_W), cur),
        pl.BlockSpec((1, POOL_HALO, POOL_W), lambda b, i: (b, jnp.maximum(i * nh - 1, 0), 0)),
        pl.BlockSpec((1, POOL_HALO, POOL_W), lambda b, i: (b, jnp.minimum((i + 1) * nh, n_halo - 1), 0)),
        pl.BlockSpec((1, tm, POOL_W), cur),
        _const_spec((POOL_GROUPS, POOL_DG, POOL_DG)), _const_spec((1, POOL_W)),
        _const_spec((GLA_V_TOT, D_MODEL)), _const_spec((POOL_W, D_MODEL)),
        pl.BlockSpec((1, tm, D_MODEL), cur)]
    return pl.pallas_call(
        functools.partial(_even_out_kernel, s),
        grid=(bsz, s // tm),
        in_specs=in_specs,
        out_specs=pl.BlockSpec((1, tm, D_MODEL), cur),
        out_shape=jax.ShapeDtypeStruct((bsz, s, D_MODEL), F32),
        compiler_params=_cparams(("parallel", "parallel")),
        name="even_out",
    )(o_f, o_b, gate, norm_g[None, :], pool_u, pool_u, pool_u, pool_gate,
      pool_w.astype(BF16), pool_scale[None, :],
      w_out[:GLA_V_TOT].astype(BF16), w_out[GLA_V_TOT:].astype(BF16), x)


def _mla_qkv_kernel(lat, cos_h, sin_h, rope_r, qg, kvg, wq, wkn, place, wvt, q_ref, k_ref, vt_ref):
    qk_scale = math.log2(math.e) * (MLA_NOPE + MLA_ROPE) ** -0.5
    w_all = MLA_HEADS * MLA_HEAD_PAD
    x = lat[0]
    cq = x[:, :MLA_Q_LORA]
    ckv = x[:, MLA_Q_LORA:MLA_Q_LORA + MLA_KV_LORA]
    nq = _rms(cq, qg[...]).astype(BF16)
    qq = _dot(nq, wq[...])
    cos2 = jnp.concatenate([cos_h[...], cos_h[...]], axis=-1)
    sin2 = jnp.concatenate([sin_h[...], sin_h[...]], axis=-1)
    for p in range(MLA_HEADS // 2):
        sl = slice(2 * p * MLA_HEAD_PAD, (2 * p + 2) * MLA_HEAD_PAD)
        sr = slice(w_all + 2 * p * MLA_HEAD_PAD, w_all + (2 * p + 2) * MLA_HEAD_PAD)
        q_ref[0, :, sl] = ((qq[:, sl] * cos2 + qq[:, sr] * sin2) * qk_scale).astype(BF16)

    nkv = _rms(ckv, kvg[...]).astype(BF16)
    k_rope = (x[:, MLA_Q_LORA + MLA_KV_LORA:] * rope_r[...]).astype(BF16)
    k_ref[0] = (_dot(nkv, wkn[...]) + _dot(k_rope, place[...])).astype(BF16)

    vt = _dot_nt(wvt[...], nkv)
    ones_row = lax.broadcasted_iota(jnp.int32, (MLA_VT_ROWS, 1), 0) == MLA_DV
    for h in range(MLA_HEADS):
        vh = vt[h * MLA_VT_ROWS:(h + 1) * MLA_VT_ROWS]
        vh = jnp.where(ones_row, 1.0, vh).astype(BF16)
        for c in range(vt_ref.shape[2]):
            vt_ref[0, h, c] = vh[:, c * ATT_TK:(c + 1) * ATT_TK]


def _mla_qkv(lat, tabs, q_norm_g, kv_norm_g, wq, wkn, place, wvt):
    bsz, s, _ = lat.shape
    tm = PROJ_TILE
    n_sub = tm // ATT_TK
    w_all = MLA_HEADS * MLA_HEAD_PAD
    cur = lambda b, i: (b, i, 0)
    tab = lambda b, i: (i, 0)
    cos_h, sin_h, rope_r = tabs
    in_specs = [
        pl.BlockSpec((1, tm, MLA_LAT_W), cur),
        pl.BlockSpec((tm, MLA_HEAD_PAD), tab), pl.BlockSpec((tm, MLA_HEAD_PAD), tab),
        pl.BlockSpec((tm, MLA_MISC_W), tab),
        _const_spec((1, MLA_Q_LORA)), _const_spec((1, MLA_KV_LORA)),
        _const_spec(wq.shape), _const_spec(wkn.shape), _const_spec(place.shape), _const_spec(wvt.shape)]
    out_specs = [
        pl.BlockSpec((1, tm, w_all), cur), pl.BlockSpec((1, tm, w_all), cur),
        pl.BlockSpec((1, MLA_HEADS, n_sub, MLA_VT_ROWS, ATT_TK), lambda b, i: (b, 0, i, 0, 0))]
    out_shape = [
        jax.ShapeDtypeStruct((bsz, s, w_all), BF16), jax.ShapeDtypeStruct((bsz, s, w_all), BF16),
        jax.ShapeDtypeStruct((bsz, MLA_HEADS, s // ATT_TK, MLA_VT_ROWS, ATT_TK), BF16)]
    return pl.pallas_call(
        _mla_qkv_kernel,
        grid=(bsz, s // tm),
        in_specs=in_specs,
        out_specs=out_specs,
        out_shape=out_shape,
        compiler_params=_cparams(("parallel", "parallel")),
        name="mla_qkv",
    )(lat, cos_h, sin_h, rope_r, q_norm_g[None, :], kv_norm_g[None, :], wq, wkn, place, wvt)


def _mla_weights(q_up, kv_up):
    dq = MLA_NOPE + MLA_ROPE
    half = MLA_ROPE // 2
    qh = q_up.reshape(MLA_Q_LORA, MLA_HEADS, dq)
    zeros = jnp.zeros((MLA_Q_LORA, MLA_HEADS, MLA_HEAD_PAD - dq), F32)
    main = jnp.concatenate([qh, zeros], axis=-1)
    x1 = qh[..., MLA_NOPE:MLA_NOPE + half]
    x2 = qh[..., MLA_NOPE + half:]
    rot = jnp.concatenate([jnp.zeros((MLA_Q_LORA, MLA_HEADS, MLA_NOPE), F32), -x2, x1, zeros], axis=-1)
    w_all = MLA_HEADS * MLA_HEAD_PAD
    wq = jnp.concatenate([main.reshape(MLA_Q_LORA, w_all), rot.reshape(MLA_Q_LORA, w_all)], axis=-1)

    kvh = kv_up.reshape(MLA_KV_LORA, MLA_HEADS, MLA_NOPE + MLA_DV)
    wkn = jnp.concatenate(
        [kvh[..., :MLA_NOPE], jnp.zeros((MLA_KV_LORA, MLA_HEADS, MLA_HEAD_PAD - MLA_NOPE), F32)],
        axis=-1).reshape(MLA_KV_LORA, w_all)
    wv = jnp.transpose(kvh[..., MLA_NOPE:], (1, 2, 0))
    wvt = jnp.concatenate(
        [wv, jnp.zeros((MLA_HEADS, MLA_VT_ROWS - MLA_DV, MLA_KV_LORA), F32)],
        axis=1).reshape(MLA_HEADS * MLA_VT_ROWS, MLA_KV_LORA)
    r = jnp.arange(MLA_MISC_W)
    cols = jnp.arange(w_all)
    place = ((r[:, None] < 2 * MLA_ROPE)
             & ((cols[None, :] % MLA_HEAD_PAD) == (MLA_NOPE + r[:, None] % MLA_ROPE))).astype(BF16)
    return wq.astype(BF16), wkn.astype(BF16), place, wvt.astype(BF16)


def _rope_tables(s):
    inv = ROPE_THETA ** (-jnp.arange(0, MLA_ROPE, 2, dtype=F32) / MLA_ROPE)
    ang = jnp.arange(s, dtype=F32)[:, None] * inv[None, :]
    cos, sin = jnp.cos(ang), jnp.sin(ang)
    pad = jnp.zeros((s, MLA_HEAD_PAD - MLA_NOPE - MLA_ROPE), F32)
    cos_h = jnp.concatenate([jnp.ones((s, MLA_NOPE), F32), cos, cos, pad], axis=-1)
    sin_h = jnp.concatenate([jnp.zeros((s, MLA_NOPE), F32), sin, sin, pad], axis=-1)
    rope_r = jnp.concatenate([cos, cos, sin, sin, jnp.zeros((s, MLA_MISC_W - 2 * MLA_ROPE), F32)], axis=-1)
    return cos_h, sin_h, rope_r


def _mla_attn_kernel(q_ref, k_ref, vt_ref, o_ref, s_ref):
    n_k = k_ref.shape[1] // ATT_TK
    for qs in range(ATT_TQ // ATT_TQ_SUB):
        rows = slice(qs * ATT_TQ_SUB, (qs + 1) * ATT_TQ_SUB)
        qts = (q_ref[0, rows, 0:MLA_HEAD_PAD], q_ref[0, rows, MLA_HEAD_PAD:2 * MLA_HEAD_PAD])

        def produce(kk, buf):
            r = pl.multiple_of(kk * ATT_TK, ATT_TK)
            tile_max = []
            for hh in range(2):
                kt = k_ref[0, pl.ds(r, ATT_TK), hh * MLA_HEAD_PAD:(hh + 1) * MLA_HEAD_PAD]
                st = _dot_nt(kt, qts[hh])
                s_ref[buf, hh] = st
                tile_max.append(jnp.max(st, axis=0, keepdims=True))
            return tuple(tile_max)

        def consume(kk, buf, tile_max, carry):
            out = []
            for hh in range(2):
                m, acc = carry[2 * hh], carry[2 * hh + 1]
                m_new = jnp.maximum(m, tile_max[hh])
                alpha = jnp.exp2(m - m_new)
                p = jnp.exp2(s_ref[buf, hh] - m_new).astype(BF16)
                acc = acc * alpha + _dot(vt_ref[0, hh, kk], p)
                out += [m_new, acc]
            return tuple(out)

        def run(k0, state, produce_next):
            carry, tmax = state[:4], state[4:]
            for t in range(ATT_UNROLL):
                last = t == ATT_UNROLL - 1
                nxt = produce(k0 + t + 1, (t + 1) % 2) if (produce_next or not last) else ()
                carry = consume(k0 + t, t % 2, tmax, carry)
                tmax = nxt
            return carry + tmax

        m0 = jnp.full((1, ATT_TQ_SUB), -jnp.inf, F32)
        a0 = jnp.zeros((MLA_VT_ROWS, ATT_TQ_SUB), F32)
        state = lax.fori_loop(0, n_k // ATT_UNROLL - 1, lambda j, st: run(ATT_UNROLL * j, st, True),
                              (m0, a0, m0, a0) + produce(0, 0))
        carry = run(n_k - ATT_UNROLL, state, False)
        outs = [carry[2 * hh + 1][:MLA_DV] / carry[2 * hh + 1][MLA_DV:MLA_DV + 1] for hh in range(2)]
        o_ref[0, rows, :] = jnp.concatenate(outs, axis=0).T.astype(o_ref.dtype)


def _mla_attn(q, k, vt):
    bsz, s, _ = q.shape
    n_kt = s // ATT_TK
    assert s % (ATT_TK * ATT_UNROLL) == 0 and s % ATT_TQ == 0 and ATT_UNROLL % 2 == 0
    return pl.pallas_call(
        _mla_attn_kernel,
        grid=(bsz, MLA_HEADS // 2, s // ATT_TQ),
        in_specs=[
            pl.BlockSpec((1, ATT_TQ, 2 * MLA_HEAD_PAD), lambda b, h, i: (b, i, h)),
            pl.BlockSpec((1, s, 2 * MLA_HEAD_PAD), lambda b, h, i: (b, 0, h)),
            pl.BlockSpec((1, 2, n_kt, MLA_VT_ROWS, ATT_TK), lambda b, h, i: (b, h, 0, 0, 0))],
        out_specs=pl.BlockSpec((1, ATT_TQ, 2 * MLA_DV), lambda b, h, i: (b, i, h)),
        out_shape=jax.ShapeDtypeStruct((bsz, s, MLA_W), BF16),
        scratch_shapes=[pltpu.VMEM((2, 2, ATT_TK, ATT_TQ_SUB), F32)],
        compiler_params=_cparams(("parallel", "parallel", "arbitrary")),
        name="mla_attn",
    )(q, k, vt)


def _lane_scan(x, op, identity, reverse):
    lane = lax.broadcasted_iota(jnp.int32, x.shape, 1)
    step = 1
    while step < ML_CHUNK:
        if reverse:
            shifted = jnp.where(lane < ML_CHUNK - step, pltpu.roll(x, ML_CHUNK - step, 1), identity)
        else:
            shifted = jnp.where(lane >= step, pltpu.roll(x, step, 1), identity)
        x = op(x, shifted)
        step *= 2
    return x


def _mlstm_prologue(grf, grb, bias_r, eye, tri_pre, tri_suf, b_ref, cmax_ref, ctot_ref, gtot_ref, ccol_ref):
    n_chunks = ML_SEQ_TILE // ML_CHUNK
    n_dir = 2 * ML_HEADS
    rows16 = lax.broadcasted_iota(jnp.int32, (1, 2 * n_dir, 1), 1)
    g = jnp.where((rows16 % n_dir) < ML_HEADS, grf[0], grb[0]) + bias_r[...][None]
    li = g[:, :n_dir].reshape(n_chunks * n_dir, ML_CHUNK)
    lf = _log_sigmoid(g[:, n_dir:]).reshape(n_chunks * n_dir, ML_CHUNK)
    is_fwd = (lax.broadcasted_iota(jnp.int32, li.shape, 0) % n_dir) < ML_HEADS
    lane = lax.broadcasted_iota(jnp.int32, li.shape, 1)
    b = jnp.where(is_fwd, _tri_right(lf, tri_pre), _tri_right(lf, tri_suf))
    c = li - b
    c_next = jnp.where(lane < ML_CHUNK - 1, pltpu.roll(c, ML_CHUNK - 1, 1), -jnp.inf)
    cmax = jnp.where(is_fwd, _lane_scan(c, jnp.maximum, -jnp.inf, False),
                     _lane_scan(c_next, jnp.maximum, -jnp.inf, True))
    shape3 = (n_chunks, n_dir, ML_CHUNK)
    b_ref[...] = b.reshape(shape3)
    cmax_ref[...] = cmax.reshape(shape3)
    ctot_ref[...] = jnp.broadcast_to(jnp.max(c, axis=-1, keepdims=True), c.shape).reshape(shape3)
    gtot_ref[...] = jnp.broadcast_to(jnp.sum(lf, axis=-1, keepdims=True), c.shape).reshape(shape3)
    hi, mid, lo = _split3(c)
    for jc in range(n_chunks):
        rs = slice(jc * n_dir, (jc + 1) * n_dir)
        ccol_ref[jc] = _dot_nt(eye, hi[rs]) + _dot_nt(eye, mid[rs]) + _dot_nt(eye, lo[rs])


def _mlstm_chunk(q_ref, k_ref, v_ref, o_ref, cnt_ref, m_ref, n, row, jc, stats, inclusive):
    b_ref, cmax_ref, ctot_ref, gtot_ref, ccol_ref = stats
    h = n % ML_HEADS
    hs = slice(h * ML_DH, (h + 1) * ML_DH)
    jj = lax.broadcasted_iota(jnp.int32, (ML_CHUNK, ML_CHUNK), 0)
    ii = lax.broadcasted_iota(jnp.int32, (ML_CHUNK, ML_CHUNK), 1)
    mask = (jj <= ii) if inclusive else (jj > ii)

    m = m_ref[n]
    b_r = b_ref[jc, n:n + 1, :]
    big_m = jnp.maximum(m, cmax_ref[jc, n:n + 1, :])
    m_tot = jnp.maximum(m, ctot_ref[jc, n:n + 1, 0:1])
    g_tot = gtot_ref[jc, n:n + 1, 0:1]
    c_b = jnp.broadcast_to(ccol_ref[jc, :, n:n + 1], (ML_CHUNK, ML_CHUNK))

    q = q_ref[0, pl.ds(row, ML_CHUNK), hs]
    ks = k_ref[0, pl.ds(row, ML_CHUNK), hs] * (ML_DH ** -0.5)
    v = v_ref[0, pl.ds(row, ML_CHUNK), hs]
    ones_row = (lax.broadcasted_iota(jnp.int32, (ML_DH, ML_CHUNK), 0) == 0).astype(BF16)
    vt_ext = jnp.concatenate([v.T, ones_row], axis=0)

    cnt = cnt_ref[n]
    lhs = jnp.concatenate([ks.astype(BF16), cnt.astype(BF16)], axis=0)
    sr = _dot_nt(lhs, q)
    pt = jnp.exp(jnp.where(mask, c_b - big_m, -jnp.inf))
    st = (sr[:ML_CHUNK] * pt).astype(BF16)
    kw = (ks * jnp.exp(c_b - m_tot)).astype(BF16)
    ho = _dot(vt_ext, jnp.concatenate([st, kw], axis=-1))
    ht = ho[:, :ML_CHUNK] + jnp.exp(m - big_m) * sr[ML_CHUNK:]
    den = ht[ML_DH:ML_DH + 1]
    scale = 1.0 / jnp.maximum(jnp.abs(den), jnp.exp(-(b_r + big_m)))
    o_ref[0, pl.ds(row, ML_CHUNK), hs] = (ht[:ML_DH] * scale).T.astype(o_ref.dtype)
    cnt_ref[n] = jnp.exp(m - m_tot) * cnt + ho[:, ML_CHUNK:]
    m_ref[n] = g_tot + m_tot


def _mlstm_kernel(qf, kf, vf, grf, qb, kb, vb, grb, bias_r, eye_ref, tri_pre, tri_suf, of, ob,
                  cnt_ref, m_ref, b_ref, cmax_ref, ctot_ref, gtot_ref, ccol_ref):
    c = pl.program_id(1)

    @pl.when(c == 0)
    def _():
        cnt_ref[...] = jnp.zeros_like(cnt_ref)
        m_ref[...] = jnp.zeros_like(m_ref)

    stats = (b_ref, cmax_ref, ctot_ref, gtot_ref, ccol_ref)
    _mlstm_prologue(grf, grb, bias_r, eye_ref[...], tri_pre[...], tri_suf[...], *stats)
    n_chunks = ML_SEQ_TILE // ML_CHUNK

    def body(j, carry):
        jb = n_chunks - 1 - j
        rf = pl.multiple_of(j * ML_CHUNK, ML_CHUNK)
        rb = pl.multiple_of(jb * ML_CHUNK, ML_CHUNK)
        for h in range(ML_HEADS):
            _mlstm_chunk(qf, kf, vf, of, cnt_ref, m_ref, h, rf, j, stats, True)
            _mlstm_chunk(qb, kb, vb, ob, cnt_ref, m_ref, ML_HEADS + h, rb, jb, stats, False)
        return carry

    lax.fori_loop(0, n_chunks, body, 0)


def _mlstm(q, k, v, gates, if_bias):
    bsz, s, _ = q.shape
    ts = ML_SEQ_TILE
    ns = s // ts
    n_chunks = ts // ML_CHUNK
    n_gate = 4 * ML_HEADS
    n_dir = 2 * ML_HEADS
    gates_r = gates.reshape(bsz, s // ML_CHUNK, ML_CHUNK, n_gate).transpose(0, 1, 3, 2)
    fwd = lambda b, c: (b, c, 0)
    bwd = lambda b, c: (b, ns - 1 - c, 0)
    fwd4 = lambda b, c: (b, c, 0, 0)
    bwd4 = lambda b, c: (b, ns - 1 - c, 0, 0)

    def seq_specs(imap, imap4):
        return [pl.BlockSpec((1, ts, ML_W), imap), pl.BlockSpec((1, ts, ML_W), imap),
                pl.BlockSpec((1, ts, ML_W), imap),
                pl.BlockSpec((1, n_chunks, n_gate, ML_CHUNK), imap4)]

    in_specs = seq_specs(fwd, fwd4) + seq_specs(bwd, bwd4) + [
        _const_spec((n_gate, ML_CHUNK))] + [_const_spec((ML_CHUNK, ML_CHUNK))] * 3
    idx = jnp.arange(ML_CHUNK)
    tri_pre = (idx[:, None] <= idx[None, :]).astype(BF16)
    tri_suf = (idx[:, None] >= idx[None, :]).astype(BF16)
    out_specs = [pl.BlockSpec((1, ts, ML_W), fwd), pl.BlockSpec((1, ts, ML_W), bwd)]
    out_shape = [jax.ShapeDtypeStruct((bsz, s, ML_W), BF16)] * 2
    stat = pltpu.VMEM((n_chunks, n_dir, ML_CHUNK), F32)
    return pl.pallas_call(
        _mlstm_kernel,
        grid=(bsz, ns),
        in_specs=in_specs,
        out_specs=out_specs,
        out_shape=out_shape,
        scratch_shapes=[pltpu.VMEM((n_dir, 2 * ML_DH, ML_DH), F32), pltpu.VMEM((n_dir, 1, 1), F32),
                        stat, stat, stat, stat, pltpu.VMEM((n_chunks, ML_CHUNK, n_dir), F32)],
        compiler_params=_cparams(("parallel", "arbitrary")),
        name="mlstm_scan",
    )(q, k, v, gates_r, q, k, v, gates_r,
      jnp.broadcast_to(if_bias[:, None], (n_gate, ML_CHUNK)), jnp.eye(ML_CHUNK, dtype=BF16),
      tri_pre, tri_suf)


def _odd_out_kernel(final, att, mgate, hf, hb, mo, lgate, ng, wa, wb, x, fg, o_ref):
    mla_out = (att[0].astype(F32) * _silu(mgate[0].astype(F32))).astype(BF16)
    hm = hf[0].astype(F32) + hb[0].astype(F32)
    mo_v, lg_v = mo[0].astype(F32), lgate[0].astype(F32)
    parts = []
    for h in range(ML_HEADS):
        hs = slice(h * ML_DH, (h + 1) * ML_DH)
        y = _rms(hm[:, hs], ng[...]) * _sigmoid(mo_v[:, hs])
        parts.append((y * _silu(lg_v[:, hs])).astype(BF16))
    ml_out = jnp.concatenate(parts, axis=-1)
    xn = x[0] + (_dot(mla_out, wa[...]) + _dot(ml_out, wb[...]))
    o_ref[0] = _rms(xn, fg[...]) if final else xn


def _odd_out(att, mla_gate, h_f, h_b, mo, ml_gate, norm_g, w_out, x, final_g, final):
    bsz, s, _ = x.shape
    tm = TAIL_TILE
    cur = lambda b, i: (b, i, 0)
    half = pl.BlockSpec((1, tm, MLA_W), cur)
    in_specs = [half] * 6 + [
        _const_spec((1, ML_DH)), _const_spec((MLA_W, D_MODEL)), _const_spec((ML_W, D_MODEL)),
        pl.BlockSpec((1, tm, D_MODEL), cur), _const_spec((1, D_MODEL))]
    return pl.pallas_call(
        functools.partial(_odd_out_kernel, final),
        grid=(bsz, s // tm),
        in_specs=in_specs,
        out_specs=pl.BlockSpec((1, tm, D_MODEL), cur),
        out_shape=jax.ShapeDtypeStruct((bsz, s, D_MODEL), F32),
        compiler_params=_cparams(("parallel", "parallel")),
        name="odd_out",
    )(att, mla_gate, h_f, h_b, mo, ml_gate, norm_g[None, :],
      w_out[:MLA_W].astype(BF16), w_out[MLA_W:].astype(BF16), x, final_g[None, :])


def _col_split(w, sizes):
    out, off = [], 0
    for n in sizes:
        out.append(w[:, off:off + n].astype(BF16))
        off += n
    return out


def _even_layer(x, g, w_in, a_up, a_bias, gla_norm_g, pool_w, pool_scale, w_out):
    bsz, s, _ = x.shape
    sizes = (GLA_K_TOT, GLA_K_TOT, GLA_V_TOT, GLA_V_TOT, 2 * GLA_LR, POOL_W, POOL_W)
    dts = (F32, F32, BF16, BF16, BF16, F32, BF16)
    outs = _norm_proj(x.reshape(bsz * s, D_MODEL), g[None, :], _col_split(w_in, sizes), dts)
    q, k, v, gate, lr, pool_u, pool_gate = [o.reshape(bsz, s, -1) for o in outs]
    o_f, o_b = _gla(q, k, v, lr, a_up, a_bias)
    return _even_out(o_f, o_b, gate, gla_norm_g, pool_u, pool_gate, pool_w, pool_scale, w_out, x)


def _odd_layer(x, g, tabs, w_in, q_norm_g, q_up, kv_norm_g, kv_up, if_bias, ml_norm_g, w_out,
               final_g, final):
    bsz, s, _ = x.shape
    half = MLA_ROPE // 2
    off = MLA_Q_LORA + MLA_KV_LORA
    kr_w = w_in[:, off:off + MLA_ROPE]
    n_gate = 4 * ML_HEADS
    gate_off = off + MLA_ROPE + MLA_W + 4 * ML_W
    w_lat = jnp.concatenate(
        [w_in[:, :off + MLA_ROPE], -kr_w[:, half:], kr_w[:, :half], w_in[:, gate_off:gate_off + n_gate],
         jnp.zeros((D_MODEL, MLA_MISC_W - 2 * MLA_ROPE - n_gate), F32)], axis=-1).astype(BF16)
    wide = _col_split(w_in[:, off + MLA_ROPE:gate_off], (MLA_W, ML_W, ML_W, ML_W, ML_W))
    wide.append(w_in[:, gate_off + n_gate:].astype(BF16))
    dts = (F32, BF16, BF16, F32, BF16, BF16, BF16)
    outs = _norm_proj(x.reshape(bsz * s, D_MODEL), g[None, :], [w_lat] + wide, dts)
    lat, mla_gate, mq, mk, mv, mo, ml_gate = [o.reshape(bsz, s, -1) for o in outs]
    mif = lat[:, :, off + 2 * MLA_ROPE:off + 2 * MLA_ROPE + n_gate]
    wq, wkn, place, wvt = _mla_weights(q_up, kv_up)
    qa, ka, vt = _mla_qkv(lat, tabs, q_norm_g, kv_norm_g, wq, wkn, place, wvt)
    att = _mla_attn(qa, ka, vt)
    h_f, h_b = _mlstm(mq, mk, mv, mif, if_bias)
    return _odd_out(att, mla_gate, h_f, h_b, mo, ml_gate, ml_norm_g, w_out, x, final_g, final)


def _trunk(x, norm_g, final_norm_g, e_w_in, e_gla_a_up, e_gla_a_bias, e_gla_norm_g, e_pool_w,
           e_pool_scale, e_w_out, o_w_in, o_q_norm_g, o_q_up, o_kv_norm_g, o_kv_up, o_if_bias,
           o_mlstm_norm_g, o_w_out):
    depth = norm_g.shape[0]
    assert depth % 2 == 0, "the final RMSNorm is fused into the last (odd) layer's tail"
    tabs = _rope_tables(x.shape[1])
    for layer in range(depth):
        i = layer // 2
        if layer % 2 == 0:
            x = _even_layer(x, norm_g[layer], e_w_in[i], e_gla_a_up[i], e_gla_a_bias[i],
                            e_gla_norm_g[i], e_pool_w[i], e_pool_scale[i], e_w_out[i])
        else:
            x = _odd_layer(x, norm_g[layer], tabs, o_w_in[i], o_q_norm_g[i], o_q_up[i],
                           o_kv_norm_g[i], o_kv_up[i], o_if_bias[i], o_mlstm_norm_g[i], o_w_out[i],
                           final_norm_g, layer == depth - 1)
    return x


def kernel(x_prompt, x_sample, norm_g, final_norm_g, e_w_in, e_gla_a_up, e_gla_a_bias, e_gla_norm_g,
           e_pool_w, e_pool_scale, e_w_out, o_w_in, o_q_norm_g, o_q_up, o_kv_norm_g, o_kv_up,
           o_if_bias, o_mlstm_norm_g, o_w_out):
    params = (norm_g, final_norm_g, e_w_in, e_gla_a_up, e_gla_a_bias, e_gla_norm_g, e_pool_w,
              e_pool_scale, e_w_out, o_w_in, o_q_norm_g, o_q_up, o_kv_norm_g, o_kv_up, o_if_bias,
              o_mlstm_norm_g, o_w_out)
    return (_trunk(x_prompt, *params), _trunk(x_sample, *params))
```

```python
import functools
import math

import jax
import jax.numpy as jnp
from jax import lax
from jax.experimental import pallas as pl
from jax.experimental.pallas import tpu as pltpu

F32 = jnp.float32
BF16 = jnp.bfloat16

D_MODEL = 1024
NORM_EPS = 1e-6
CHUNK = 64

GLA_HEADS = 4
GLA_DK = 128
GLA_DV = 256
GLA_LR = 16
GLA_GATE_NORM = 16.0
GLA_K_TOT = GLA_HEADS * GLA_DK
GLA_V_TOT = GLA_HEADS * GLA_DV
GLA_BLOCK = 2 * CHUNK

POOL_GROUPS = 4
POOL_WINDOWS = (2, 4, 8, 16)
POOL_DG = 128
POOL_W = POOL_GROUPS * POOL_DG
POOL_HALO = 8

MLA_HEADS = 8
MLA_NOPE = 64
MLA_ROPE = 32
MLA_DV = 64
MLA_Q_LORA = 384
MLA_KV_LORA = 256
MLA_W = MLA_HEADS * MLA_DV
MLA_HEAD_PAD = 128
MLA_VT_ROWS = 80
MLA_MISC_W = 128
MLA_LAT_W = MLA_Q_LORA + MLA_KV_LORA + MLA_MISC_W
ROPE_THETA = 10000.0

ML_HEADS = 4
ML_DH = 128
ML_W = ML_HEADS * ML_DH
ML_CHUNK = 128

VMEM_LIMIT_BYTES = 56 * 1024 * 1024

TOKEN_TILE = 512
PROJ_TILE = 1024
TAIL_TILE = 1024
SEQ_TILE = 1024
ML_SEQ_TILE = 2048
ATT_TQ = 2048
ATT_TQ_SUB = 512
ATT_TK = 256
ATT_UNROLL = 8


def _cparams(sem):
    return pltpu.CompilerParams(dimension_semantics=sem, vmem_limit_bytes=VMEM_LIMIT_BYTES)


def _const_spec(shape):
    nd = len(shape)
    return pl.BlockSpec(shape, lambda *_: (0,) * nd)


def _dot(a, b):
    return jnp.dot(a, b, preferred_element_type=F32)


def _dot_nt(a, b):
    return lax.dot_general(a, b, (((1,), (1,)), ((), ())), preferred_element_type=F32)


def _dot_tn(a, b):
    return lax.dot_general(a, b, (((0,), (0,)), ((), ())), preferred_element_type=F32)


def _sigmoid(x):
    return 1.0 / (1.0 + jnp.exp(-x))


def _silu(x):
    return x * _sigmoid(x)


def _log_sigmoid(x):
    return jnp.minimum(x, 0.0) - jnp.log(1.0 + jnp.exp(-jnp.abs(x)))


def _rms(x, g):
    return x * lax.rsqrt(jnp.mean(x * x, axis=-1, keepdims=True) + NORM_EPS) * g


def _split3(x):
    hi = x.astype(BF16)
    r1 = x - hi.astype(F32)
    mid = r1.astype(BF16)
    lo = (r1 - mid.astype(F32)).astype(BF16)
    return hi, mid, lo


def _tri_left(tri, x):
    hi, mid, lo = _split3(x)
    return _dot(tri, hi) + _dot(tri, mid) + _dot(tri, lo)


def _tri_right(x, tri):
    hi, mid, lo = _split3(x)
    return _dot(hi, tri) + _dot(mid, tri) + _dot(lo, tri)


def _norm_proj_kernel(n_out, x_ref, g_ref, *refs):
    w_refs, o_refs = refs[:n_out], refs[n_out:]
    h = _rms(x_ref[...], g_ref[...]).astype(BF16)
    for w_ref, o_ref in zip(w_refs, o_refs):
        o_ref[...] = _dot(h, w_ref[...]).astype(o_ref.dtype)


def _norm_proj(x2d, g, weights, out_dtypes):
    t = x2d.shape[0]
    tm = PROJ_TILE
    n_out = len(weights)
    in_specs = [pl.BlockSpec((tm, D_MODEL), lambda i: (i, 0)), _const_spec((1, D_MODEL))]
    in_specs += [_const_spec(w.shape) for w in weights]
    out_specs = [pl.BlockSpec((tm, w.shape[1]), lambda i: (i, 0)) for w in weights]
    out_shape = [jax.ShapeDtypeStruct((t, w.shape[1]), dt) for w, dt in zip(weights, out_dtypes)]
    return pl.pallas_call(
        functools.partial(_norm_proj_kernel, n_out),
        grid=(t // tm,),
        in_specs=in_specs,
        out_specs=out_specs,
        out_shape=out_shape,
        compiler_params=_cparams(("parallel",)),
        name="norm_proj",
    )(x2d, g, *weights)


def _gla_block(q_ref, k_ref, v_ref, o_ref, st_ref, st_idx, row, b_all, inclusive):
    ii = lax.broadcasted_iota(jnp.int32, (GLA_BLOCK, GLA_BLOCK), 0)
    jj = lax.broadcasted_iota(jnp.int32, (GLA_BLOCK, GLA_BLOCK), 1)
    rr = lax.broadcasted_iota(jnp.int32, (GLA_BLOCK, 1), 0)
    same = (ii >= CHUNK) == (jj >= CHUNK)
    if inclusive:
        intra, cross, far_rows = same & (jj <= ii), (ii >= CHUNK) & (jj < CHUNK), rr >= CHUNK
        edge_lo, edge_hi = CHUNK - 1, GLA_BLOCK - 1
    else:
        intra, cross, far_rows = same & (jj > ii), (ii < CHUNK) & (jj >= CHUNK), rr < CHUNK
        edge_lo, edge_hi = 0, CHUNK
    for h in range(GLA_HEADS):
        ks = slice(h * GLA_DK, (h + 1) * GLA_DK)
        vs = slice(h * GLA_DV, (h + 1) * GLA_DV)
        q = q_ref[0, pl.ds(row, GLA_BLOCK), ks] * (GLA_DK ** -0.5)
        k = k_ref[0, pl.ds(row, GLA_BLOCK), ks]
        v = v_ref[0, pl.ds(row, GLA_BLOCK), vs]
        b = b_all[:, ks]
        tot_lo, tot_hi = b[edge_lo:edge_lo + 1, :], b[edge_hi:edge_hi + 1, :]
        tot_near, tot_far = (tot_lo, tot_hi) if inclusive else (tot_hi, tot_lo)
        qe = q * jnp.exp(b)
        kd = (k * jnp.exp(-b)).astype(BF16)
        kdec = k * jnp.exp(jnp.where(rr < CHUNK, tot_lo, tot_hi) - b)
        pp = _dot_nt(qe.astype(BF16), jnp.concatenate([kd, kdec.astype(BF16)], axis=0))
        a = jnp.where(intra, pp[:, :GLA_BLOCK], jnp.where(cross, pp[:, GLA_BLOCK:], 0.0)).astype(BF16)
        qe_in = jnp.where(far_rows, qe * jnp.exp(tot_near), qe).astype(BF16)
        kdec_out = jnp.where(far_rows, kdec, kdec * jnp.exp(tot_far)).astype(BF16)
        st = st_ref[st_idx + h]
        o_ref[0, pl.ds(row, GLA_BLOCK), vs] = (_dot(a, v) + _dot_nt(qe_in, st.astype(BF16))).astype(o_ref.dtype)
        st_ref[st_idx + h] = st * jnp.exp(tot_near + tot_far) + _dot_tn(v, kdec_out)


def _gla_kernel(qf, kf, vf, lrf, qb, kb, vb, lrb, aupf, aupb, biasf, biasb, tril, triu,
                of, ob, st_ref, la_ref):
    c = pl.program_id(1)

    @pl.when(c == 0)
    def _():
        st_ref[...] = jnp.zeros_like(st_ref)

    inv = 1.0 / GLA_GATE_NORM
    la_ref[0] = _log_sigmoid(_dot(lrf[0], aupf[...]) + biasf[...]) * inv
    la_ref[1] = _log_sigmoid(_dot(lrb[0], aupb[...]) + biasb[...]) * inv
    n_blocks = SEQ_TILE // GLA_BLOCK

    def body(j, carry):
        rf = pl.multiple_of(j * GLA_BLOCK, GLA_BLOCK)
        rb = pl.multiple_of((n_blocks - 1 - j) * GLA_BLOCK, GLA_BLOCK)
        b_f = _tri_left(tril[...], la_ref[0, pl.ds(rf, GLA_BLOCK), :])
        b_b = _tri_left(triu[...], la_ref[1, pl.ds(rb, GLA_BLOCK), :])
        _gla_block(qf, kf, vf, of, st_ref, 0, rf, b_f, True)
        _gla_block(qb, kb, vb, ob, st_ref, GLA_HEADS, rb, b_b, False)
        return carry

    lax.fori_loop(0, n_blocks, body, 0)


def _gla(q, k, v, lr, a_up, a_bias):
    bsz, s, _ = q.shape
    ts = SEQ_TILE
    ns = s // ts
    fwd = lambda b, c: (b, c, 0)
    bwd = lambda b, c: (b, ns - 1 - c, 0)
    idx = jnp.arange(GLA_BLOCK)
    same_chunk = (idx[None, :] // CHUNK) == (idx[:, None] // CHUNK)
    tril = (same_chunk & (idx[None, :] <= idx[:, None])).astype(BF16)
    triu = (same_chunk & (idx[None, :] >= idx[:, None])).astype(BF16)

    def seq_specs(imap):
        return [pl.BlockSpec((1, ts, GLA_K_TOT), imap), pl.BlockSpec((1, ts, GLA_K_TOT), imap),
                pl.BlockSpec((1, ts, GLA_V_TOT), imap), pl.BlockSpec((1, ts, 2 * GLA_LR), imap)]

    in_specs = seq_specs(fwd) + seq_specs(bwd) + [
        _const_spec((2 * GLA_LR, GLA_K_TOT)), _const_spec((2 * GLA_LR, GLA_K_TOT)),
        _const_spec((1, GLA_K_TOT)), _const_spec((1, GLA_K_TOT)),
        _const_spec((GLA_BLOCK, GLA_BLOCK)), _const_spec((GLA_BLOCK, GLA_BLOCK))]
    out_specs = [pl.BlockSpec((1, ts, GLA_V_TOT), fwd), pl.BlockSpec((1, ts, GLA_V_TOT), bwd)]
    out_shape = [jax.ShapeDtypeStruct((bsz, s, GLA_V_TOT), BF16)] * 2
    zeros = jnp.zeros((GLA_LR, GLA_K_TOT), F32)
    aup_f = jnp.concatenate([a_up[0], zeros], axis=0).astype(BF16)
    aup_b = jnp.concatenate([zeros, a_up[1]], axis=0).astype(BF16)
    return pl.pallas_call(
        _gla_kernel,
        grid=(bsz, ns),
        in_specs=in_specs,
        out_specs=out_specs,
        out_shape=out_shape,
        scratch_shapes=[pltpu.VMEM((2 * GLA_HEADS, GLA_DV, GLA_DK), F32),
                        pltpu.VMEM((2, ts, GLA_K_TOT), F32)],
        compiler_params=_cparams(("parallel", "arbitrary")),
        name="gla_scan",
    )(q, k, v, lr, q, k, v, lr, aup_f, aup_b, a_bias[0:1], a_bias[1:2], tril, triu)


def _even_out_kernel(seq_len, of, ob, gate, ng, pu, pprev, pnext, pgate, pw, pscale, wa, wb, x, o_ref):
    tm = of.shape[1]
    i = pl.program_id(1)
    n_i = pl.num_programs(1)

    o = of[0].astype(F32) + ob[0].astype(F32)
    g_all = gate[0].astype(F32)
    parts = []
    for h in range(GLA_HEADS):
        vs = slice(h * GLA_DV, (h + 1) * GLA_DV)
        parts.append((_rms(o[:, vs], ng[...]) * _silu(g_all[:, vs])).astype(BF16))
    gla_out = jnp.concatenate(parts, axis=-1)

    u = pu[0]
    prev = jnp.where(i > 0, pprev[0], 0.0)
    nxt = jnp.where(i < n_i - 1, pnext[0], 0.0)
    ext = jnp.concatenate([prev, u, nxt], axis=0)
    n_ext = tm + 2 * POOL_HALO
    pos = i * tm + lax.broadcasted_iota(jnp.int32, (tm, 1), 0)
    mixed = []
    for gi, w in enumerate(POOL_WINDOWS):
        cs = slice(gi * POOL_DG, (gi + 1) * POOL_DG)
        a = ext[:, cs]
        span = 1
        while span < w:
            a = a + pltpu.roll(a, span, 0)
            span *= 2
        shift = w // 2 - 1
        if shift:
            a = pltpu.roll(a, n_ext - shift, 0)
        win = a[POOL_HALO:POOL_HALO + tm]
        lo = jnp.maximum(pos - w // 2, 0)
        hi = jnp.minimum(pos + w // 2, seq_len)
        pooled = win / (hi - lo).astype(F32) - u[:, cs]
        mixed.append(_dot(pooled.astype(BF16), pw[gi]))
    pool_out = (jnp.concatenate(mixed, axis=-1) * pscale[...] * _silu(pgate[0].astype(F32))).astype(BF16)

    y = _dot(gla_out, wa[...]) + _dot(pool_out, wb[...])
    o_ref[0] = x[0] + y


def _even_out(o_f, o_b, gate, norm_g, pool_u, pool_gate, pool_w, pool_scale, w_out, x):
    bsz, s, _ = x.shape
    tm = TOKEN_TILE
    nh = tm // POOL_HALO
    n_halo = s // POOL_HALO
    cur = lambda b, i: (b, i, 0)
    in_specs = [
        pl.BlockSpec((1, tm, GLA_V_TOT), cur), pl.BlockSpec((1, tm, GLA_V_TOT), cur),
        pl.BlockSpec((1, tm, GLA_V_TOT), cur), _const_spec((1, GLA_DV)),
        pl.BlockSpec((1, tm, POOL_W), cur),
        pl.BlockSpec((1, POOL_HALO, POOL_W), lambda b, i: (b, jnp.maximum(i * nh - 1, 0), 0)),
        pl.BlockSpec((1, POOL_HALO, POOL_W), lambda b, i: (b, jnp.minimum((i + 1) * nh, n_halo - 1), 0)),
        pl.BlockSpec((1, tm, POOL_W), cur),
        _const_spec((POOL_GROUPS, POOL_DG, POOL_DG)), _const_spec((1, POOL_W)),
        _const_spec((GLA_V_TOT, D_MODEL)), _const_spec((POOL_W, D_MODEL)),
        pl.BlockSpec((1, tm, D_MODEL), cur)]
    return pl.pallas_call(
        functools.partial(_even_out_kernel, s),
        grid=(bsz, s // tm),
        in_specs=in_specs,
        out_specs=pl.BlockSpec((1, tm, D_MODEL), cur),
        out_shape=jax.ShapeDtypeStruct((bsz, s, D_MODEL), F32),
        compiler_params=_cparams(("parallel", "parallel")),
        name="even_out",
    )(o_f, o_b, gate, norm_g[None, :], pool_u, pool_u, pool_u, pool_gate,
      pool_w.astype(BF16), pool_scale[None, :],
      w_out[:GLA_V_TOT].astype(BF16), w_out[GLA_V_TOT:].astype(BF16), x)


def _mla_qkv_kernel(lat, cos_h, sin_h, rope_r, qg, kvg, wq, wkn, place, wvt, q_ref, k_ref, vt_ref):
    qk_scale = math.log2(math.e) * (MLA_NOPE + MLA_ROPE) ** -0.5
    w_all = MLA_HEADS * MLA_HEAD_PAD
    x = lat[0]
    cq = x[:, :MLA_Q_LORA]
    ckv = x[:, MLA_Q_LORA:MLA_Q_LORA + MLA_KV_LORA]
    nq = _rms(cq, qg[...]).astype(BF16)
    qq = _dot(nq, wq[...])
    cos2 = jnp.concatenate([cos_h[...], cos_h[...]], axis=-1)
    sin2 = jnp.concatenate([sin_h[...], sin_h[...]], axis=-1)
    for p in range(MLA_HEADS // 2):
        sl = slice(2 * p * MLA_HEAD_PAD, (2 * p + 2) * MLA_HEAD_PAD)
        sr = slice(w_all + 2 * p * MLA_HEAD_PAD, w_all + (2 * p + 2) * MLA_HEAD_PAD)
        q_ref[0, :, sl] = ((qq[:, sl] * cos2 + qq[:, sr] * sin2) * qk_scale).astype(BF16)

    nkv = _rms(ckv, kvg[...]).astype(BF16)
    k_rope = (x[:, MLA_Q_LORA + MLA_KV_LORA:] * rope_r[...]).astype(BF16)
    k_ref[0] = (_dot(nkv, wkn[...]) + _dot(k_rope, place[...])).astype(BF16)

    vt = _dot_nt(wvt[...], nkv)
    ones_row = lax.broadcasted_iota(jnp.int32, (MLA_VT_ROWS, 1), 0) == MLA_DV
    for h in range(MLA_HEADS):
        vh = vt[h * MLA_VT_ROWS:(h + 1) * MLA_VT_ROWS]
        vh = jnp.where(ones_row, 1.0, vh).astype(BF16)
        for c in range(vt_ref.shape[2]):
            vt_ref[0, h, c] = vh[:, c * ATT_TK:(c + 1) * ATT_TK]


def _mla_qkv(lat, tabs, q_norm_g, kv_norm_g, wq, wkn, place, wvt):
    bsz, s, _ = lat.shape
    tm = PROJ_TILE
    n_sub = tm // ATT_TK
    w_all = MLA_HEADS * MLA_HEAD_PAD
    cur = lambda b, i: (b, i, 0)
    tab = lambda b, i: (i, 0)
    cos_h, sin_h, rope_r = tabs
    in_specs = [
        pl.BlockSpec((1, tm, MLA_LAT_W), cur),
        pl.BlockSpec((tm, MLA_HEAD_PAD), tab), pl.BlockSpec((tm, MLA_HEAD_PAD), tab),
        pl.BlockSpec((tm, MLA_MISC_W), tab),
        _const_spec((1, MLA_Q_LORA)), _const_spec((1, MLA_KV_LORA)),
        _const_spec(wq.shape), _const_spec(wkn.shape), _const_spec(place.shape), _const_spec(wvt.shape)]
    out_specs = [
        pl.BlockSpec((1, tm, w_all), cur), pl.BlockSpec((1, tm, w_all), cur),
        pl.BlockSpec((1, MLA_HEADS, n_sub, MLA_VT_ROWS, ATT_TK), lambda b, i: (b, 0, i, 0, 0))]
    out_shape = [
        jax.ShapeDtypeStruct((bsz, s, w_all), BF16), jax.ShapeDtypeStruct((bsz, s, w_all), BF16),
        jax.ShapeDtypeStruct((bsz, MLA_HEADS, s // ATT_TK, MLA_VT_ROWS, ATT_TK), BF16)]
    return pl.pallas_call(
        _mla_qkv_kernel,
        grid=(bsz, s // tm),
        in_specs=in_specs,
        out_specs=out_specs,
        out_shape=out_shape,
        compiler_params=_cparams(("parallel", "parallel")),
        name="mla_qkv",
    )(lat, cos_h, sin_h, rope_r, q_norm_g[None, :], kv_norm_g[None, :], wq, wkn, place, wvt)


def _mla_weights(q_up, kv_up):
    dq = MLA_NOPE + MLA_ROPE
    half = MLA_ROPE // 2
    qh = q_up.reshape(MLA_Q_LORA, MLA_HEADS, dq)
    zeros = jnp.zeros((MLA_Q_LORA, MLA_HEADS, MLA_HEAD_PAD - dq), F32)
    main = jnp.concatenate([qh, zeros], axis=-1)
    x1 = qh[..., MLA_NOPE:MLA_NOPE + half]
    x2 = qh[..., MLA_NOPE + half:]
    rot = jnp.concatenate([jnp.zeros((MLA_Q_LORA, MLA_HEADS, MLA_NOPE), F32), -x2, x1, zeros], axis=-1)
    w_all = MLA_HEADS * MLA_HEAD_PAD
    wq = jnp.concatenate([main.reshape(MLA_Q_LORA, w_all), rot.reshape(MLA_Q_LORA, w_all)], axis=-1)

    kvh = kv_up.reshape(MLA_KV_LORA, MLA_HEADS, MLA_NOPE + MLA_DV)
    wkn = jnp.concatenate(
        [kvh[..., :MLA_NOPE], jnp.zeros((MLA_KV_LORA, MLA_HEADS, MLA_HEAD_PAD - MLA_NOPE), F32)],
        axis=-1).reshape(MLA_KV_LORA, w_all)
    wv = jnp.transpose(kvh[..., MLA_NOPE:], (1, 2, 0))
    wvt = jnp.concatenate(
        [wv, jnp.zeros((MLA_HEADS, MLA_VT_ROWS - MLA_DV, MLA_KV_LORA), F32)],
        axis=1).reshape(MLA_HEADS * MLA_VT_ROWS, MLA_KV_LORA)
    r = jnp.arange(MLA_MISC_W)
    cols = jnp.arange(w_all)
    place = ((r[:, None] < 2 * MLA_ROPE)
             & ((cols[None, :] % MLA_HEAD_PAD) == (MLA_NOPE + r[:, None] % MLA_ROPE))).astype(BF16)
    return wq.astype(BF16), wkn.astype(BF16), place, wvt.astype(BF16)


def _rope_tables(s):
    inv = ROPE_THETA ** (-jnp.arange(0, MLA_ROPE, 2, dtype=F32) / MLA_ROPE)
    ang = jnp.arange(s, dtype=F32)[:, None] * inv[None, :]
    cos, sin = jnp.cos(ang), jnp.sin(ang)
    pad = jnp.zeros((s, MLA_HEAD_PAD - MLA_NOPE - MLA_ROPE), F32)
    cos_h = jnp.concatenate([jnp.ones((s, MLA_NOPE), F32), cos, cos, pad], axis=-1)
    sin_h = jnp.concatenate([jnp.zeros((s, MLA_NOPE), F32), sin, sin, pad], axis=-1)
    rope_r = jnp.concatenate([cos, cos, sin, sin, jnp.zeros((s, MLA_MISC_W - 2 * MLA_ROPE), F32)], axis=-1)
    return cos_h, sin_h, rope_r


def _mla_attn_kernel(q_ref, k_ref, vt_ref, o_ref, s_ref):
    n_k = k_ref.shape[1] // ATT_TK
    for qs in range(ATT_TQ // ATT_TQ_SUB):
        rows = slice(qs * ATT_TQ_SUB, (qs + 1) * ATT_TQ_SUB)
        qts = (q_ref[0, rows, 0:MLA_HEAD_PAD], q_ref[0, rows, MLA_HEAD_PAD:2 * MLA_HEAD_PAD])

        def produce(kk, buf):
            r = pl.multiple_of(kk * ATT_TK, ATT_TK)
            tile_max = []
            for hh in range(2):
                kt = k_ref[0, pl.ds(r, ATT_TK), hh * MLA_HEAD_PAD:(hh + 1) * MLA_HEAD_PAD]
                st = _dot_nt(kt, qts[hh])
                s_ref[buf, hh] = st
                tile_max.append(jnp.max(st, axis=0, keepdims=True))
            return tuple(tile_max)

        def consume(kk, buf, tile_max, carry):
            out = []
            for hh in range(2):
                m, acc = carry[2 * hh], carry[2 * hh + 1]
                m_new = jnp.maximum(m, tile_max[hh])
                alpha = jnp.exp2(m - m_new)
                p = jnp.exp2(s_ref[buf, hh] - m_new).astype(BF16)
                acc = acc * alpha + _dot(vt_ref[0, hh, kk], p)
                out += [m_new, acc]
            return tuple(out)

        def run(k0, state, produce_next):
            carry, tmax = state[:4], state[4:]
            for t in range(ATT_UNROLL):
                last = t == ATT_UNROLL - 1
                nxt = produce(k0 + t + 1, (t + 1) % 2) if (produce_next or not last) else ()
                carry = consume(k0 + t, t % 2, tmax, carry)
                tmax = nxt
            return carry + tmax

        m0 = jnp.full((1, ATT_TQ_SUB), -jnp.inf, F32)
        a0 = jnp.zeros((MLA_VT_ROWS, ATT_TQ_SUB), F32)
        state = lax.fori_loop(0, n_k // ATT_UNROLL - 1, lambda j, st: run(ATT_UNROLL * j, st, True),
                              (m0, a0, m0, a0) + produce(0, 0))
        carry = run(n_k - ATT_UNROLL, state, False)
        outs = [carry[2 * hh + 1][:MLA_DV] / carry[2 * hh + 1][MLA_DV:MLA_DV + 1] for hh in range(2)]
        o_ref[0, rows, :] = jnp.concatenate(outs, axis=0).T.astype(o_ref.dtype)


def _mla_attn(q, k, vt):
    bsz, s, _ = q.shape
    n_kt = s // ATT_TK
    assert s % (ATT_TK * ATT_UNROLL) == 0 and s % ATT_TQ == 0 and ATT_UNROLL % 2 == 0
    return pl.pallas_call(
        _mla_attn_kernel,
        grid=(bsz, MLA_HEADS // 2, s // ATT_TQ),
        in_specs=[
            pl.BlockSpec((1, ATT_TQ, 2 * MLA_HEAD_PAD), lambda b, h, i: (b, i, h)),
            pl.BlockSpec((1, s, 2 * MLA_HEAD_PAD), lambda b, h, i: (b, 0, h)),
            pl.BlockSpec((1, 2, n_kt, MLA_VT_ROWS, ATT_TK), lambda b, h, i: (b, h, 0, 0, 0))],
        out_specs=pl.BlockSpec((1, ATT_TQ, 2 * MLA_DV), lambda b, h, i: (b, i, h)),
        out_shape=jax.ShapeDtypeStruct((bsz, s, MLA_W), BF16),
        scratch_shapes=[pltpu.VMEM((2, 2, ATT_TK, ATT_TQ_SUB), F32)],
        compiler_params=_cparams(("parallel", "parallel", "arbitrary")),
        name="mla_attn",
    )(q, k, vt)


def _lane_scan(x, op, identity, reverse):
    lane = lax.broadcasted_iota(jnp.int32, x.shape, 1)
    step = 1
    while step < ML_CHUNK:
        if reverse:
            shifted = jnp.where(lane < ML_CHUNK - step, pltpu.roll(x, ML_CHUNK - step, 1), identity)
        else:
            shifted = jnp.where(lane >= step, pltpu.roll(x, step, 1), identity)
        x = op(x, shifted)
        step *= 2
    return x


def _mlstm_prologue(grf, grb, bias_r, eye, tri_pre, tri_suf, b_ref, cmax_ref, ctot_ref, gtot_ref, ccol_ref):
    n_chunks = ML_SEQ_TILE // ML_CHUNK
    n_dir = 2 * ML_HEADS
    rows16 = lax.broadcasted_iota(jnp.int32, (1, 2 * n_dir, 1), 1)
    g = jnp.where((rows16 % n_dir) < ML_HEADS, grf[0], grb[0]) + bias_r[...][None]
    li = g[:, :n_dir].reshape(n_chunks * n_dir, ML_CHUNK)
    lf = _log_sigmoid(g[:, n_dir:]).reshape(n_chunks * n_dir, ML_CHUNK)
    is_fwd = (lax.broadcasted_iota(jnp.int32, li.shape, 0) % n_dir) < ML_HEADS
    lane = lax.broadcasted_iota(jnp.int32, li.shape, 1)
    b = jnp.where(is_fwd, _tri_right(lf, tri_pre), _tri_right(lf, tri_suf))
    c = li - b
    c_next = jnp.where(lane < ML_CHUNK - 1, pltpu.roll(c, ML_CHUNK - 1, 1), -jnp.inf)
    cmax = jnp.where(is_fwd, _lane_scan(c, jnp.maximum, -jnp.inf, False),
                     _lane_scan(c_next, jnp.maximum, -jnp.inf, True))
    shape3 = (n_chunks, n_dir, ML_CHUNK)
    b_ref[...] = b.reshape(shape3)
    cmax_ref[...] = cmax.reshape(shape3)
    ctot_ref[...] = jnp.broadcast_to(jnp.max(c, axis=-1, keepdims=True), c.shape).reshape(shape3)
    gtot_ref[...] = jnp.broadcast_to(jnp.sum(lf, axis=-1, keepdims=True), c.shape).reshape(shape3)
    hi, mid, lo = _split3(c)
    for jc in range(n_chunks):
        rs = slice(jc * n_dir, (jc + 1) * n_dir)
        ccol_ref[jc] = _dot_nt(eye, hi[rs]) + _dot_nt(eye, mid[rs]) + _dot_nt(eye, lo[rs])


def _mlstm_chunk(q_ref, k_ref, v_ref, o_ref, cnt_ref, m_ref, n, row, jc, stats, inclusive):
    b_ref, cmax_ref, ctot_ref, gtot_ref, ccol_ref = stats
    h = n % ML_HEADS
    hs = slice(h * ML_DH, (h + 1) * ML_DH)
    jj = lax.broadcasted_iota(jnp.int32, (ML_CHUNK, ML_CHUNK), 0)
    ii = lax.broadcasted_iota(jnp.int32, (ML_CHUNK, ML_CHUNK), 1)
    mask = (jj <= ii) if inclusive else (jj > ii)

    m = m_ref[n]
    b_r = b_ref[jc, n:n + 1, :]
    big_m = jnp.maximum(m, cmax_ref[jc, n:n + 1, :])
    m_tot = jnp.maximum(m, ctot_ref[jc, n:n + 1, 0:1])
    g_tot = gtot_ref[jc, n:n + 1, 0:1]
    c_b = jnp.broadcast_to(ccol_ref[jc, :, n:n + 1], (ML_CHUNK, ML_CHUNK))

    q = q_ref[0, pl.ds(row, ML_CHUNK), hs]
    ks = k_ref[0, pl.ds(row, ML_CHUNK), hs] * (ML_DH ** -0.5)
    v = v_ref[0, pl.ds(row, ML_CHUNK), hs]
    ones_row = (lax.broadcasted_iota(jnp.int32, (ML_DH, ML_CHUNK), 0) == 0).astype(BF16)
    vt_ext = jnp.concatenate([v.T, ones_row], axis=0)

    cnt = cnt_ref[n]
    lhs = jnp.concatenate([ks.astype(BF16), cnt.astype(BF16)], axis=0)
    sr = _dot_nt(lhs, q)
    pt = jnp.exp(jnp.where(mask, c_b - big_m, -jnp.inf))
    st = (sr[:ML_CHUNK] * pt).astype(BF16)
    kw = (ks * jnp.exp(c_b - m_tot)).astype(BF16)
    ho = _dot(vt_ext, jnp.concatenate([st, kw], axis=-1))
    ht = ho[:, :ML_CHUNK] + jnp.exp(m - big_m) * sr[ML_CHUNK:]
    den = ht[ML_DH:ML_DH + 1]
    scale = 1.0 / jnp.maximum(jnp.abs(den), jnp.exp(-(b_r + big_m)))
    o_ref[0, pl.ds(row, ML_CHUNK), hs] = (ht[:ML_DH] * scale).T.astype(o_ref.dtype)
    cnt_ref[n] = jnp.exp(m - m_tot) * cnt + ho[:, ML_CHUNK:]
    m_ref[n] = g_tot + m_tot


def _mlstm_kernel(qf, kf, vf, grf, qb, kb, vb, grb, bias_r, eye_ref, tri_pre, tri_suf, of, ob,
                  cnt_ref, m_ref, b_ref, cmax_ref, ctot_ref, gtot_ref, ccol_ref):
    c = pl.program_id(1)

    @pl.when(c == 0)
    def _():
        cnt_ref[...] = jnp.zeros_like(cnt_ref)
        m_ref[...] = jnp.zeros_like(m_ref)

    stats = (b_ref, cmax_ref, ctot_ref, gtot_ref, ccol_ref)
    _mlstm_prologue(grf, grb, bias_r, eye_ref[...], tri_pre[...], tri_suf[...], *stats)
    n_chunks = ML_SEQ_TILE // ML_CHUNK

    def body(j, carry):
        jb = n_chunks - 1 - j
        rf = pl.multiple_of(j * ML_CHUNK, ML_CHUNK)
        rb = pl.multiple_of(jb * ML_CHUNK, ML_CHUNK)
        for h in range(ML_HEADS):
            _mlstm_chunk(qf, kf, vf, of, cnt_ref, m_ref, h, rf, j, stats, True)
            _mlstm_chunk(qb, kb, vb, ob, cnt_ref, m_ref, ML_HEADS + h, rb, jb, stats, False)
        return carry

    lax.fori_loop(0, n_chunks, body, 0)


def _mlstm(q, k, v, gates, if_bias):
    bsz, s, _ = q.shape
    ts = ML_SEQ_TILE
    ns = s // ts
    n_chunks = ts // ML_CHUNK
    n_gate = 4 * ML_HEADS
    n_dir = 2 * ML_HEADS
    gates_r = gates.reshape(bsz, s // ML_CHUNK, ML_CHUNK, n_gate).transpose(0, 1, 3, 2)
    fwd = lambda b, c: (b, c, 0)
    bwd = lambda b, c: (b, ns - 1 - c, 0)
    fwd4 = lambda b, c: (b, c, 0, 0)
    bwd4 = lambda b, c: (b, ns - 1 - c, 0, 0)

    def seq_specs(imap, imap4):
        return [pl.BlockSpec((1, ts, ML_W), imap), pl.BlockSpec((1, ts, ML_W), imap),
                pl.BlockSpec((1, ts, ML_W), imap),
                pl.BlockSpec((1, n_chunks, n_gate, ML_CHUNK), imap4)]

    in_specs = seq_specs(fwd, fwd4) + seq_specs(bwd, bwd4) + [
        _const_spec((n_gate, ML_CHUNK))] + [_const_spec((ML_CHUNK, ML_CHUNK))] * 3
    idx = jnp.arange(ML_CHUNK)
    tri_pre = (idx[:, None] <= idx[None, :]).astype(BF16)
    tri_suf = (idx[:, None] >= idx[None, :]).astype(BF16)
    out_specs = [pl.BlockSpec((1, ts, ML_W), fwd), pl.BlockSpec((1, ts, ML_W), bwd)]
    out_shape = [jax.ShapeDtypeStruct((bsz, s, ML_W), BF16)] * 2
    stat = pltpu.VMEM((n_chunks, n_dir, ML_CHUNK), F32)
    return pl.pallas_call(
        _mlstm_kernel,
        grid=(bsz, ns),
        in_specs=in_specs,
        out_specs=out_specs,
        out_shape=out_shape,
        scratch_shapes=[pltpu.VMEM((n_dir, 2 * ML_DH, ML_DH), F32), pltpu.VMEM((n_dir, 1, 1), F32),
                        stat, stat, stat, stat, pltpu.VMEM((n_chunks, ML_CHUNK, n_dir), F32)],
        compiler_params=_cparams(("parallel", "arbitrary")),
        name="mlstm_scan",
    )(q, k, v, gates_r, q, k, v, gates_r,
      jnp.broadcast_to(if_bias[:, None], (n_gate, ML_CHUNK)), jnp.eye(ML_CHUNK, dtype=BF16),
      tri_pre, tri_suf)


def _odd_out_kernel(final, att, mgate, hf, hb, mo, lgate, ng, wa, wb, x, fg, o_ref):
    mla_out = (att[0].astype(F32) * _silu(mgate[0].astype(F32))).astype(BF16)
    hm = hf[0].astype(F32) + hb[0].astype(F32)
    mo_v, lg_v = mo[0].astype(F32), lgate[0].astype(F32)
    parts = []
    for h in range(ML_HEADS):
        hs = slice(h * ML_DH, (h + 1) * ML_DH)
        y = _rms(hm[:, hs], ng[...]) * _sigmoid(mo_v[:, hs])
        parts.append((y * _silu(lg_v[:, hs])).astype(BF16))
    ml_out = jnp.concatenate(parts, axis=-1)
    xn = x[0] + (_dot(mla_out, wa[...]) + _dot(ml_out, wb[...]))
    o_ref[0] = _rms(xn, fg[...]) if final else xn


def _odd_out(att, mla_gate, h_f, h_b, mo, ml_gate, norm_g, w_out, x, final_g, final):
    bsz, s, _ = x.shape
    tm = TAIL_TILE
    cur = lambda b, i: (b, i, 0)
    half = pl.BlockSpec((1, tm, MLA_W), cur)
    in_specs = [half] * 6 + [
        _const_spec((1, ML_DH)), _const_spec((MLA_W, D_MODEL)), _const_spec((ML_W, D_MODEL)),
        pl.BlockSpec((1, tm, D_MODEL), cur), _const_spec((1, D_MODEL))]
    return pl.pallas_call(
        functools.partial(_odd_out_kernel, final),
        grid=(bsz, s // tm),
        in_specs=in_specs,
        out_specs=pl.BlockSpec((1, tm, D_MODEL), cur),
        out_shape=jax.ShapeDtypeStruct((bsz, s, D_MODEL), F32),
        compiler_params=_cparams(("parallel", "parallel")),
        name="odd_out",
    )(att, mla_gate, h_f, h_b, mo, ml_gate, norm_g[None, :],
      w_out[:MLA_W].astype(BF16), w_out[MLA_W:].astype(BF16), x, final_g[None, :])


def _col_split(w, sizes):
    out, off = [], 0
    for n in sizes:
        out.append(w[:, off:off + n].astype(BF16))
        off += n
    return out


def _even_layer(x, g, w_in, a_up, a_bias, gla_norm_g, pool_w, pool_scale, w_out):
    bsz, s, _ = x.shape
    sizes = (GLA_K_TOT, GLA_K_TOT, GLA_V_TOT, GLA_V_TOT, 2 * GLA_LR, POOL_W, POOL_W)
    dts = (F32, F32, BF16, BF16, BF16, F32, BF16)
    outs = _norm_proj(x.reshape(bsz * s, D_MODEL), g[None, :], _col_split(w_in, sizes), dts)
    q, k, v, gate, lr, pool_u, pool_gate = [o.reshape(bsz, s, -1) for o in outs]
    o_f, o_b = _gla(q, k, v, lr, a_up, a_bias)
    return _even_out(o_f, o_b, gate, gla_norm_g, pool_u, pool_gate, pool_w, pool_scale, w_out, x)


def _odd_layer(x, g, tabs, w_in, q_norm_g, q_up, kv_norm_g, kv_up, if_bias, ml_norm_g, w_out,
               final_g, final):
    bsz, s, _ = x.shape
    half = MLA_ROPE // 2
    off = MLA_Q_LORA + MLA_KV_LORA
    kr_w = w_in[:, off:off + MLA_ROPE]
    n_gate = 4 * ML_HEADS
    gate_off = off + MLA_ROPE + MLA_W + 4 * ML_W
    w_lat = jnp.concatenate(
        [w_in[:, :off + MLA_ROPE], -kr_w[:, half:], kr_w[:, :half], w_in[:, gate_off:gate_off + n_gate],
         jnp.zeros((D_MODEL, MLA_MISC_W - 2 * MLA_ROPE - n_gate), F32)], axis=-1).astype(BF16)
    wide = _col_split(w_in[:, off + MLA_ROPE:gate_off], (MLA_W, ML_W, ML_W, ML_W, ML_W))
    wide.append(w_in[:, gate_off + n_gate:].astype(BF16))
    dts = (F32, BF16, BF16, F32, BF16, BF16, BF16)
    outs = _norm_proj(x.reshape(bsz * s, D_MODEL), g[None, :], [w_lat] + wide, dts)
    lat, mla_gate, mq, mk, mv, mo, ml_gate = [o.reshape(bsz, s, -1) for o in outs]
    mif = lat[:, :, off + 2 * MLA_ROPE:off + 2 * MLA_ROPE + n_gate]
    wq, wkn, place, wvt = _mla_weights(q_up, kv_up)
    qa, ka, vt = _mla_qkv(lat, tabs, q_norm_g, kv_norm_g, wq, wkn, place, wvt)
    att = _mla_attn(qa, ka, vt)
    h_f, h_b = _mlstm(mq, mk, mv, mif, if_bias)
    return _odd_out(att, mla_gate, h_f, h_b, mo, ml_gate, ml_norm_g, w_out, x, final_g, final)


def _trunk(x, norm_g, final_norm_g, e_w_in, e_gla_a_up, e_gla_a_bias, e_gla_norm_g, e_pool_w,
           e_pool_scale, e_w_out, o_w_in, o_q_norm_g, o_q_up, o_kv_norm_g, o_kv_up, o_if_bias,
           o_mlstm_norm_g, o_w_out):
    depth = norm_g.shape[0]
    assert depth % 2 == 0, "the final RMSNorm is fused into the last (odd) layer's tail"
    tabs = _rope_tables(x.shape[1])
    for layer in range(depth):
        i = layer // 2
        if layer % 2 == 0:
            x = _even_layer(x, norm_g[layer], e_w_in[i], e_gla_a_up[i], e_gla_a_bias[i],
                            e_gla_norm_g[i], e_pool_w[i], e_pool_scale[i], e_w_out[i])
        else:
            x = _odd_layer(x, norm_g[layer], tabs, o_w_in[i], o_q_norm_g[i], o_q_up[i],
                           o_kv_norm_g[i], o_kv_up[i], o_if_bias[i], o_mlstm_norm_g[i], o_w_out[i],
                           final_norm_g, layer == depth - 1)
    return x


def kernel(x_prompt, x_sample, norm_g, final_norm_g, e_w_in, e_gla_a_up, e_gla_a_bias, e_gla_norm_g,
           e_pool_w, e_pool_scale, e_w_out, o_w_in, o_q_norm_g, o_q_up, o_kv_norm_g, o_kv_up,
           o_if_bias, o_mlstm_norm_g, o_w_out):
    params = (norm_g, final_norm_g, e_w_in, e_gla_a_up, e_gla_a_bias, e_gla_norm_g, e_pool_w,
              e_pool_scale, e_w_out, o_w_in, o_q_norm_g, o_q_up, o_kv_norm_g, o_kv_up, o_if_bias,
              o_mlstm_norm_g, o_w_out)
    return (_trunk(x_prompt, *params), _trunk(x_sample, *params))
```

```python
import functools
import math

import jax
import jax.numpy as jnp
from jax import lax
from jax.experimental import pallas as pl
from jax.experimental.pallas import tpu as pltpu

F32 = jnp.float32
BF16 = jnp.bfloat16

D_MODEL = 1024
NORM_EPS = 1e-6
CHUNK = 64

GLA_HEADS = 4
GLA_DK = 128
GLA_DV = 256
GLA_LR = 16
GLA_GATE_NORM = 16.0
GLA_K_TOT = GLA_HEADS * GLA_DK
GLA_V_TOT = GLA_HEADS * GLA_DV
GLA_BLOCK = 2 * CHUNK

POOL_GROUPS = 4
POOL_WINDOWS = (2, 4, 8, 16)
POOL_DG = 128
POOL_W = POOL_GROUPS * POOL_DG
POOL_HALO = 8

MLA_HEADS = 8
MLA_NOPE = 64
MLA_ROPE = 32
MLA_DV = 64
MLA_Q_LORA = 384
MLA_KV_LORA = 256
MLA_W = MLA_HEADS * MLA_DV
MLA_HEAD_PAD = 128
MLA_VT_ROWS = 80
MLA_MISC_W = 128
MLA_LAT_W = MLA_Q_LORA + MLA_KV_LORA + MLA_MISC_W
ROPE_THETA = 10000.0

ML_HEADS = 4
ML_DH = 128
ML_W = ML_HEADS * ML_DH
ML_CHUNK = 128

VMEM_LIMIT_BYTES = 56 * 1024 * 1024

TOKEN_TILE = 512
PROJ_TILE = 1024
TAIL_TILE = 1024
SEQ_TILE = 1024
ML_SEQ_TILE = 1024
ATT_TQ = 1024
ATT_TQ_SUB = 512
ATT_TK = 256
ATT_UNROLL = 8


def _cparams(sem):
    return pltpu.CompilerParams(dimension_semantics=sem, vmem_limit_bytes=VMEM_LIMIT_BYTES)


def _const_spec(shape):
    nd = len(shape)
    return pl.BlockSpec(shape, lambda *_: (0,) * nd)


def _dot(a, b):
    return jnp.dot(a, b, preferred_element_type=F32)


def _dot_nt(a, b):
    return lax.dot_general(a, b, (((1,), (1,)), ((), ())), preferred_element_type=F32)


def _dot_tn(a, b):
    return lax.dot_general(a, b, (((0,), (0,)), ((), ())), preferred_element_type=F32)


def _sigmoid(x):
    return 1.0 / (1.0 + jnp.exp(-x))


def _silu(x):
    return x * _sigmoid(x)


def _log_sigmoid(x):
    return jnp.minimum(x, 0.0) - jnp.log(1.0 + jnp.exp(-jnp.abs(x)))


def _rms(x, g):
    return x * lax.rsqrt(jnp.mean(x * x, axis=-1, keepdims=True) + NORM_EPS) * g


def _split3(x):
    hi = x.astype(BF16)
    r1 = x - hi.astype(F32)
    mid = r1.astype(BF16)
    lo = (r1 - mid.astype(F32)).astype(BF16)
    return hi, mid, lo


def _tri_left(tri, x):
    hi, mid, lo = _split3(x)
    return _dot(tri, hi) + _dot(tri, mid) + _dot(tri, lo)


def _tri_left2(tri, x):
    hi = x.astype(BF16)
    mid = (x - hi.astype(F32)).astype(BF16)
    return _dot(tri, hi) + _dot(tri, mid)


def _tri_right(x, tri):
    hi, mid, lo = _split3(x)
    return _dot(hi, tri) + _dot(mid, tri) + _dot(lo, tri)


def _norm_proj_kernel(n_out, x_ref, g_ref, *refs):
    w_refs, o_refs = refs[:n_out], refs[n_out:]
    h = _rms(x_ref[...], g_ref[...]).astype(BF16)
    for w_ref, o_ref in zip(w_refs, o_refs):
        o_ref[...] = _dot(h, w_ref[...]).astype(o_ref.dtype)


def _norm_proj(x2d, g, weights, out_dtypes):
    t = x2d.shape[0]
    tm = PROJ_TILE
    n_out = len(weights)
    in_specs = [pl.BlockSpec((tm, D_MODEL), lambda i: (i, 0)), _const_spec((1, D_MODEL))]
    in_specs += [_const_spec(w.shape) for w in weights]
    out_specs = [pl.BlockSpec((tm, w.shape[1]), lambda i: (i, 0)) for w in weights]
    out_shape = [jax.ShapeDtypeStruct((t, w.shape[1]), dt) for w, dt in zip(weights, out_dtypes)]
    return pl.pallas_call(
        functools.partial(_norm_proj_kernel, n_out),
        grid=(t // tm,),
        in_specs=in_specs,
        out_specs=out_specs,
        out_shape=out_shape,
        compiler_params=_cparams(("parallel",)),
        name="norm_proj",
    )(x2d, g, *weights)


def _gla_block(q_ref, k_ref, v_ref, o_ref, st_ref, st_idx, row, b_all, inclusive):
    ii = lax.broadcasted_iota(jnp.int32, (GLA_BLOCK, GLA_BLOCK), 0)
    jj = lax.broadcasted_iota(jnp.int32, (GLA_BLOCK, GLA_BLOCK), 1)
    rr = lax.broadcasted_iota(jnp.int32, (GLA_BLOCK, 1), 0)
    same = (ii >= CHUNK) == (jj >= CHUNK)
    if inclusive:
        intra, cross, far_rows = same & (jj <= ii), (ii >= CHUNK) & (jj < CHUNK), rr >= CHUNK
        edge_lo, edge_hi = CHUNK - 1, GLA_BLOCK - 1
    else:
        intra, cross, far_rows = same & (jj > ii), (ii < CHUNK) & (jj >= CHUNK), rr < CHUNK
        edge_lo, edge_hi = 0, CHUNK
    for h in range(GLA_HEADS):
        ks = slice(h * GLA_DK, (h + 1) * GLA_DK)
        vs = slice(h * GLA_DV, (h + 1) * GLA_DV)
        q = q_ref[0, pl.ds(row, GLA_BLOCK), ks] * (GLA_DK ** -0.5)
        k = k_ref[0, pl.ds(row, GLA_BLOCK), ks]
        v = v_ref[0, pl.ds(row, GLA_BLOCK), vs]
        b = b_all[:, ks]
        tot_lo, tot_hi = b[edge_lo:edge_lo + 1, :], b[edge_hi:edge_hi + 1, :]
        tot_near, tot_far = (tot_lo, tot_hi) if inclusive else (tot_hi, tot_lo)
        qe = q * jnp.exp(b)
        kd = (k * jnp.exp(-b)).astype(BF16)
        kdec = k * jnp.exp(jnp.where(rr < CHUNK, tot_lo, tot_hi) - b)
        pp = _dot_nt(qe.astype(BF16), jnp.concatenate([kd, kdec.astype(BF16)], axis=0))
        a = jnp.where(intra, pp[:, :GLA_BLOCK], jnp.where(cross, pp[:, GLA_BLOCK:], 0.0)).astype(BF16)
        qe_in = jnp.where(far_rows, qe * jnp.exp(tot_near), qe).astype(BF16)
        kdec_out = jnp.where(far_rows, kdec, kdec * jnp.exp(tot_far)).astype(BF16)
        st = st_ref[st_idx + h]
        o_ref[0, pl.ds(row, GLA_BLOCK), vs] = (_dot(a, v) + _dot_nt(qe_in, st.astype(BF16))).astype(o_ref.dtype)
        st_ref[st_idx + h] = st * jnp.exp(tot_near + tot_far) + _dot_tn(v, kdec_out)


def _gla_kernel(qf, kf, vf, lrf, qb, kb, vb, lrb, aupf, aupb, biasf, biasb, tril, triu,
                of, ob, st_ref, la_ref):
    c = pl.program_id(1)

    @pl.when(c == 0)
    def _():
        st_ref[...] = jnp.zeros_like(st_ref)

    inv = 1.0 / GLA_GATE_NORM
    la_ref[0] = _log_sigmoid(_dot(lrf[0], aupf[...]) + biasf[...]) * inv
    la_ref[1] = _log_sigmoid(_dot(lrb[0], aupb[...]) + biasb[...]) * inv
    n_blocks = SEQ_TILE // GLA_BLOCK

    def body(j, carry):
        rf = pl.multiple_of(j * GLA_BLOCK, GLA_BLOCK)
        rb = pl.multiple_of((n_blocks - 1 - j) * GLA_BLOCK, GLA_BLOCK)
        b_f = _tri_left2(tril[...], la_ref[0, pl.ds(rf, GLA_BLOCK), :])
        b_b = _tri_left2(triu[...], la_ref[1, pl.ds(rb, GLA_BLOCK), :])
        _gla_block(qf, kf, vf, of, st_ref, 0, rf, b_f, True)
        _gla_block(qb, kb, vb, ob, st_ref, GLA_HEADS, rb, b_b, False)
        return carry

    lax.fori_loop(0, n_blocks, body, 0)


def _gla(q, k, v, lr, a_up, a_bias):
    bsz, s, _ = q.shape
    ts = SEQ_TILE
    ns = s // ts
    fwd = lambda b, c: (b, c, 0)
    bwd = lambda b, c: (b, ns - 1 - c, 0)
    idx = jnp.arange(GLA_BLOCK)
    same_chunk = (idx[None, :] // CHUNK) == (idx[:, None] // CHUNK)
    tril = (same_chunk & (idx[None, :] <= idx[:, None])).astype(BF16)
    triu = (same_chunk & (idx[None, :] >= idx[:, None])).astype(BF16)

    def seq_specs(imap):
        return [pl.BlockSpec((1, ts, GLA_K_TOT), imap), pl.BlockSpec((1, ts, GLA_K_TOT), imap),
                pl.BlockSpec((1, ts, GLA_V_TOT), imap), pl.BlockSpec((1, ts, 2 * GLA_LR), imap)]

    in_specs = seq_specs(fwd) + seq_specs(bwd) + [
        _const_spec((2 * GLA_LR, GLA_K_TOT)), _const_spec((2 * GLA_LR, GLA_K_TOT)),
        _const_spec((1, GLA_K_TOT)), _const_spec((1, GLA_K_TOT)),
        _const_spec((GLA_BLOCK, GLA_BLOCK)), _const_spec((GLA_BLOCK, GLA_BLOCK))]
    out_specs = [pl.BlockSpec((1, ts, GLA_V_TOT), fwd), pl.BlockSpec((1, ts, GLA_V_TOT), bwd)]
    out_shape = [jax.ShapeDtypeStruct((bsz, s, GLA_V_TOT), BF16)] * 2
    zeros = jnp.zeros((GLA_LR, GLA_K_TOT), F32)
    aup_f = jnp.concatenate([a_up[0], zeros], axis=0).astype(BF16)
    aup_b = jnp.concatenate([zeros, a_up[1]], axis=0).astype(BF16)
    return pl.pallas_call(
        _gla_kernel,
        grid=(bsz, ns),
        in_specs=in_specs,
        out_specs=out_specs,
        out_shape=out_shape,
        scratch_shapes=[pltpu.VMEM((2 * GLA_HEADS, GLA_DV, GLA_DK), F32),
                        pltpu.VMEM((2, ts, GLA_K_TOT), F32)],
        compiler_params=_cparams(("parallel", "arbitrary")),
        name="gla_scan",
    )(q, k, v, lr, q, k, v, lr, aup_f, aup_b, a_bias[0:1], a_bias[1:2], tril, triu)


def _even_out_kernel(seq_len, of, ob, gate, ng, pu, pprev, pnext, pgate, pw, pscale, wa, wb, x, o_ref):
    tm = of.shape[1]
    i = pl.program_id(1)
    n_i = pl.num_programs(1)

    o = of[0].astype(F32) + ob[0].astype(F32)
    g_all = gate[0].astype(F32)
    parts = []
    for h in range(GLA_HEADS):
        vs = slice(h * GLA_DV, (h + 1) * GLA_DV)
        parts.append((_rms(o[:, vs], ng[...]) * _silu(g_all[:, vs])).astype(BF16))
    gla_out = jnp.concatenate(parts, axis=-1)

    u = pu[0]
    prev = jnp.where(i > 0, pprev[0], 0.0)
    nxt = jnp.where(i < n_i - 1, pnext[0], 0.0)
    ext = jnp.concatenate([prev, u, nxt], axis=0)
    n_ext = tm + 2 * POOL_HALO
    pos = i * tm + lax.broadcasted_iota(jnp.int32, (tm, 1), 0)
    mixed = []
    for gi, w in enumerate(POOL_WINDOWS):
        cs = slice(gi * POOL_DG, (gi + 1) * POOL_DG)
        a = ext[:, cs]
        span = 1
        while span < w:
            a = a + pltpu.roll(a, span, 0)
            span *= 2
        shift = w // 2 - 1
        if shift:
            a = pltpu.roll(a, n_ext - shift, 0)
        win = a[POOL_HALO:POOL_HALO + tm]
        lo = jnp.maximum(pos - w // 2, 0)
        hi = jnp.minimum(pos + w // 2, seq_len)
        pooled = win / (hi - lo).astype(F32) - u[:, cs]
        mixed.append(_dot(pooled.astype(BF16), pw[gi]))
    pool_out = (jnp.concatenate(mixed, axis=-1) * pscale[...] * _silu(pgate[0].astype(F32))).astype(BF16)

    y = _dot(gla_out, wa[...]) + _dot(pool_out, wb[...])
    o_ref[0] = x[0] + y


def _even_out(o_f, o_b, gate, norm_g, pool_u, pool_gate, pool_w, pool_scale, w_out, x):
    bsz, s, _ = x.shape
    tm = TOKEN_TILE
    nh = tm // POOL_HALO
    n_halo = s // POOL_HALO
    cur = lambda b, i: (b, i, 0)
    in_specs = [
        pl.BlockSpec((1, tm, GLA_V_TOT), cur), pl.BlockSpec((1, tm, GLA_V_TOT), cur),
        pl.BlockSpec((1, tm, GLA_V_TOT), cur), _const_spec((1, GLA_DV)),
        pl.BlockSpec((1, tm, POOL_W), cur),
        pl.BlockSpec((1, POOL_HALO, POOL_W), lambda b, i: (b, jnp.maximum(i * nh - 1, 0), 0)),
        pl.BlockSpec((1, POOL_HALO, POOL_W), lambda b, i: (b, jnp.minimum((i + 1) * nh, n_halo - 1), 0)),
        pl.BlockSpec((1, tm, POOL_W), cur),
        _const_spec((POOL_GROUPS, POOL_DG, POOL_DG)), _const_spec((1, POOL_W)),
        _const_spec((GLA_V_TOT, D_MODEL)), _const_spec((POOL_W, D_MODEL)),
        pl.BlockSpec((1, tm, D_MODEL), cur)]
    return pl.pallas_call(
        functools.partial(_even_out_kernel, s),
        grid=(bsz, s // tm),
        in_specs=in_specs,
        out_specs=pl.BlockSpec((1, tm, D_MODEL), cur),
        out_shape=jax.ShapeDtypeStruct((bsz, s, D_MODEL), F32),
        compiler_params=_cparams(("parallel", "parallel")),
        name="even_out",
    )(o_f, o_b, gate, norm_g[None, :], pool_u, pool_u, pool_u, pool_gate,
      pool_w.astype(BF16), pool_scale[None, :],
      w_out[:GLA_V_TOT].astype(BF16), w_out[GLA_V_TOT:].astype(BF16), x)


def _mla_qkv_kernel(lat, cos_h, sin_h, rope_r, qg, kvg, wq, wkn, place, wvt, q_ref, k_ref, vt_ref):
    qk_scale = math.log2(math.e) * (MLA_NOPE + MLA_ROPE) ** -0.5
    w_all = MLA_HEADS * MLA_HEAD_PAD
    x = lat[0]
    cq = x[:, :MLA_Q_LORA]
    ckv = x[:, MLA_Q_LORA:MLA_Q_LORA + MLA_KV_LORA]
    nq = _rms(cq, qg[...]).astype(BF16)
    qq = _dot(nq, wq[...])
    cos2 = jnp.concatenate([cos_h[...], cos_h[...]], axis=-1)
    sin2 = jnp.concatenate([sin_h[...], sin_h[...]], axis=-1)
    for p in range(MLA_HEADS // 2):
        sl = slice(2 * p * MLA_HEAD_PAD, (2 * p + 2) * MLA_HEAD_PAD)
        sr = slice(w_all + 2 * p * MLA_HEAD_PAD, w_all + (2 * p + 2) * MLA_HEAD_PAD)
        q_ref[0, :, sl] = ((qq[:, sl] * cos2 + qq[:, sr] * sin2) * qk_scale).astype(BF16)

    nkv = _rms(ckv, kvg[...]).astype(BF16)
    k_rope = (x[:, MLA_Q_LORA + MLA_KV_LORA:] * rope_r[...]).astype(BF16)
    k_ref[0] = (_dot(nkv, wkn[...]) + _dot(k_rope, place[...])).astype(BF16)

    vt = _dot_nt(wvt[...], nkv)
    ones_row = lax.broadcasted_iota(jnp.int32, (MLA_VT_ROWS, 1), 0) == MLA_DV
    for h in range(MLA_HEADS):
        vh = vt[h * MLA_VT_ROWS:(h + 1) * MLA_VT_ROWS]
        vh = jnp.where(ones_row, 1.0, vh).astype(BF16)
        for c in range(vt_ref.shape[2]):
            vt_ref[0, h, c] = vh[:, c * ATT_TK:(c + 1) * ATT_TK]


def _mla_qkv(lat, tabs, q_norm_g, kv_norm_g, wq, wkn, place, wvt):
    bsz, s, _ = lat.shape
    tm = PROJ_TILE
    n_sub = tm // ATT_TK
    w_all = MLA_HEADS * MLA_HEAD_PAD
    cur = lambda b, i: (b, i, 0)
    tab = lambda b, i: (i, 0)
    cos_h, sin_h, rope_r = tabs
    in_specs = [
        pl.BlockSpec((1, tm, MLA_LAT_W), cur),
        pl.BlockSpec((tm, MLA_HEAD_PAD), tab), pl.BlockSpec((tm, MLA_HEAD_PAD), tab),
        pl.BlockSpec((tm, MLA_MISC_W), tab),
        _const_spec((1, MLA_Q_LORA)), _const_spec((1, MLA_KV_LORA)),
        _const_spec(wq.shape), _const_spec(wkn.shape), _const_spec(place.shape), _const_spec(wvt.shape)]
    out_specs = [
        pl.BlockSpec((1, tm, w_all), cur), pl.BlockSpec((1, tm, w_all), cur),
        pl.BlockSpec((1, MLA_HEADS, n_sub, MLA_VT_ROWS, ATT_TK), lambda b, i: (b, 0, i, 0, 0))]
    out_shape = [
        jax.ShapeDtypeStruct((bsz, s, w_all), BF16), jax.ShapeDtypeStruct((bsz, s, w_all), BF16),
        jax.ShapeDtypeStruct((bsz, MLA_HEADS, s // ATT_TK, MLA_VT_ROWS, ATT_TK), BF16)]
    return pl.pallas_call(
        _mla_qkv_kernel,
        grid=(bsz, s // tm),
        in_specs=in_specs,
        out_specs=out_specs,
        out_shape=out_shape,
        compiler_params=_cparams(("parallel", "parallel")),
        name="mla_qkv",
    )(lat, cos_h, sin_h, rope_r, q_norm_g[None, :], kv_norm_g[None, :], wq, wkn, place, wvt)


def _mla_weights(q_up, kv_up):
    dq = MLA_NOPE + MLA_ROPE
    half = MLA_ROPE // 2
    qh = q_up.reshape(MLA_Q_LORA, MLA_HEADS, dq)
    zeros = jnp.zeros((MLA_Q_LORA, MLA_HEADS, MLA_HEAD_PAD - dq), F32)
    main = jnp.concatenate([qh, zeros], axis=-1)
    x1 = qh[..., MLA_NOPE:MLA_NOPE + half]
    x2 = qh[..., MLA_NOPE + half:]
    rot = jnp.concatenate([jnp.zeros((MLA_Q_LORA, MLA_HEADS, MLA_NOPE), F32), -x2, x1, zeros], axis=-1)
    w_all = MLA_HEADS * MLA_HEAD_PAD
    wq = jnp.concatenate([main.reshape(MLA_Q_LORA, w_all), rot.reshape(MLA_Q_LORA, w_all)], axis=-1)

    kvh = kv_up.reshape(MLA_KV_LORA, MLA_HEADS, MLA_NOPE + MLA_DV)
    wkn = jnp.concatenate(
        [kvh[..., :MLA_NOPE], jnp.zeros((MLA_KV_LORA, MLA_HEADS, MLA_HEAD_PAD - MLA_NOPE), F32)],
        axis=-1).reshape(MLA_KV_LORA, w_all)
    wv = jnp.transpose(kvh[..., MLA_NOPE:], (1, 2, 0))
    wvt = jnp.concatenate(
        [wv, jnp.zeros((MLA_HEADS, MLA_VT_ROWS - MLA_DV, MLA_KV_LORA), F32)],
        axis=1).reshape(MLA_HEADS * MLA_VT_ROWS, MLA_KV_LORA)
    r = jnp.arange(MLA_MISC_W)
    cols = jnp.arange(w_all)
    place = ((r[:, None] < 2 * MLA_ROPE)
             & ((cols[None, :] % MLA_HEAD_PAD) == (MLA_NOPE + r[:, None] % MLA_ROPE))).astype(BF16)
    return wq.astype(BF16), wkn.astype(BF16), place, wvt.astype(BF16)


def _rope_tables(s):
    inv = ROPE_THETA ** (-jnp.arange(0, MLA_ROPE, 2, dtype=F32) / MLA_ROPE)
    ang = jnp.arange(s, dtype=F32)[:, None] * inv[None, :]
    cos, sin = jnp.cos(ang), jnp.sin(ang)
    pad = jnp.zeros((s, MLA_HEAD_PAD - MLA_NOPE - MLA_ROPE), F32)
    cos_h = jnp.concatenate([jnp.ones((s, MLA_NOPE), F32), cos, cos, pad], axis=-1)
    sin_h = jnp.concatenate([jnp.zeros((s, MLA_NOPE), F32), sin, sin, pad], axis=-1)
    rope_r = jnp.concatenate([cos, cos, sin, sin, jnp.zeros((s, MLA_MISC_W - 2 * MLA_ROPE), F32)], axis=-1)
    return cos_h, sin_h, rope_r


def _mla_attn_kernel(q_ref, k_ref, vt_ref, o_ref, s_ref):
    n_k = k_ref.shape[1] // ATT_TK
    for qs in range(ATT_TQ // ATT_TQ_SUB):
        rows = slice(qs * ATT_TQ_SUB, (qs + 1) * ATT_TQ_SUB)
        qts = (q_ref[0, rows, 0:MLA_HEAD_PAD], q_ref[0, rows, MLA_HEAD_PAD:2 * MLA_HEAD_PAD])

        def produce(kk, buf):
            r = pl.multiple_of(kk * ATT_TK, ATT_TK)
            tile_max = []
            for hh in range(2):
                kt = k_ref[0, pl.ds(r, ATT_TK), hh * MLA_HEAD_PAD:(hh + 1) * MLA_HEAD_PAD]
                st = _dot_nt(kt, qts[hh])
                s_ref[buf, hh] = st
                tile_max.append(jnp.max(st, axis=0, keepdims=True))
            return tuple(tile_max)

        def consume(kk, buf, tile_max, carry):
            out = []
            for hh in range(2):
                m, acc = carry[2 * hh], carry[2 * hh + 1]
                m_new = jnp.maximum(m, tile_max[hh])
                alpha = jnp.exp2(m - m_new)
                p = jnp.exp2(s_ref[buf, hh] - m_new).astype(BF16)
                acc = acc * alpha + _dot(vt_ref[0, hh, kk], p)
                out += [m_new, acc]
            return tuple(out)

        def run(k0, state, produce_next):
            carry, tmax = state[:4], state[4:]
            for t in range(ATT_UNROLL):
                last = t == ATT_UNROLL - 1
                nxt = produce(k0 + t + 1, (t + 1) % 2) if (produce_next or not last) else ()
                carry = consume(k0 + t, t % 2, tmax, carry)
                tmax = nxt
            return carry + tmax

        m0 = jnp.full((1, ATT_TQ_SUB), -jnp.inf, F32)
        a0 = jnp.zeros((MLA_VT_ROWS, ATT_TQ_SUB), F32)
        state = lax.fori_loop(0, n_k // ATT_UNROLL - 1, lambda j, st: run(ATT_UNROLL * j, st, True),
                              (m0, a0, m0, a0) + produce(0, 0))
        carry = run(n_k - ATT_UNROLL, state, False)
        outs = [carry[2 * hh + 1][:MLA_DV] / carry[2 * hh + 1][MLA_DV:MLA_DV + 1] for hh in range(2)]
        o_ref[0, rows, :] = jnp.concatenate(outs, axis=0).T.astype(o_ref.dtype)


def _mla_attn(q, k, vt):
    bsz, s, _ = q.shape
    n_kt = s // ATT_TK
    assert s % (ATT_TK * ATT_UNROLL) == 0 and s % ATT_TQ == 0 and ATT_UNROLL % 2 == 0
    return pl.pallas_call(
        _mla_attn_kernel,
        grid=(bsz, MLA_HEADS // 2, s // ATT_TQ),
        in_specs=[
            pl.BlockSpec((1, ATT_TQ, 2 * MLA_HEAD_PAD), lambda b, h, i: (b, i, h)),
            pl.BlockSpec((1, s, 2 * MLA_HEAD_PAD), lambda b, h, i: (b, 0, h)),
            pl.BlockSpec((1, 2, n_kt, MLA_VT_ROWS, ATT_TK), lambda b, h, i: (b, h, 0, 0, 0))],
        out_specs=pl.BlockSpec((1, ATT_TQ, 2 * MLA_DV), lambda b, h, i: (b, i, h)),
        out_shape=jax.ShapeDtypeStruct((bsz, s, MLA_W), BF16),
        scratch_shapes=[pltpu.VMEM((2, 2, ATT_TK, ATT_TQ_SUB), F32)],
        compiler_params=_cparams(("parallel", "parallel", "arbitrary")),
        name="mla_attn",
    )(q, k, vt)


def _lane_scan(x, op, identity, reverse):
    lane = lax.broadcasted_iota(jnp.int32, x.shape, 1)
    step = 1
    while step < ML_CHUNK:
        if reverse:
            shifted = jnp.where(lane < ML_CHUNK - step, pltpu.roll(x, ML_CHUNK - step, 1), identity)
        else:
            shifted = jnp.where(lane >= step, pltpu.roll(x, step, 1), identity)
        x = op(x, shifted)
        step *= 2
    return x


def _mlstm_prologue(grf, grb, bias_r, eye, tri_pre, tri_suf, b_ref, cmax_ref, ctot_ref, gtot_ref, ccol_ref):
    n_chunks = ML_SEQ_TILE // ML_CHUNK
    n_dir = 2 * ML_HEADS
    rows16 = lax.broadcasted_iota(jnp.int32, (1, 2 * n_dir, 1), 1)
    g = jnp.where((rows16 % n_dir) < ML_HEADS, grf[0], grb[0]) + bias_r[...][None]
    li = g[:, :n_dir].reshape(n_chunks * n_dir, ML_CHUNK)
    lf = _log_sigmoid(g[:, n_dir:]).reshape(n_chunks * n_dir, ML_CHUNK)
    is_fwd = (lax.broadcasted_iota(jnp.int32, li.shape, 0) % n_dir) < ML_HEADS
    lane = lax.broadcasted_iota(jnp.int32, li.shape, 1)
    b = jnp.where(is_fwd, _tri_right(lf, tri_pre), _tri_right(lf, tri_suf))
    c = li - b
    c_next = jnp.where(lane < ML_CHUNK - 1, pltpu.roll(c, ML_CHUNK - 1, 1), -jnp.inf)
    cmax = jnp.where(is_fwd, _lane_scan(c, jnp.maximum, -jnp.inf, False),
                     _lane_scan(c_next, jnp.maximum, -jnp.inf, True))
    shape3 = (n_chunks, n_dir, ML_CHUNK)
    b_ref[...] = b.reshape(shape3)
    cmax_ref[...] = cmax.reshape(shape3)
    ctot_ref[...] = jnp.broadcast_to(jnp.max(c, axis=-1, keepdims=True), c.shape).reshape(shape3)
    gtot_ref[...] = jnp.broadcast_to(jnp.sum(lf, axis=-1, keepdims=True), c.shape).reshape(shape3)
    hi, mid, lo = _split3(c)
    for jc in range(n_chunks):
        rs = slice(jc * n_dir, (jc + 1) * n_dir)
        ccol_ref[jc] = _dot_nt(eye, hi[rs]) + _dot_nt(eye, mid[rs]) + _dot_nt(eye, lo[rs])


def _mlstm_chunk(q_ref, k_ref, v_ref, o_ref, cnt_ref, m_ref, n, row, jc, stats, inclusive):
    b_ref, cmax_ref, ctot_ref, gtot_ref, ccol_ref = stats
    h = n % ML_HEADS
    hs = slice(h * ML_DH, (h + 1) * ML_DH)
    jj = lax.broadcasted_iota(jnp.int32, (ML_CHUNK, ML_CHUNK), 0)
    ii = lax.broadcasted_iota(jnp.int32, (ML_CHUNK, ML_CHUNK), 1)
    mask = (jj <= ii) if inclusive else (jj > ii)

    m = m_ref[n]
    b_r = b_ref[jc, n:n + 1, :]
    big_m = jnp.maximum(m, cmax_ref[jc, n:n + 1, :])
    m_tot = jnp.maximum(m, ctot_ref[jc, n:n + 1, 0:1])
    g_tot = gtot_ref[jc, n:n + 1, 0:1]
    c_b = jnp.broadcast_to(ccol_ref[jc, :, n:n + 1], (ML_CHUNK, ML_CHUNK))

    q = q_ref[0, pl.ds(row, ML_CHUNK), hs]
    ks = k_ref[0, pl.ds(row, ML_CHUNK), hs] * (ML_DH ** -0.5)
    v = v_ref[0, pl.ds(row, ML_CHUNK), hs]
    ones_row = (lax.broadcasted_iota(jnp.int32, (ML_DH, ML_CHUNK), 0) == 0).astype(BF16)
    vt_ext = jnp.concatenate([v.T, ones_row], axis=0)

    cnt = cnt_ref[n]
    lhs = jnp.concatenate([ks.astype(BF16), cnt.astype(BF16)], axis=0)
    sr = _dot_nt(lhs, q)
    pt = jnp.exp(jnp.where(mask, c_b - big_m, -jnp.inf))
    st = (sr[:ML_CHUNK] * pt).astype(BF16)
    kw = (ks * jnp.exp(c_b - m_tot)).astype(BF16)
    ho = _dot(vt_ext, jnp.concatenate([st, kw], axis=-1))
    ht = ho[:, :ML_CHUNK] + jnp.exp(m - big_m) * sr[ML_CHUNK:]
    den = ht[ML_DH:ML_DH + 1]
    scale = 1.0 / jnp.maximum(jnp.abs(den), jnp.exp(-(b_r + big_m)))
    o_ref[0, pl.ds(row, ML_CHUNK), hs] = (ht[:ML_DH] * scale).T.astype(o_ref.dtype)
    cnt_ref[n] = jnp.exp(m - m_tot) * cnt + ho[:, ML_CHUNK:]
    m_ref[n] = g_tot + m_tot


def _mlstm_kernel(qf, kf, vf, grf, qb, kb, vb, grb, bias_r, eye_ref, tri_pre, tri_suf, of, ob,
                  cnt_ref, m_ref, b_ref, cmax_ref, ctot_ref, gtot_ref, ccol_ref):
    c = pl.program_id(1)

    @pl.when(c == 0)
    def _():
        cnt_ref[...] = jnp.zeros_like(cnt_ref)
        m_ref[...] = jnp.zeros_like(m_ref)

    stats = (b_ref, cmax_ref, ctot_ref, gtot_ref, ccol_ref)
    _mlstm_prologue(grf, grb, bias_r, eye_ref[...], tri_pre[...], tri_suf[...], *stats)
    n_chunks = ML_SEQ_TILE // ML_CHUNK

    def body(j, carry):
        jb = n_chunks - 1 - j
        rf = pl.multiple_of(j * ML_CHUNK, ML_CHUNK)
        rb = pl.multiple_of(jb * ML_CHUNK, ML_CHUNK)
        for h in range(ML_HEADS):
            _mlstm_chunk(qf, kf, vf, of, cnt_ref, m_ref, h, rf, j, stats, True)
            _mlstm_chunk(qb, kb, vb, ob, cnt_ref, m_ref, ML_HEADS + h, rb, jb, stats, False)
        return carry

    lax.fori_loop(0, n_chunks, body, 0)


def _mlstm(q, k, v, gates, if_bias):
    bsz, s, _ = q.shape
    ts = ML_SEQ_TILE
    ns = s // ts
    n_chunks = ts // ML_CHUNK
    n_gate = 4 * ML_HEADS
    n_dir = 2 * ML_HEADS
    gates_r = gates.reshape(bsz, s // ML_CHUNK, ML_CHUNK, n_gate).transpose(0, 1, 3, 2)
    fwd = lambda b, c: (b, c, 0)
    bwd = lambda b, c: (b, ns - 1 - c, 0)
    fwd4 = lambda b, c: (b, c, 0, 0)
    bwd4 = lambda b, c: (b, ns - 1 - c, 0, 0)

    def seq_specs(imap, imap4):
        return [pl.BlockSpec((1, ts, ML_W), imap), pl.BlockSpec((1, ts, ML_W), imap),
                pl.BlockSpec((1, ts, ML_W), imap),
                pl.BlockSpec((1, n_chunks, n_gate, ML_CHUNK), imap4)]

    in_specs = seq_specs(fwd, fwd4) + seq_specs(bwd, bwd4) + [
        _const_spec((n_gate, ML_CHUNK))] + [_const_spec((ML_CHUNK, ML_CHUNK))] * 3
    idx = jnp.arange(ML_CHUNK)
    tri_pre = (idx[:, None] <= idx[None, :]).astype(BF16)
    tri_suf = (idx[:, None] >= idx[None, :]).astype(BF16)
    out_specs = [pl.BlockSpec((1, ts, ML_W), fwd), pl.BlockSpec((1, ts, ML_W), bwd)]
    out_shape = [jax.ShapeDtypeStruct((bsz, s, ML_W), BF16)] * 2
    stat = pltpu.VMEM((n_chunks, n_dir, ML_CHUNK), F32)
    return pl.pallas_call(
        _mlstm_kernel,
        grid=(bsz, ns),
        in_specs=in_specs,
        out_specs=out_specs,
        out_shape=out_shape,
        scratch_shapes=[pltpu.VMEM((n_dir, 2 * ML_DH, ML_DH), F32), pltpu.VMEM((n_dir, 1, 1), F32),
                        stat, stat, stat, stat, pltpu.VMEM((n_chunks, ML_CHUNK, n_dir), F32)],
        compiler_params=_cparams(("parallel", "arbitrary")),
        name="mlstm_scan",
    )(q, k, v, gates_r, q, k, v, gates_r,
      jnp.broadcast_to(if_bias[:, None], (n_gate, ML_CHUNK)), jnp.eye(ML_CHUNK, dtype=BF16),
      tri_pre, tri_suf)


def _odd_out_kernel(final, att, mgate, hf, hb, mo, lgate, ng, wa, wb, x, fg, o_ref):
    mla_out = (att[0].astype(F32) * _silu(mgate[0].astype(F32))).astype(BF16)
    hm = hf[0].astype(F32) + hb[0].astype(F32)
    mo_v, lg_v = mo[0].astype(F32), lgate[0].astype(F32)
    parts = []
    for h in range(ML_HEADS):
        hs = slice(h * ML_DH, (h + 1) * ML_DH)
        y = _rms(hm[:, hs], ng[...]) * _sigmoid(mo_v[:, hs])
        parts.append((y * _silu(lg_v[:, hs])).astype(BF16))
    ml_out = jnp.concatenate(parts, axis=-1)
    xn = x[0] + (_dot(mla_out, wa[...]) + _dot(ml_out, wb[...]))
    o_ref[0] = _rms(xn, fg[...]) if final else xn


def _odd_out(att, mla_gate, h_f, h_b, mo, ml_gate, norm_g, w_out, x, final_g, final):
    bsz, s, _ = x.shape
    tm = TAIL_TILE
    cur = lambda b, i: (b, i, 0)
    half = pl.BlockSpec((1, tm, MLA_W), cur)
    in_specs = [half] * 6 + [
        _const_spec((1, ML_DH)), _const_spec((MLA_W, D_MODEL)), _const_spec((ML_W, D_MODEL)),
        pl.BlockSpec((1, tm, D_MODEL), cur), _const_spec((1, D_MODEL))]
    return pl.pallas_call(
        functools.partial(_odd_out_kernel, final),
        grid=(bsz, s // tm),
        in_specs=in_specs,
        out_specs=pl.BlockSpec((1, tm, D_MODEL), cur),
        out_shape=jax.ShapeDtypeStruct((bsz, s, D_MODEL), F32),
        compiler_params=_cparams(("parallel", "parallel")),
        name="odd_out",
    )(att, mla_gate, h_f, h_b, mo, ml_gate, norm_g[None, :],
      w_out[:MLA_W].astype(BF16), w_out[MLA_W:].astype(BF16), x, final_g[None, :])


def _col_split(w, sizes):
    out, off = [], 0
    for n in sizes:
        out.append(w[:, off:off + n].astype(BF16))
        off += n
    return out


def _even_layer(x, g, w_in, a_up, a_bias, gla_norm_g, pool_w, pool_scale, w_out):
    bsz, s, _ = x.shape
    sizes = (GLA_K_TOT, GLA_K_TOT, GLA_V_TOT, GLA_V_TOT, 2 * GLA_LR, POOL_W, POOL_W)
    dts = (F32, F32, BF16, BF16, BF16, F32, BF16)
    outs = _norm_proj(x.reshape(bsz * s, D_MODEL), g[None, :], _col_split(w_in, sizes), dts)
    q, k, v, gate, lr, pool_u, pool_gate = [o.reshape(bsz, s, -1) for o in outs]
    o_f, o_b = _gla(q, k, v, lr, a_up, a_bias)
    return _even_out(o_f, o_b, gate, gla_norm_g, pool_u, pool_gate, pool_w, pool_scale, w_out, x)


def _odd_layer(x, g, tabs, w_in, q_norm_g, q_up, kv_norm_g, kv_up, if_bias, ml_norm_g, w_out,
               final_g, final):
    bsz, s, _ = x.shape
    half = MLA_ROPE // 2
    off = MLA_Q_LORA + MLA_KV_LORA
    kr_w = w_in[:, off:off + MLA_ROPE]
    n_gate = 4 * ML_HEADS
    gate_off = off + MLA_ROPE + MLA_W + 4 * ML_W
    w_lat = jnp.concatenate(
        [w_in[:, :off + MLA_ROPE], -kr_w[:, half:], kr_w[:, :half], w_in[:, gate_off:gate_off + n_gate],
         jnp.zeros((D_MODEL, MLA_MISC_W - 2 * MLA_ROPE - n_gate), F32)], axis=-1).astype(BF16)
    wide = _col_split(w_in[:, off + MLA_ROPE:gate_off], (MLA_W, ML_W, ML_W, ML_W, ML_W))
    wide.append(w_in[:, gate_off + n_gate:].astype(BF16))
    dts = (F32, BF16, BF16, F32, BF16, BF16, BF16)
    outs = _norm_proj(x.reshape(bsz * s, D_MODEL), g[None, :], [w_lat] + wide, dts)
    lat, mla_gate, mq, mk, mv, mo, ml_gate = [o.reshape(bsz, s, -1) for o in outs]
    mif = lat[:, :, off + 2 * MLA_ROPE:off + 2 * MLA_ROPE + n_gate]
    wq, wkn, place, wvt = _mla_weights(q_up, kv_up)
    qa, ka, vt = _mla_qkv(lat, tabs, q_norm_g, kv_norm_g, wq, wkn, place, wvt)
    att = _mla_attn(qa, ka, vt)
    h_f, h_b = _mlstm(mq, mk, mv, mif, if_bias)
    return _odd_out(att, mla_gate, h_f, h_b, mo, ml_gate, ml_norm_g, w_out, x, final_g, final)


def _trunk(x, norm_g, final_norm_g, e_w_in, e_gla_a_up, e_gla_a_bias, e_gla_norm_g, e_pool_w,
           e_pool_scale, e_w_out, o_w_in, o_q_norm_g, o_q_up, o_kv_norm_g, o_kv_up, o_if_bias,
           o_mlstm_norm_g, o_w_out):
    depth = norm_g.shape[0]
    assert depth % 2 == 0, "the final RMSNorm is fused into the last (odd) layer's tail"
    tabs = _rope_tables(x.shape[1])
    for layer in range(depth):
        i = layer // 2
        if layer % 2 == 0:
            x = _even_layer(x, norm_g[layer], e_w_in[i], e_gla_a_up[i], e_gla_a_bias[i],
                            e_gla_norm_g[i], e_pool_w[i], e_pool_scale[i], e_w_out[i])
        else:
            x = _odd_layer(x, norm_g[layer], tabs, o_w_in[i], o_q_norm_g[i], o_q_up[i],
                           o_kv_norm_g[i], o_kv_up[i], o_if_bias[i], o_mlstm_norm_g[i], o_w_out[i],
                           final_norm_g, layer == depth - 1)
    return x


def kernel(x_prompt, x_sample, norm_g, final_norm_g, e_w_in, e_gla_a_up, e_gla_a_bias, e_gla_norm_g,
           e_pool_w, e_pool_scale, e_w_out, o_w_in, o_q_norm_g, o_q_up, o_kv_norm_g, o_kv_up,
           o_if_bias, o_mlstm_norm_g, o_w_out):
    params = (norm_g, final_norm_g, e_w_in, e_gla_a_up, e_gla_a_bias, e_gla_norm_g, e_pool_w,
              e_pool_scale, e_w_out, o_w_in, o_q_norm_g, o_q_up, o_kv_norm_g, o_kv_up, o_if_bias,
              o_mlstm_norm_g, o_w_out)
    return (_trunk(x_prompt, *params), _trunk(x_sample, *params))
```
